```python
import math
import jax, jax.numpy as jnp
from jax import lax
import numpy as np

D_MODEL = 1024
BATCH = 8
SEQ = 2048
DEPTH = 1

GDN_HEADS = 8
GDN_HEAD_DIM = 128
GDN_WIDTH = GDN_HEADS * GDN_HEAD_DIM
GDN_CHUNK = 64
CONV_K = 5
DIL_GROUPS = ((128, 1), (512, 4), (2048, 16))
DIL_N_GROUPS = 3
DIL_HEADS_PER_GROUP = 4
DIL_HEADS = DIL_N_GROUPS * DIL_HEADS_PER_GROUP
DIL_HEAD_DIM = 128
DIL_WIDTH = DIL_HEADS * DIL_HEAD_DIM
DIL_OUT_WIDTH = DIL_HEADS_PER_GROUP * DIL_HEAD_DIM
DIL_BLOCK = 64
REL_BUCKETS = 32
REL_MAX_DIST = 1024
D_FF = 4 * D_MODEL
EPS = 1e-6
NEG = -1e30

C_QA = 0
C_KA = C_QA + GDN_WIDTH
C_VA = C_KA + GDN_WIDTH
C_ZA = C_VA + GDN_WIDTH
C_AF = C_ZA + GDN_WIDTH
C_AB = C_AF + GDN_HEADS
C_BF = C_AB + GDN_HEADS
C_BB = C_BF + GDN_HEADS
C_QB = C_BB + GDN_HEADS
C_KB = C_QB + DIL_WIDTH
C_VB = C_KB + DIL_WIDTH
C_GA = C_VB + DIL_WIDTH
C_GB = C_GA + D_MODEL
IN_COLS = C_GB + D_MODEL

kernel_name = "hybrid_gdn_dilated_attn_block"


def _rmsnorm(x, w):
    xf = x.astype(jnp.float32)
    y = xf * lax.rsqrt(jnp.mean(xf * xf, axis=-1, keepdims=True) + EPS) * w.astype(jnp.float32)
    return y.astype(x.dtype)


def _l2norm(x):
    return x * lax.rsqrt(jnp.sum(x * x, axis=-1, keepdims=True) + EPS)


def _t5_bucket(rel):
    nb = REL_BUCKETS // 2
    ret = (rel > 0).astype(np.int32) * nb
    n = np.abs(rel)
    max_exact = nb // 2
    large = max_exact + (np.log(np.maximum(n, 1) / max_exact) / math.log(REL_MAX_DIST / max_exact)
                         * (nb - max_exact)).astype(np.int32)
    large = np.minimum(large, nb - 1)
    return ret + np.where(n < max_exact, n, large).astype(np.int32)


def _chunk_gated_delta(q, k, v, g, beta):
    b, s, h, dk = q.shape
    dv = v.shape[-1]
    n = s // GDN_CHUNK

    def chunks(t):
        return jnp.moveaxis(t.reshape(b, n, GDN_CHUNK, h, -1), 3, 1)

    qc, kc, vc = chunks(q), chunks(k), chunks(v)
    gc = jnp.cumsum(chunks(g[..., None])[..., 0], axis=-1)
    bc = chunks(beta[..., None])
    tri = np.tril(np.ones((GDN_CHUNK, GDN_CHUNK), bool))
    strict = np.tril(np.ones((GDN_CHUNK, GDN_CHUNK), bool), -1)
    dd = gc[..., :, None] - gc[..., None, :]
    gam = jnp.where(tri, jnp.exp(jnp.where(tri, dd, 0.0)), 0.0)
    kb = kc * bc
    a_kk = jnp.where(strict, jnp.einsum('bhnic,bhnjc->bhnij', kb, kc) * gam, 0.0)
    eye = jnp.eye(GDN_CHUNK, dtype=a_kk.dtype)
    rhs = jnp.concatenate([vc * bc, kb * jnp.exp(gc)[..., None]], axis=-1)
    sol = lax.linalg.triangular_solve(a_kk + eye, rhs, left_side=True, lower=True,
                                      unit_diagonal=True)
    u, w = sol[..., :dv], sol[..., dv:]
    a_qk = jnp.einsum('bhnic,bhnjc->bhnij', qc, kc) * gam

    def step(state, inp):
        qn, kn, un, wn, gn, aqk = inp
        v_new = un - jnp.einsum('bhck,bhkv->bhcv', wn, state)
        o = (jnp.einsum('bhck,bhkv->bhcv', qn * jnp.exp(gn)[..., None], state)
             + jnp.einsum('bhij,bhjv->bhiv', aqk, v_new))
        g_last = gn[..., -1]
        state = (state * jnp.exp(g_last)[..., None, None]
                 + jnp.einsum('bhck,bhcv->bhkv', kn * jnp.exp(g_last[..., None] - gn)[..., None], v_new))
        return state, o

    xs = tuple(jnp.moveaxis(t, 2, 0) for t in (qc, kc, u, w, gc, a_qk))
    state0 = jnp.zeros((b, h, dk, dv), jnp.float32)
    _, o = lax.scan(step, state0, xs)
    o = jnp.moveaxis(o, 0, 2)
    return jnp.moveaxis(o, 1, 3).reshape(b, s, h, dv)


def _dilated_group(q, k, v, bias, dil, half):
    b, s, h, hd = q.shape
    blk = DIL_BLOCK
    L = s // dil
    nb = -(-L // blk)
    lp = nb * blk

    def res(t):
        return t.reshape(b, L, dil, h, hd).transpose(0, 2, 1, 3, 4)

    qr = jnp.pad(res(q), ((0, 0), (0, 0), (0, lp - L), (0, 0), (0, 0))).reshape(b, dil, nb, blk, h, hd)

    def win(t):
        tp = jnp.pad(res(t), ((0, 0), (0, 0), (blk, lp - L + blk), (0, 0), (0, 0)))
        tp = tp.reshape(b, dil, nb + 2, blk, h, hd)
        return jnp.concatenate([tp[:, :, :-2], tp[:, :, 1:-1], tp[:, :, 2:]], axis=3)

    kw, vw = win(k), win(v)
    off = np.arange(3 * blk)[None, :] - blk - np.arange(blk)[:, None]
    band = np.abs(off) <= half
    key_pos = np.arange(nb)[:, None] * blk - blk + np.arange(3 * blk)[None, :]
    key_ok = (key_pos >= 0) & (key_pos < L)
    mask = band[None] & key_ok[:, None, :]
    logits = (jnp.einsum('brnqhc,brnkhc->brnhqk', qr, kw) * (hd ** -0.5)
              + bias.astype(jnp.float32))
    logits = jnp.where(mask[:, None], logits, NEG)
    lse = jax.nn.logsumexp(logits, axis=-1)
    p = jnp.exp(logits - lse[..., None])
    o = jnp.einsum('brnhqk,brnkhc->brnqhc', p, vw).reshape(b, dil, lp, h, hd)[:, :, :L]
    o = o.transpose(0, 2, 1, 3, 4).reshape(b, s, h, hd)
    lse = lse.transpose(0, 1, 2, 4, 3).reshape(b, dil, lp, h)[:, :, :L]
    lse = lse.transpose(0, 2, 1, 3).reshape(b, s, h)
    return o, lse


def _gdn_decay(a, a_log, dt_bias):
    return -jnp.exp(a_log.astype(jnp.float32)) * jax.nn.softplus(a + dt_bias.astype(jnp.float32))


def setup_inputs(seed: int = 0) -> dict:
    key = jax.random.key(seed)
    ks = jax.random.split(key, 24)
    f32 = jnp.float32

    def nrm(k, shape, scale):
        return jax.random.normal(k, shape, f32) * scale

    def gain(k, shape):
        return 1.0 + 0.05 * jax.random.normal(k, shape, f32)

    dt = jnp.exp(jax.random.uniform(ks[6], (2, DEPTH, GDN_HEADS), f32, math.log(1e-3), math.log(1e-1)))
    dt_bias = dt + jnp.log(-jnp.expm1(-dt))
    a_log = jnp.log(jax.random.uniform(ks[7], (2, DEPTH, GDN_HEADS), f32, 1.0, 16.0))
    return {
        "x": jax.random.normal(ks[0], (BATCH, SEQ, D_MODEL), f32),
        "rel_bias": nrm(ks[1], (REL_BUCKETS, DIL_HEADS), 0.5),
        "ln_mix_pre": gain(ks[2], (DEPTH, D_MODEL)),
        "w_in": nrm(ks[3], (DEPTH, D_MODEL, IN_COLS), D_MODEL ** -0.5),
        "conv_w": nrm(ks[4], (DEPTH, CONV_K, 3 * GDN_WIDTH), CONV_K ** -0.5),
        "a_log_f": a_log[0],
        "a_log_b": a_log[1],
        "dt_bias_f": dt_bias[0],
        "dt_bias_b": dt_bias[1],
        "norm_a": gain(ks[5], (DEPTH, GDN_HEAD_DIM)),
        "w_branch_a": nrm(ks[8], (DEPTH, GDN_WIDTH, D_MODEL), GDN_WIDTH ** -0.5),
        "w_branch_b": nrm(ks[9], (DEPTH, DIL_OUT_WIDTH, D_MODEL), DIL_OUT_WIDTH ** -0.5),
        "w_out": nrm(ks[10], (DEPTH, D_MODEL, D_MODEL), D_MODEL ** -0.5),
        "ln_mix_post": gain(ks[11], (DEPTH, D_MODEL)),
        "ln_mlp_pre": gain(ks[12], (DEPTH, D_MODEL)),
        "w_ff1": nrm(ks[13], (DEPTH, D_MODEL, D_FF), D_MODEL ** -0.5),
        "w_ff2": nrm(ks[14], (DEPTH, D_FF, D_MODEL), D_FF ** -0.5),
        "ln_mlp_post": gain(ks[15], (DEPTH, D_MODEL)),
    }


def reference(x, rel_bias, ln_mix_pre, w_in, conv_w, a_log_f, a_log_b, dt_bias_f, dt_bias_b,
              norm_a, w_branch_a, w_branch_b, w_out, ln_mix_post, ln_mlp_pre, w_ff1, w_ff2,
              ln_mlp_post):
    b, s, _ = x.shape
    f32 = jnp.float32
    blk = DIL_BLOCK
    off = np.arange(3 * blk)[None, :] - blk - np.arange(blk)[:, None]
    group_bias = []
    for gi, (window, dil) in enumerate(DIL_GROUPS):
        bt = rel_bias[_t5_bucket(off * dil)]
        hs = slice(gi * DIL_HEADS_PER_GROUP, (gi + 1) * DIL_HEADS_PER_GROUP)
        group_bias.append(jnp.transpose(bt[:, :, hs], (2, 0, 1)))

    for l in range(DEPTH):
        h = _rmsnorm(x, ln_mix_pre[l])
        proj = jnp.einsum('bsd,dc->bsc', h, w_in[l]).astype(f32)

        qkv = lax.conv_general_dilated(
            proj[..., C_QA:C_ZA], conv_w[l].astype(f32)[:, None, :], window_strides=(1,),
            padding=[(CONV_K // 2, CONV_K // 2)], dimension_numbers=('NWC', 'WIO', 'NWC'),
            feature_group_count=3 * GDN_WIDTH)
        qkv = jax.nn.silu(qkv)
        qa = _l2norm(qkv[..., :GDN_WIDTH].reshape(b, s, GDN_HEADS, GDN_HEAD_DIM)) * (GDN_HEAD_DIM ** -0.5)
        ka = _l2norm(qkv[..., GDN_WIDTH:2 * GDN_WIDTH].reshape(b, s, GDN_HEADS, GDN_HEAD_DIM))
        va = qkv[..., 2 * GDN_WIDTH:].reshape(b, s, GDN_HEADS, GDN_HEAD_DIM)
        g_f = _gdn_decay(proj[..., C_AF:C_AB], a_log_f[l], dt_bias_f[l])
        g_b = _gdn_decay(proj[..., C_AB:C_BF], a_log_b[l], dt_bias_b[l])
        beta_f = jax.nn.sigmoid(proj[..., C_BF:C_BB])
        beta_b = jax.nn.sigmoid(proj[..., C_BB:C_QB])
        flip = lambda t: jnp.flip(t, axis=1)
        o_a = (_chunk_gated_delta(qa, ka, va, g_f, beta_f)
               + flip(_chunk_gated_delta(flip(qa), flip(ka), flip(va), flip(g_b), flip(beta_b))))
        z = proj[..., C_ZA:C_AF].reshape(b, s, GDN_HEADS, GDN_HEAD_DIM)
        o_a = _rmsnorm(o_a, norm_a[l]) * jax.nn.silu(z)
        o_a = o_a.reshape(b, s, GDN_WIDTH).astype(x.dtype)

        qb = proj[..., C_QB:C_KB].reshape(b, s, DIL_HEADS, DIL_HEAD_DIM)
        kb = proj[..., C_KB:C_VB].reshape(b, s, DIL_HEADS, DIL_HEAD_DIM)
        vb = proj[..., C_VB:C_GA].reshape(b, s, DIL_HEADS, DIL_HEAD_DIM)
        outs, lses = [], []
        for gi, (window, dil) in enumerate(DIL_GROUPS):
            hs = slice(gi * DIL_HEADS_PER_GROUP, (gi + 1) * DIL_HEADS_PER_GROUP)
            o_g, lse_g = _dilated_group(qb[:, :, hs], kb[:, :, hs], vb[:, :, hs], group_bias[gi],
                                        dil, window // (2 * dil))
            outs.append(o_g)
            lses.append(lse_g)
        wgt = jax.nn.softmax(jnp.stack(lses, axis=0), axis=0)
        o_b = jnp.sum(wgt[..., None] * jnp.stack(outs, axis=0), axis=0)
        o_b = o_b.reshape(b, s, DIL_OUT_WIDTH).astype(x.dtype)

        gate_a = jax.nn.sigmoid(proj[..., C_GA:C_GB]).astype(x.dtype)
        gate_b = jax.nn.sigmoid(proj[..., C_GB:IN_COLS]).astype(x.dtype)
        merged = (gate_a * jnp.einsum('bsc,cd->bsd', o_a, w_branch_a[l])
                  + gate_b * jnp.einsum('bsc,cd->bsd', o_b, w_branch_b[l]))
        y = jnp.einsum('bsd,de->bse', merged, w_out[l])
        x = x + _rmsnorm(y, ln_mix_post[l])

        h2 = _rmsnorm(x, ln_mlp_pre[l])
        f = jnp.square(jax.nn.relu(jnp.einsum('bsd,df->bsf', h2, w_ff1[l])))
        f = jnp.einsum('bsf,fd->bsd', f, w_ff2[l])
        x = x + _rmsnorm(f, ln_mlp_post[l])
    return x
```

```python
import functools
import math

import numpy as np
import jax
import jax.numpy as jnp
from jax import lax
from jax.experimental import pallas as pl
from jax.experimental.pallas import tpu as pltpu

F32 = jnp.float32
BF16 = jnp.bfloat16
HIGHEST = lax.Precision.HIGHEST

D_MODEL = 1024
GDN_HEADS = 8
HEAD_DIM = 128
GDN_WIDTH = GDN_HEADS * HEAD_DIM
CHUNK = 64
CONV_K = 5
DIL_GROUPS = ((128, 1), (512, 4), (2048, 16))
DIL_HEADS_PER_GROUP = 4
DIL_HEADS = 12
DIL_WIDTH = DIL_HEADS * HEAD_DIM
DIL_OUT_WIDTH = DIL_HEADS_PER_GROUP * HEAD_DIM
HALF_WINDOW = 64
QBLK = 128
KWIN = QBLK + 2 * HALF_WINDOW
REL_BUCKETS = 32
REL_MAX_DIST = 1024
D_FF = 4 * D_MODEL
EPS = 1e-6
NEG = -1e30
N_SMALL = 4 * GDN_HEADS
LANES = 128

CB_QA, CB_KA, CB_VA, CB_ZA = 0, 8, 16, 24
CB_GA, CB_GB = 32, 40
CB_QB, CB_KB, CB_VB = 48, 60, 72
N_WIDE = 84 * LANES

VMEM_LIMIT = 56 * 1024 * 1024


def _mm(a, b):
    return jnp.dot(a, b, preferred_element_type=F32)


def _mm_nt(a, b):
    return lax.dot_general(a, b, (((1,), (1,)), ((), ())), preferred_element_type=F32)


def _mm_hi(a, b):
    return jnp.dot(a, b, preferred_element_type=F32, precision=HIGHEST)


def _rms(x, gain):
    return x * lax.rsqrt(jnp.mean(x * x, axis=-1, keepdims=True) + EPS) * gain


def _sigmoid(x):
    return 1.0 / (1.0 + jnp.exp(-x))


def _softplus(x):
    return jnp.maximum(x, 0.0) + jnp.log(1.0 + jnp.exp(-jnp.abs(x)))


def _norm_matmul_body(x_ref, g_ref, w_ref, o_ref, h_ref):
    @pl.when(pl.program_id(1) == 0)
    def _():
        h_ref[...] = _rms(x_ref[...], g_ref[...]).astype(BF16)

    o_ref[...] = _mm(h_ref[...], w_ref[...]).astype(o_ref.dtype)


def _norm_matmul(x2d, gain, w, out_dtype, tm, tn, name):
    t, d = x2d.shape
    n = w.shape[1]
    return pl.pallas_call(
        _norm_matmul_body,
        grid=(t // tm, n // tn),
        in_specs=[pl.BlockSpec((tm, d), lambda i, j: (i, 0)),
                  pl.BlockSpec((1, d), lambda i, j: (0, 0)),
                  pl.BlockSpec((d, tn), lambda i, j: (0, j))],
        out_specs=pl.BlockSpec((tm, tn), lambda i, j: (i, j)),
        out_shape=jax.ShapeDtypeStruct((t, n), out_dtype),
        scratch_shapes=[pltpu.VMEM((tm, d), BF16)],
        compiler_params=pltpu.CompilerParams(dimension_semantics=("parallel", "arbitrary"),
                                             vmem_limit_bytes=VMEM_LIMIT),
        name=name,
    )(x2d, gain, w)


def _unit_tri_inverse(a, eye):
    p = eye - a
    pw = a
    for _ in range(int(math.log2(CHUNK)) - 1):
        pw = _mm_hi(pw, pw)
        p = p + _mm_hi(p, pw)
    return p


def _gdn_body(par_ref, cwq_ref, cwk_ref, cwv_ref, q_ref, k_ref, v_ref, z_ref, small_ref, rows_ref,
              na_ref, o_ref,
              xp, qn, kn, vn, gcol, bcol, gcrow, brow, wq_s, ak_s, u_s, dec_s, of_s, ob_s):
    s = q_ref.shape[0]
    nchunk = s // CHUNK
    h = pl.program_id(1)
    rc = 256

    zeros8 = jnp.zeros((8, HEAD_DIM), F32)
    xp[pl.ds(0, 8), :] = zeros8
    xp[pl.ds(8 + s, 8), :] = zeros8

    def conv_into(src_ref, cw_ref, dst_ref, l2, scale):
        xp[pl.ds(8, s), :] = src_ref[...].astype(F32)
        for c in range(s // rc):
            acc = None
            for j in range(CONV_K):
                t = xp[pl.ds(c * rc + 8 - CONV_K // 2 + j, rc), :] * cw_ref[j:j + 1, :]
                acc = t if acc is None else acc + t
            y = acc * _sigmoid(acc)
            if l2:
                y = y * lax.rsqrt(jnp.sum(y * y, axis=-1, keepdims=True) + EPS)
                if scale != 1.0:
                    y = y * scale
            dst_ref[pl.ds(c * rc, rc), :] = y

    conv_into(q_ref, cwq_ref, qn, True, HEAD_DIM ** -0.5)
    conv_into(k_ref, cwk_ref, kn, True, 1.0)
    conv_into(v_ref, cwv_ref, vn, False, 1.0)

    lane = lax.broadcasted_iota(jnp.int32, (rc, LANES), 1)
    ii = lax.broadcasted_iota(jnp.int32, (CHUNK, CHUNK), 0)
    jj = lax.broadcasted_iota(jnp.int32, (CHUNK, CHUNK), 1)
    for d in range(2):
        neg_a = -jnp.exp(par_ref[2 * d:2 * d + 1, 0:1])
        dtb = par_ref[2 * d + 1:2 * d + 2, 0:1]
        for c in range(s // rc):
            slab = small_ref[pl.ds(c * rc, rc), :]
            a_col = jnp.sum(jnp.where(lane == d * GDN_HEADS + h, slab, 0.0), axis=1, keepdims=True)
            b_col = jnp.sum(jnp.where(lane == (2 + d) * GDN_HEADS + h, slab, 0.0), axis=1, keepdims=True)
            gcol[d, pl.ds(c * rc, rc), :] = neg_a * _softplus(a_col + dtb)
            bcol[d, pl.ds(c * rc, rc), :] = _sigmoid(b_col)
        g_row = neg_a * _softplus(rows_ref[d * GDN_HEADS + h] + dtb)
        cum = jnp.where((ii <= jj) if d == 0 else (ii >= jj), 1.0, 0.0)
        gcrow[d] = _mm_hi(g_row, cum)
        brow[d] = _sigmoid(rows_ref[(2 + d) * GDN_HEADS + h])

    eye = jnp.where(ii == jj, 1.0, 0.0)
    tri = ((ii >= jj), (ii <= jj))
    strict = ((ii > jj), (ii < jj))
    cum_col = (jnp.where(jj <= ii, 1.0, 0.0), jnp.where(jj >= ii, 1.0, 0.0))

    def phase1(n, carry):
        rows = pl.ds(pl.multiple_of(n * CHUNK, CHUNK), CHUNK)
        q = qn[rows, :]
        k = kn[rows, :]
        kb = k.astype(BF16)
        vb = vn[rows, :].astype(BF16)
        kk = _mm_nt(kb, kb)
        qk = _mm_nt(q.astype(BF16), kb)
        for d in range(2):
            gcc = _mm_hi(cum_col[d], jnp.broadcast_to(gcol[d, rows, :], (CHUNK, LANES)))
            gcr = gcrow[d, pl.ds(n, 1), :]
            bc = bcol[d, rows, :]
            br = brow[d, pl.ds(n, 1), :]
            dd = gcc[:, :CHUNK] - gcr
            gam = jnp.where(tri[d], jnp.exp(jnp.where(tri[d], dd, 0.0)), 0.0)
            a = jnp.where(strict[d], kk * bc * gam, 0.0)
            t = _unit_tri_inverse(a, eye)
            tb = t * br
            tw = tb * jnp.exp(gcr)
            u_s[d, n] = _mm(tb.astype(BF16), vb)
            w = _mm(tw.astype(BF16), kb)
            aqk = qk * gam
            gtot = gcr[:, CHUNK - 1:CHUNK] if d == 0 else gcr[:, 0:1]
            qg = q * jnp.exp(gcc)
            kg = k * jnp.exp(gtot - gcc)
            wq_s[d, n, pl.ds(0, CHUNK), :] = w.astype(BF16)
            wq_s[d, n, pl.ds(CHUNK, CHUNK), :] = qg.astype(BF16)
            ak_s[d, n, pl.ds(0, CHUNK), :] = aqk.astype(BF16)
            ak_s[d, n, pl.ds(CHUNK, HEAD_DIM), :] = kg.T.astype(BF16)
            dec_s[d, n] = jnp.broadcast_to(jnp.exp(gtot), (1, HEAD_DIM))
        return carry

    lax.fori_loop(0, nchunk, phase1, 0)

    def phase2(step, states):
        new_states = []
        for d in range(2):
            n = step if d == 0 else nchunk - 1 - step
            st = states[d]
            r1 = _mm(wq_s[d, n], st.astype(BF16))
            v_new = u_s[d, n] - r1[:CHUNK]
            r2 = _mm(ak_s[d, n], v_new.astype(BF16))
            o = r1[CHUNK:] + r2[:CHUNK]
            rows = pl.ds(pl.multiple_of(n * CHUNK, CHUNK), CHUNK)
            if d == 0:
                of_s[rows, :] = o
            else:
                ob_s[rows, :] = o
            new_states.append(st * dec_s[d, n] + r2[CHUNK:])
        return tuple(new_states)

    zero_state = jnp.zeros((HEAD_DIM, HEAD_DIM), F32)
    lax.fori_loop(0, nchunk, phase2, (zero_state, zero_state))

    for c in range(s // rc):
        rows = pl.ds(c * rc, rc)
        o = of_s[rows, :] + ob_s[rows, :]
        z = z_ref[rows, :].astype(F32)
        o_ref[rows, :] = (_rms(o, na_ref[...]) * (z * _sigmoid(z))).astype(o_ref.dtype)


def _gdn(proj, small, rows, par, conv_w, norm_a):
    b, s, _ = proj.shape
    nchunk = s // CHUNK
    col = lambda cb: pl.BlockSpec((None, s, HEAD_DIM), lambda i, h, cb=cb: (i, 0, cb + h))
    cw = lambda cb: pl.BlockSpec((CONV_K, HEAD_DIM), lambda i, h, cb=cb: (0, cb + h))
    return pl.pallas_call(
        _gdn_body,
        grid=(b, GDN_HEADS),
        in_specs=[pl.BlockSpec((None, 8, LANES), lambda i, h: (h, 0, 0)),
                  cw(0), cw(GDN_HEADS), cw(2 * GDN_HEADS),
                  col(CB_QA), col(CB_KA), col(CB_VA), col(CB_ZA),
                  pl.BlockSpec((None, s, LANES), lambda i, h: (i, 0, 0)),
                  pl.BlockSpec((None, N_SMALL, nchunk, CHUNK), lambda i, h: (i, 0, 0, 0)),
                  pl.BlockSpec((1, HEAD_DIM), lambda i, h: (0, 0))],
        out_specs=pl.BlockSpec((None, s, HEAD_DIM), lambda i, h: (i, 0, h)),
        out_shape=jax.ShapeDtypeStruct((b, s, GDN_WIDTH), BF16),
        scratch_shapes=[
            pltpu.VMEM((s + 16, HEAD_DIM), F32),
            pltpu.VMEM((s, HEAD_DIM), F32),
            pltpu.VMEM((s, HEAD_DIM), F32),
            pltpu.VMEM((s, HEAD_DIM), F32),
            pltpu.VMEM((2, s, 1), F32),
            pltpu.VMEM((2, s, 1), F32),
            pltpu.VMEM((2, nchunk, CHUNK), F32),
            pltpu.VMEM((2, nchunk, CHUNK), F32),
            pltpu.VMEM((2, nchunk, 2 * CHUNK, HEAD_DIM), BF16),
            pltpu.VMEM((2, nchunk, CHUNK + HEAD_DIM, CHUNK), BF16),
            pltpu.VMEM((2, nchunk, CHUNK, HEAD_DIM), F32),
            pltpu.VMEM((2, nchunk, 1, HEAD_DIM), F32),
            pltpu.VMEM((s, HEAD_DIM), F32),
            pltpu.VMEM((s, HEAD_DIM), F32),
        ],
        compiler_params=pltpu.CompilerParams(dimension_semantics=("parallel", "arbitrary"),
                                             vmem_limit_bytes=VMEM_LIMIT),
        name="gdn",
    )(par, conv_w, conv_w, conv_w, proj, proj, proj, proj, small, rows, norm_a)


def _dil_body(q0, k0, v0, q1, k1, v1, q2, k2, v2, bias_ref, o_ref,
              qf, kf, vf, qs, kp, vp, acc_s, m_s, l_s):
    s = q0.shape[0]
    scale = HEAD_DIM ** -0.5
    qi = lax.broadcasted_iota(jnp.int32, (QBLK, KWIN), 0)
    kj = lax.broadcasted_iota(jnp.int32, (QBLK, KWIN), 1)
    off = kj - HALF_WINDOW - qi
    band = (off >= -HALF_WINDOW) & (off <= HALF_WINDOW)
    zpad = jnp.zeros((HALF_WINDOW, HEAD_DIM), F32)

    for g, (q_ref, k_ref, v_ref) in enumerate(((q0, k0, v0), (q1, k1, v1), (q2, k2, v2))):
        dil = DIL_GROUPS[g][1]
        length = s // dil
        nblk = length // QBLK
        qf[...] = q_ref[...].astype(F32)
        kf[...] = k_ref[...].astype(F32)
        vf[...] = v_ref[...].astype(F32)
        kp[pl.ds(0, HALF_WINDOW), :] = zpad
        vp[pl.ds(0, HALF_WINDOW), :] = zpad
        kp[pl.ds(HALF_WINDOW + length, HALF_WINDOW), :] = zpad
        vp[pl.ds(HALF_WINDOW + length, HALF_WINDOW), :] = zpad
        bias = bias_ref[g]

        def residue(r, carry, g=g, dil=dil, length=length, nblk=nblk, bias=bias):
            cls = pl.ds(r, length, stride=dil) if dil > 1 else pl.ds(0, length)
            qs[pl.ds(0, length), :] = qf[cls, :]
            kp[pl.ds(HALF_WINDOW, length), :] = kf[cls, :]
            vp[pl.ds(HALF_WINDOW, length), :] = vf[cls, :]

            def block(i, carry2):
                j0 = pl.multiple_of(i * QBLK, QBLK)
                qb = qs[pl.ds(j0, QBLK), :].astype(BF16)
                kw = kp[pl.ds(j0, KWIN), :].astype(BF16)
                vw = vp[pl.ds(j0, KWIN), :].astype(BF16)
                kpos = j0 - HALF_WINDOW + kj
                ok = band & (kpos >= 0) & (kpos < length)
                logits = jnp.where(ok, _mm_nt(qb, kw) * scale + bias, NEG)
                m = jnp.max(logits, axis=-1, keepdims=True)
                p = jnp.exp(logits - m)
                lsum = jnp.sum(p, axis=-1, keepdims=True)
                acc = _mm(p.astype(BF16), vw)
                tok = pl.ds(r + dil * j0, QBLK, stride=dil) if dil > 1 else pl.ds(j0, QBLK)
                acc_s[g, tok, :] = acc
                m_s[g, tok, :] = m
                l_s[g, tok, :] = lsum
                return carry2

            lax.fori_loop(0, nblk, block, 0)
            return carry

        lax.fori_loop(0, dil, residue, 0)

    rc = 256
    for c in range(s // rc):
        rows = pl.ds(c * rc, rc)
        ms = [m_s[g, rows, :] for g in range(3)]
        mx = jnp.maximum(jnp.maximum(ms[0], ms[1]), ms[2])
        ws = [jnp.exp(ms[g] - mx) for g in range(3)]
        den = ws[0] * l_s[0, rows, :] + ws[1] * l_s[1, rows, :] + ws[2] * l_s[2, rows, :]
        num = ws[0] * acc_s[0, rows, :] + ws[1] * acc_s[1, rows, :] + ws[2] * acc_s[2, rows, :]
        o_ref[rows, :] = (num / den).astype(o_ref.dtype)


def _dilated(proj, bias):
    b, s, _ = proj.shape
    specs = []
    for g in range(3):
        for cb in (CB_QB, CB_KB, CB_VB):
            specs.append(pl.BlockSpec((None, s, HEAD_DIM),
                                      lambda i, h, cb=cb, g=g: (i, 0, cb + g * DIL_HEADS_PER_GROUP + h)))
    specs.append(pl.BlockSpec((None, 3, QBLK, KWIN), lambda i, h: (h, 0, 0, 0)))
    return pl.pallas_call(
        _dil_body,
        grid=(b, DIL_HEADS_PER_GROUP),
        in_specs=specs,
        out_specs=pl.BlockSpec((None, s, HEAD_DIM), lambda i, h: (i, 0, h)),
        out_shape=jax.ShapeDtypeStruct((b, s, DIL_OUT_WIDTH), BF16),
        scratch_shapes=[
            pltpu.VMEM((s, HEAD_DIM), F32),
            pltpu.VMEM((s, HEAD_DIM), F32),
            pltpu.VMEM((s, HEAD_DIM), F32),
            pltpu.VMEM((s, HEAD_DIM), F32),
            pltpu.VMEM((s + 2 * HALF_WINDOW, HEAD_DIM), F32),
            pltpu.VMEM((s + 2 * HALF_WINDOW, HEAD_DIM), F32),
            pltpu.VMEM((3, s, HEAD_DIM), F32),
            pltpu.VMEM((3, s, 1), F32),
            pltpu.VMEM((3, s, 1), F32),
        ],
        compiler_params=pltpu.CompilerParams(dimension_semantics=("parallel", "arbitrary"),
                                             vmem_limit_bytes=VMEM_LIMIT),
        name="dilated_attn",
    )(*([proj] * 9), bias)


def _mix_out_body(x_ref, oa_ref, ob_ref, ga_ref, gb_ref, wa_ref, wb_ref, wo_ref, ln_ref, o_ref):
    ya = _mm(oa_ref[...], wa_ref[...])
    yb = _mm(ob_ref[...], wb_ref[...])
    merged = _sigmoid(ga_ref[...].astype(F32)) * ya + _sigmoid(gb_ref[...].astype(F32)) * yb
    y = _mm(merged.astype(BF16), wo_ref[...])
    o_ref[...] = x_ref[...] + _rms(y, ln_ref[...])


def _mix_out(x2d, o_a, o_b, proj2d, w_a, w_b, w_o, ln_post, tm):
    t, d = x2d.shape
    assert (CB_GA * LANES) % d == 0 and (CB_GB * LANES) % d == 0
    nga, ngb = CB_GA * LANES // d, CB_GB * LANES // d
    const = lambda shape: pl.BlockSpec(shape, lambda i: (0, 0))
    return pl.pallas_call(
        _mix_out_body,
        grid=(t // tm,),
        in_specs=[pl.BlockSpec((tm, d), lambda i: (i, 0)),
                  pl.BlockSpec((tm, GDN_WIDTH), lambda i: (i, 0)),
                  pl.BlockSpec((tm, DIL_OUT_WIDTH), lambda i: (i, 0)),
                  pl.BlockSpec((tm, d), lambda i: (i, nga)),
                  pl.BlockSpec((tm, d), lambda i: (i, ngb)),
                  const((GDN_WIDTH, d)), const((DIL_OUT_WIDTH, d)), const((d, d)), const((1, d))],
        out_specs=pl.BlockSpec((tm, d), lambda i: (i, 0)),
        out_shape=jax.ShapeDtypeStruct((t, d), F32),
        compiler_params=pltpu.CompilerParams(dimension_semantics=("parallel",),
                                             vmem_limit_bytes=VMEM_LIMIT),
        name="mix_out",
    )(x2d, o_a, o_b, proj2d, proj2d, w_a, w_b, w_o, ln_post)


def _mlp_body(x_ref, g1_ref, w1_ref, w2_ref, g2_ref, o_ref, h_ref, acc_ref):
    k = pl.program_id(1)

    @pl.when(k == 0)
    def _():
        h_ref[...] = _rms(x_ref[...], g1_ref[...]).astype(BF16)

    f = jnp.maximum(_mm(h_ref[...], w1_ref[...]), 0.0)
    part = _mm((f * f).astype(BF16), w2_ref[...])

    @pl.when(k == 0)
    def _():
        acc_ref[...] = part

    @pl.when(k > 0)
    def _():
        acc_ref[...] += part

    @pl.when(k == pl.num_programs(1) - 1)
    def _():
        o_ref[...] = x_ref[...] + _rms(acc_ref[...], g2_ref[...])


def _mlp(x2d, ln_pre, w1, w2, ln_post, tm, tf):
    t, d = x2d.shape
    dff = w1.shape[1]
    return pl.pallas_call(
        _mlp_body,
        grid=(t // tm, dff // tf),
        in_specs=[pl.BlockSpec((tm, d), lambda i, k: (i, 0)),
                  pl.BlockSpec((1, d), lambda i, k: (0, 0)),
                  pl.BlockSpec((d, tf), lambda i, k: (0, k)),
                  pl.BlockSpec((tf, d), lambda i, k: (k, 0)),
                  pl.BlockSpec((1, d), lambda i, k: (0, 0))],
        out_specs=pl.BlockSpec((tm, d), lambda i, k: (i, 0)),
        out_shape=jax.ShapeDtypeStruct((t, d), F32),
        scratch_shapes=[pltpu.VMEM((tm, d), BF16), pltpu.VMEM((tm, d), F32)],
        compiler_params=pltpu.CompilerParams(dimension_semantics=("parallel", "arbitrary"),
                                             vmem_limit_bytes=VMEM_LIMIT),
        name="mlp",
    )(x2d, ln_pre, w1, w2, ln_post)


def _t5_bucket_np(rel):
    nb = REL_BUCKETS // 2
    ret = (rel > 0).astype(np.int32) * nb
    n = np.abs(rel)
    max_exact = nb // 2
    large = max_exact + (np.log(np.maximum(n, 1) / max_exact) / math.log(REL_MAX_DIST / max_exact)
                         * (nb - max_exact)).astype(np.int32)
    large = np.minimum(large, nb - 1)
    return ret + np.where(n < max_exact, n, large).astype(np.int32)


def _attention_bias(rel_bias):
    off = np.arange(KWIN)[None, :] - HALF_WINDOW - np.arange(QBLK)[:, None]
    off = np.clip(off, -HALF_WINDOW, HALF_WINDOW)
    per_group = []
    for gi, (_, dil) in enumerate(DIL_GROUPS):
        bt = rel_bias[_t5_bucket_np(off * dil)]
        per_group.append(bt[:, :, gi * DIL_HEADS_PER_GROUP:(gi + 1) * DIL_HEADS_PER_GROUP])
    return jnp.transpose(jnp.stack(per_group, axis=0), (3, 0, 1, 2)).astype(F32)


def kernel(x, rel_bias, ln_mix_pre, w_in, conv_w, a_log_f, a_log_b, dt_bias_f, dt_bias_b, norm_a,
           w_branch_a, w_branch_b, w_out, ln_mix_post, ln_mlp_pre, w_ff1, w_ff2, ln_mlp_post):
    b, s, d = x.shape
    t = b * s
    nchunk = s // CHUNK
    c_small = 4 * GDN_WIDTH
    bias = _attention_bias(rel_bias)
    for l in range(ln_mix_pre.shape[0]):
        x2d = x.reshape(t, d)
        w = w_in[l]
        c_qb = c_small + N_SMALL
        c_ga = c_qb + 3 * DIL_WIDTH
        w_wide = jnp.concatenate([w[:, :c_small], w[:, c_ga:], w[:, c_qb:c_ga]], axis=1).astype(BF16)
        w_small = jnp.pad(w[:, c_small:c_small + N_SMALL], ((0, 0), (0, LANES - N_SMALL))).astype(BF16)
        gain = ln_mix_pre[l][None, :]
        proj = _norm_matmul(x2d, gain, w_wide, BF16, 1024, 512, "in_proj_wide")
        small = _norm_matmul(x2d, gain, w_small, F32, 1024, LANES, "in_proj_small")

        proj3 = proj.reshape(b, s, N_WIDE)
        small3 = small.reshape(b, s, LANES)
        rows = jnp.transpose(small3[:, :, :N_SMALL], (0, 2, 1)).reshape(b, N_SMALL, nchunk, CHUNK)
        par = jnp.stack([a_log_f[l], dt_bias_f[l], a_log_b[l], dt_bias_b[l]], axis=1)
        par = jnp.broadcast_to(jnp.pad(par, ((0, 0), (0, 4)))[:, :, None], (GDN_HEADS, 8, LANES)).astype(F32)
        o_a = _gdn(proj3, small3, rows, par, conv_w[l].astype(F32), norm_a[l][None, :].astype(F32))
        o_b = _dilated(proj3, bias)

        x2d = _mix_out(x2d, o_a.reshape(t, GDN_WIDTH), o_b.reshape(t, DIL_OUT_WIDTH), proj,
                       w_branch_a[l].astype(BF16), w_branch_b[l].astype(BF16), w_out[l].astype(BF16),
                       ln_mix_post[l][None, :], 512)
        x2d = _mlp(x2d, ln_mlp_pre[l][None, :], w_ff1[l].astype(BF16), w_ff2[l].astype(BF16),
                   ln_mlp_post[l][None, :], 1024, 1024)
        x = x2d.reshape(b, s, d)
    return x
```

```python
import math

import numpy as np
import jax
import jax.numpy as jnp
from jax import lax
from jax.experimental import pallas as pl
from jax.experimental.pallas import tpu as pltpu

F32 = jnp.float32
BF16 = jnp.bfloat16
HIGHEST = lax.Precision.HIGHEST

D_MODEL = 1024
GDN_HEADS = 8
HEAD_DIM = 128
GDN_WIDTH = GDN_HEADS * HEAD_DIM
CHUNK = 64
CHUNK_GROUP = 4
CONV_K = 5
DIL_GROUPS = ((128, 1), (512, 4), (2048, 16))
DIL_HEADS_PER_GROUP = 4
DIL_HEADS = 12
DIL_WIDTH = DIL_HEADS * HEAD_DIM
DIL_OUT_WIDTH = DIL_HEADS_PER_GROUP * HEAD_DIM
HALF_WINDOW = 64
QBLK = 128
KWIN = QBLK + 2 * HALF_WINDOW
REL_BUCKETS = 32
REL_MAX_DIST = 1024
D_FF = 4 * D_MODEL
EPS = 1e-6
NEG = -1e30
N_SMALL = 4 * GDN_HEADS
LANES = 128
BIAS_LEN = 3 * LANES

CB_QA, CB_KA, CB_VA, CB_ZA = 0, 8, 16, 24
CB_GA, CB_GB = 32, 40
CB_QB, CB_KB, CB_VB = 48, 60, 72
N_WIDE = 84 * LANES

VMEM_LIMIT = 56 * 1024 * 1024


def _mm(a, b):
    return jnp.dot(a, b, preferred_element_type=F32)


def _mm_nt(a, b):
    return lax.dot_general(a, b, (((1,), (1,)), ((), ())), preferred_element_type=F32)


def _mm_hi(a, b):
    return jnp.dot(a, b, preferred_element_type=F32, precision=HIGHEST)


def _rms(x, gain):
    return x * lax.rsqrt(jnp.mean(x * x, axis=-1, keepdims=True) + EPS) * gain


def _sigmoid(x):
    return 1.0 / (1.0 + jnp.exp(-x))


def _softplus(x):
    return jnp.maximum(x, 0.0) + jnp.log(1.0 + jnp.exp(-jnp.abs(x)))


def _norm_matmul_body(x_ref, g_ref, w_ref, o_ref, h_ref):
    @pl.when(pl.program_id(1) == 0)
    def _():
        h_ref[...] = _rms(x_ref[...], g_ref[...]).astype(BF16)

    o_ref[...] = _mm(h_ref[...], w_ref[...]).astype(o_ref.dtype)


def _norm_matmul(x2d, gain, w, out_dtype, tm, tn, name):
    t, d = x2d.shape
    n = w.shape[1]
    return pl.pallas_call(
        _norm_matmul_body,
        grid=(t // tm, n // tn),
        in_specs=[pl.BlockSpec((tm, d), lambda i, j: (i, 0)),
                  pl.BlockSpec((1, d), lambda i, j: (0, 0)),
                  pl.BlockSpec((d, tn), lambda i, j: (0, j))],
        out_specs=pl.BlockSpec((tm, tn), lambda i, j: (i, j)),
        out_shape=jax.ShapeDtypeStruct((t, n), out_dtype),
        scratch_shapes=[pltpu.VMEM((tm, d), BF16)],
        compiler_params=pltpu.CompilerParams(dimension_semantics=("parallel", "arbitrary"),
                                             vmem_limit_bytes=VMEM_LIMIT),
        name=name,
    )(x2d, gain, w)


def _split_bf16(x):
    hi = x.astype(BF16)
    return hi, (x - hi.astype(F32)).astype(BF16)


def _split3_lanes(x):
    hi = x.astype(BF16)
    r = x - hi.astype(F32)
    mid = r.astype(BF16)
    lo = (r - mid.astype(F32)).astype(BF16)
    return jnp.concatenate([hi, mid, lo], axis=1)


def _block_diag(x, lo):
    return jnp.concatenate([jnp.where(lo, x, 0.0), jnp.where(lo, 0.0, x)], axis=0)


def _mm_split(lhs, rhs):
    lh, ll = _split_bf16(lhs)
    rh, rl = _split_bf16(rhs)
    return (_mm(jnp.concatenate([lh, ll], axis=1), jnp.concatenate([rh, rh], axis=0))
            + _mm(lh, rl))


def _unit_tri_inverse_pairs(a2s, eye2, lo):
    xs = [_mm_split(a2, _block_diag(a2, lo)) for a2 in a2s]
    qs = [eye2 - a2 for a2 in a2s]
    levels = int(math.log2(CHUNK)) - 1
    for lvl in range(levels):
        if lvl < levels - 1:
            outs = [_mm_split(jnp.concatenate([q, x], axis=0), _block_diag(x, lo)) for q, x in zip(qs, xs)]
            qs = [q + out[:CHUNK] for q, out in zip(qs, outs)]
            xs = [out[CHUNK:] for out in outs]
        else:
            qs = [q + _mm_split(q, _block_diag(x, lo)) for q, x in zip(qs, xs)]
    return qs


def _gdn_body(par_ref, lanepar_ref, cwq_ref, cwk_ref, cwv_ref, q_ref, k_ref, v_ref, z_ref, small_ref,
              rows_ref, na_ref, o_ref,
              xp, qn, kn, vn, act, gcf, bcf, gcrow, brow, mq_s, c_s, dec_s, of_s, ob_s):
    s = q_ref.shape[0]
    nchunk = s // CHUNK
    h = pl.program_id(1)
    rc = 256

    @pl.when(h == 0)
    def _():
        is_decay = lax.broadcasted_iota(jnp.int32, (rc, LANES), 1) < 2 * GDN_HEADS
        neg_a = -jnp.exp(lanepar_ref[0:1, :])
        dtb = lanepar_ref[1:2, :]
        for c in range(s // rc):
            rows = pl.ds(c * rc, rc)
            slab = small_ref[rows, :]
            act[rows, :] = jnp.where(is_decay, neg_a * _softplus(slab + dtb), _sigmoid(slab))

    zeros8 = jnp.zeros((8, HEAD_DIM), F32)
    xp[pl.ds(0, 8), :] = zeros8
    xp[pl.ds(8 + s, 8), :] = zeros8

    def conv_into(src_ref, cw_ref, dst_ref, l2, scale):
        xp[pl.ds(8, s), :] = src_ref[...].astype(F32)
        for c in range(s // rc):
            acc = None
            for j in range(CONV_K):
                t = xp[pl.ds(c * rc + 8 - CONV_K // 2 + j, rc), :] * cw_ref[j:j + 1, :]
                acc = t if acc is None else acc + t
            y = acc * _sigmoid(acc)
            if l2:
                y = y * lax.rsqrt(jnp.sum(y * y, axis=-1, keepdims=True) + EPS)
                if scale != 1.0:
                    y = y * scale
            dst_ref[pl.ds(c * rc, rc), :] = y

    conv_into(q_ref, cwq_ref, qn, True, HEAD_DIM ** -0.5)
    conv_into(k_ref, cwk_ref, kn, True, 1.0)
    conv_into(v_ref, cwv_ref, vn, False, 1.0)

    sel_k = lax.broadcasted_iota(jnp.int32, (LANES, 4 * LANES), 0)
    sel_c = lax.broadcasted_iota(jnp.int32, (LANES, 4 * LANES), 1)
    sel = jnp.where(sel_k == (sel_c >> 7) * GDN_HEADS + h, 1.0, 0.0).astype(BF16)
    sel3 = jnp.concatenate([sel, sel, sel], axis=0)
    bi = lax.broadcasted_iota(jnp.int32, (rc, rc), 0)
    bj = lax.broadcasted_iota(jnp.int32, (rc, rc), 1)
    same_chunk = jnp.bitwise_xor(bi, bj) < CHUNK
    cums = (jnp.where(same_chunk & (bj <= bi), 1.0, 0.0).astype(BF16),
            jnp.where(same_chunk & (bj >= bi), 1.0, 0.0).astype(BF16))
    for c in range(s // rc):
        rows = pl.ds(c * rc, rc)
        picked = _mm(_split3_lanes(act[rows, :]), sel3)
        for d in range(2):
            g = picked[:, d * LANES:(d + 1) * LANES]
            gc3 = _mm(cums[d], _split3_lanes(g))
            gcf[d, rows, :] = gc3[:, :LANES] + gc3[:, LANES:2 * LANES] + gc3[:, 2 * LANES:]
            bcf[d, rows, :] = picked[:, (2 + d) * LANES:(3 + d) * LANES]
    ri = lax.broadcasted_iota(jnp.int32, (LANES, LANES), 0)
    rj = lax.broadcasted_iota(jnp.int32, (LANES, LANES), 1)
    cum_row = jnp.where(((ri < CHUNK) & (rj < CHUNK) & (ri <= rj))
                        | ((ri >= CHUNK) & (rj >= CHUNK) & (ri >= rj)), 1.0, 0.0)
    g_row = -jnp.exp(par_ref[0:1, :]) * _softplus(rows_ref[0] + par_ref[1:2, :])
    gcrow[...] = _mm_hi(g_row, cum_row)
    brow[...] = _sigmoid(rows_ref[1])

    ii = lax.broadcasted_iota(jnp.int32, (CHUNK, 2 * CHUNK), 0)
    lj = lax.broadcasted_iota(jnp.int32, (CHUNK, 2 * CHUNK), 1)
    lo = lj < CHUNK
    hi = lj >= CHUNK
    jj = jnp.where(lo, lj, lj - CHUNK)
    eye2 = jnp.where(ii == jj, 1.0, 0.0)
    tri2 = (lo & (ii >= jj)) | (hi & (ii <= jj))
    strict2 = (lo & (ii > jj)) | (hi & (ii < jj))
    lo_t = lax.broadcasted_iota(jnp.int32, (2 * CHUNK, 2 * CHUNK), 1) < CHUNK

    def phase1(grp, carry):
        ns = [grp * CHUNK_GROUP + i for i in range(CHUNK_GROUP)]
        rws = [pl.ds(pl.multiple_of(n * CHUNK, CHUNK), CHUNK) for n in ns]
        qv = [qn[r, :] for r in rws]
        kv = [kn[r, :] for r in rws]
        kb = [k.astype(BF16) for k in kv]
        grams = [_mm_nt(jnp.concatenate([kb_, q.astype(BF16)], axis=0), jnp.concatenate([kb_, kb_], axis=0))
                 for kb_, q in zip(kb, qv)]
        gcc = [(gcf[0, r, :], gcf[1, r, :]) for r in rws]
        bcc = [(bcf[0, r, :], bcf[1, r, :]) for r in rws]
        gcr2 = [gcrow[pl.ds(n, 1), :] for n in ns]
        gams = [jnp.where(tri2, jnp.exp(jnp.where(tri2, jnp.where(lo, gf, gb) - gr, 0.0)), 0.0)
                for (gf, gb), gr in zip(gcc, gcr2)]
        a2s = [jnp.where(strict2, gram[:CHUNK] * jnp.where(lo, bf, bb) * gam, 0.0)
               for gram, (bf, bb), gam in zip(grams, bcc, gams)]
        t2s = _unit_tri_inverse_pairs(a2s, eye2, lo)
        uws = []
        for i, r in enumerate(rws):
            v = vn[r, :]
            (gf, gb), (bf, bb) = gcc[i], bcc[i]
            rhs = jnp.concatenate([jnp.concatenate([v * bf, kv[i] * (bf * jnp.exp(gf))], axis=1),
                                   jnp.concatenate([v * bb, kv[i] * (bb * jnp.exp(gb))], axis=1)], axis=0)
            uws.append(_mm(_block_diag(t2s[i], lo).astype(BF16), rhs.astype(BF16)))
        fins = []
        for i in range(CHUNK_GROUP):
            gf, gb = gcc[i]
            gtot_f = gcr2[i][:, CHUNK - 1:CHUNK]
            gtot_b = gcr2[i][:, CHUNK:CHUNK + 1]
            kg = jnp.concatenate([kv[i] * jnp.exp(gtot_f - gf), kv[i] * jnp.exp(gtot_b - gb)], axis=0)
            lhs = jnp.concatenate([_block_diag(kg.T, lo_t), _block_diag(grams[i][CHUNK:] * gams[i], lo)], axis=0)
            fins.append(_mm(lhs.astype(BF16), uws[i].astype(BF16)))
            dec_s[0, ns[i]] = jnp.broadcast_to(jnp.exp(gtot_f), (1, HEAD_DIM))
            dec_s[1, ns[i]] = jnp.broadcast_to(jnp.exp(gtot_b), (1, HEAD_DIM))
        for i, (n, r) in enumerate(zip(ns, rws)):
            fin = fins[i]
            for d in range(2):
                top = fin[d * HEAD_DIM:(d + 1) * HEAD_DIM]
                bot = fin[2 * HEAD_DIM + d * CHUNK:2 * HEAD_DIM + (d + 1) * CHUNK]
                c_s[d, n] = top[:, :HEAD_DIM]
                mq_s[d, n, pl.ds(0, HEAD_DIM), :] = (-top[:, HEAD_DIM:]).astype(BF16)
                mq_s[d, n, pl.ds(HEAD_DIM, CHUNK), :] = (qv[i] * jnp.exp(gcc[i][d]) - bot[:, HEAD_DIM:]).astype(BF16)
                (of_s if d == 0 else ob_s)[r, :] = bot[:, :HEAD_DIM]
        return carry

    lax.fori_loop(0, nchunk // CHUNK_GROUP, phase1, 0)

    def phase2(step, states):
        new = []
        for d, (st, out_s) in enumerate(zip(states, (of_s, ob_s))):
            n = step if d == 0 else nchunk - 1 - step
            r = _mm(mq_s[d, n], st.astype(BF16))
            rows = pl.ds(pl.multiple_of(n * CHUNK, CHUNK), CHUNK)
            out_s[rows, :] = out_s[rows, :] + r[HEAD_DIM:]
            new.append(st * dec_s[d, n] + r[:HEAD_DIM] + c_s[d, n])
        return tuple(new)

    zero_state = jnp.zeros((HEAD_DIM, HEAD_DIM), F32)
    lax.fori_loop(0, nchunk, phase2, (zero_state, zero_state))

    for c in range(s // rc):
        rows = pl.ds(c * rc, rc)
        o = of_s[rows, :] + ob_s[rows, :]
        z = z_ref[rows, :].astype(F32)
        o_ref[rows, :] = (_rms(o, na_ref[...]) * (z * _sigmoid(z))).astype(o_ref.dtype)


def _gdn(proj, small, rows, par, lanepar, conv_w, norm_a):
    b, s, _ = proj.shape
    nchunk = s // CHUNK
    assert 2 * CHUNK == LANES == HEAD_DIM and nchunk % CHUNK_GROUP == 0
    col = lambda cb: pl.BlockSpec((None, s, HEAD_DIM), lambda i, h, cb=cb: (i, 0, cb + h))
    cw = lambda cb: pl.BlockSpec((CONV_K, HEAD_DIM), lambda i, h, cb=cb: (0, cb + h))
    return pl.pallas_call(
        _gdn_body,
        grid=(b, GDN_HEADS),
        in_specs=[pl.BlockSpec((None, 8, LANES), lambda i, h: (h, 0, 0)),
                  pl.BlockSpec((8, LANES), lambda i, h: (0, 0)),
                  cw(0), cw(GDN_HEADS), cw(2 * GDN_HEADS),
                  col(CB_QA), col(CB_KA), col(CB_VA), col(CB_ZA),
                  pl.BlockSpec((None, s, LANES), lambda i, h: (i, 0, 0)),
                  pl.BlockSpec((None, None, 2, nchunk, LANES), lambda i, h: (i, h, 0, 0, 0)),
                  pl.BlockSpec((1, HEAD_DIM), lambda i, h: (0, 0))],
        out_specs=pl.BlockSpec((None, s, HEAD_DIM), lambda i, h: (i, 0, h)),
        out_shape=jax.ShapeDtypeStruct((b, s, GDN_WIDTH), BF16),
        scratch_shapes=[
            pltpu.VMEM((s + 16, HEAD_DIM), F32),
            pltpu.VMEM((s, HEAD_DIM), F32),
            pltpu.VMEM((s, HEAD_DIM), F32),
            pltpu.VMEM((s, HEAD_DIM), F32),
            pltpu.VMEM((s, LANES), F32),
            pltpu.VMEM((2, s, LANES), F32),
            pltpu.VMEM((2, s, LANES), F32),
            pltpu.VMEM((nchunk, LANES), F32),
            pltpu.VMEM((nchunk, LANES), F32),
            pltpu.VMEM((2, nchunk, HEAD_DIM + CHUNK, HEAD_DIM), BF16),
            pltpu.VMEM((2, nchunk, HEAD_DIM, HEAD_DIM), F32),
            pltpu.VMEM((2, nchunk, 1, HEAD_DIM), F32),
            pltpu.VMEM((s, HEAD_DIM), F32),
            pltpu.VMEM((s, HEAD_DIM), F32),
        ],
        compiler_params=pltpu.CompilerParams(dimension_semantics=("parallel", "arbitrary"),
                                             vmem_limit_bytes=VMEM_LIMIT),
        name="gdn",
    )(par, lanepar, conv_w, conv_w, conv_w, proj, proj, proj, proj, small, rows, norm_a)


def _dil_body(q0, k0, v0, q1, k1, v1, q2, k2, v2, bias_ref, o_ref,
              qf, kf, vf, qs, kp, vp, acc_s, m_s, l_s):
    s = q0.shape[0]
    scale = HEAD_DIM ** -0.5
    qi = lax.broadcasted_iota(jnp.int32, (QBLK, KWIN), 0)
    kj = lax.broadcasted_iota(jnp.int32, (QBLK, KWIN), 1)
    off = kj - HALF_WINDOW - qi
    band = (off >= -HALF_WINDOW) & (off <= HALF_WINDOW)
    zpad = jnp.zeros((HALF_WINDOW, HEAD_DIM), F32)

    for g, (q_ref, k_ref, v_ref) in enumerate(((q0, k0, v0), (q1, k1, v1), (q2, k2, v2))):
        dil = DIL_GROUPS[g][1]
        length = s // dil
        nblk = length // QBLK
        qf[...] = q_ref[...].astype(F32)
        kf[...] = k_ref[...].astype(F32)
        vf[...] = v_ref[...].astype(F32)
        kp[pl.ds(0, HALF_WINDOW), :] = zpad
        vp[pl.ds(0, HALF_WINDOW), :] = zpad
        kp[pl.ds(HALF_WINDOW + length, HALF_WINDOW), :] = zpad
        vp[pl.ds(HALF_WINDOW + length, HALF_WINDOW), :] = zpad
        bias = pltpu.roll(jnp.broadcast_to(bias_ref[g], (QBLK, BIAS_LEN)), 0, 1,
                          stride=1, stride_axis=0)[:, :KWIN]

        def residue(r, carry, g=g, dil=dil, length=length, nblk=nblk, bias=bias):
            cls = pl.ds(r, length, stride=dil) if dil > 1 else pl.ds(0, length)
            qs[pl.ds(0, length), :] = qf[cls, :]
            kp[pl.ds(HALF_WINDOW, length), :] = kf[cls, :]
            vp[pl.ds(HALF_WINDOW, length), :] = vf[cls, :]

            def block(i, carry2):
                j0 = pl.multiple_of(i * QBLK, QBLK)
                qb = qs[pl.ds(j0, QBLK), :].astype(BF16)
                kw = kp[pl.ds(j0, KWIN), :].astype(BF16)
                vw = vp[pl.ds(j0, KWIN), :].astype(BF16)
                kpos = j0 - HALF_WINDOW + kj
                ok = band & (kpos >= 0) & (kpos < length)
                logits = jnp.where(ok, _mm_nt(qb, kw) * scale + bias, NEG)
                m = jnp.max(logits, axis=-1, keepdims=True)
                p = jnp.exp(logits - m)
                lsum = jnp.sum(p, axis=-1, keepdims=True)
                acc = _mm(p.astype(BF16), vw)
                tok = pl.ds(r + dil * j0, QBLK, stride=dil) if dil > 1 else pl.ds(j0, QBLK)
                acc_s[g, tok, :] = acc
                m_s[g, tok, :] = m
                l_s[g, tok, :] = lsum
                return carry2

            lax.fori_loop(0, nblk, block, 0)
            return carry

        lax.fori_loop(0, dil, residue, 0)

    rc = 256
    for c in range(s // rc):
        rows = pl.ds(c * rc, rc)
        ms = [m_s[g, rows, :] for g in range(3)]
        mx = jnp.maximum(jnp.maximum(ms[0], ms[1]), ms[2])
        ws = [jnp.exp(ms[g] - mx) for g in range(3)]
        den = ws[0] * l_s[0, rows, :] + ws[1] * l_s[1, rows, :] + ws[2] * l_s[2, rows, :]
        num = ws[0] * acc_s[0, rows, :] + ws[1] * acc_s[1, rows, :] + ws[2] * acc_s[2, rows, :]
        o_ref[rows, :] = (num / den).astype(o_ref.dtype)


def _dilated(proj, bias):
    b, s, _ = proj.shape
    specs = []
    for g in range(3):
        for cb in (CB_QB, CB_KB, CB_VB):
            specs.append(pl.BlockSpec((None, s, HEAD_DIM),
                                      lambda i, h, cb=cb, g=g: (i, 0, cb + g * DIL_HEADS_PER_GROUP + h)))
    specs.append(pl.BlockSpec((None, 3, 1, BIAS_LEN), lambda i, h: (h, 0, 0, 0)))
    return pl.pallas_call(
        _dil_body,
        grid=(b, DIL_HEADS_PER_GROUP),
        in_specs=specs,
        out_specs=pl.BlockSpec((None, s, HEAD_DIM), lambda i, h: (i, 0, h)),
        out_shape=jax.ShapeDtypeStruct((b, s, DIL_OUT_WIDTH), BF16),
        scratch_shapes=[
            pltpu.VMEM((s, HEAD_DIM), F32),
            pltpu.VMEM((s, HEAD_DIM), F32),
            pltpu.VMEM((s, HEAD_DIM), F32),
            pltpu.VMEM((s, HEAD_DIM), F32),
            pltpu.VMEM((s + 2 * HALF_WINDOW, HEAD_DIM), F32),
            pltpu.VMEM((s + 2 * HALF_WINDOW, HEAD_DIM), F32),
            pltpu.VMEM((3, s, HEAD_DIM), F32),
            pltpu.VMEM((3, s, 1), F32),
            pltpu.VMEM((3, s, 1), F32),
        ],
        compiler_params=pltpu.CompilerParams(dimension_semantics=("parallel", "arbitrary"),
                                             vmem_limit_bytes=VMEM_LIMIT),
        name="dilated_attn",
    )(*([proj] * 9), bias)


def _mix_out_body(x_ref, oa_ref, ob_ref, ga_ref, gb_ref, wa_ref, wb_ref, wo_ref, ln_ref, o_ref):
    ya = _mm(oa_ref[...], wa_ref[...])
    yb = _mm(ob_ref[...], wb_ref[...])
    merged = _sigmoid(ga_ref[...].astype(F32)) * ya + _sigmoid(gb_ref[...].astype(F32)) * yb
    y = _mm(merged.astype(BF16), wo_ref[...])
    o_ref[...] = x_ref[...] + _rms(y, ln_ref[...])


def _mix_out(x2d, o_a, o_b, proj2d, w_a, w_b, w_o, ln_post, tm):
    t, d = x2d.shape
    assert (CB_GA * LANES) % d == 0 and (CB_GB * LANES) % d == 0
    nga, ngb = CB_GA * LANES // d, CB_GB * LANES // d
    const = lambda shape: pl.BlockSpec(shape, lambda i: (0, 0))
    return pl.pallas_call(
        _mix_out_body,
        grid=(t // tm,),
        in_specs=[pl.BlockSpec((tm, d), lambda i: (i, 0)),
                  pl.BlockSpec((tm, GDN_WIDTH), lambda i: (i, 0)),
                  pl.BlockSpec((tm, DIL_OUT_WIDTH), lambda i: (i, 0)),
                  pl.BlockSpec((tm, d), lambda i: (i, nga)),
                  pl.BlockSpec((tm, d), lambda i: (i, ngb)),
                  const((GDN_WIDTH, d)), const((DIL_OUT_WIDTH, d)), const((d, d)), const((1, d))],
        out_specs=pl.BlockSpec((tm, d), lambda i: (i, 0)),
        out_shape=jax.ShapeDtypeStruct((t, d), F32),
        compiler_params=pltpu.CompilerParams(dimension_semantics=("parallel",),
                                             vmem_limit_bytes=VMEM_LIMIT),
        name="mix_out",
    )(x2d, o_a, o_b, proj2d, proj2d, w_a, w_b, w_o, ln_post)


def _mlp_body(x_ref, g1_ref, w1_ref, w2_ref, g2_ref, o_ref, h_ref, acc_ref):
    k = pl.program_id(1)

    @pl.when(k == 0)
    def _():
        h_ref[...] = _rms(x_ref[...], g1_ref[...]).astype(BF16)

    f = jnp.maximum(_mm(h_ref[...], w1_ref[...]), 0.0)
    part = _mm((f * f).astype(BF16), w2_ref[...])

    @pl.when(k == 0)
    def _():
        acc_ref[...] = part

    @pl.when(k > 0)
    def _():
        acc_ref[...] += part

    @pl.when(k == pl.num_programs(1) - 1)
    def _():
        o_ref[...] = x_ref[...] + _rms(acc_ref[...], g2_ref[...])


def _mlp(x2d, ln_pre, w1, w2, ln_post, tm, tf):
    t, d = x2d.shape
    dff = w1.shape[1]
    return pl.pallas_call(
        _mlp_body,
        grid=(t // tm, dff // tf),
        in_specs=[pl.BlockSpec((tm, d), lambda i, k: (i, 0)),
                  pl.BlockSpec((1, d), lambda i, k: (0, 0)),
                  pl.BlockSpec((d, tf), lambda i, k: (0, k)),
                  pl.BlockSpec((tf, d), lambda i, k: (k, 0)),
                  pl.BlockSpec((1, d), lambda i, k: (0, 0))],
        out_specs=pl.BlockSpec((tm, d), lambda i, k: (i, 0)),
        out_shape=jax.ShapeDtypeStruct((t, d), F32),
        scratch_shapes=[pltpu.VMEM((tm, d), BF16), pltpu.VMEM((tm, d), F32)],
        compiler_params=pltpu.CompilerParams(dimension_semantics=("parallel", "arbitrary"),
                                             vmem_limit_bytes=VMEM_LIMIT),
        name="mlp",
    )(x2d, ln_pre, w1, w2, ln_post)


def _t5_bucket_np(rel):
    nb = REL_BUCKETS // 2
    ret = (rel > 0).astype(np.int32) * nb
    n = np.abs(rel)
    max_exact = nb // 2
    large = max_exact + (np.log(np.maximum(n, 1) / max_exact) / math.log(REL_MAX_DIST / max_exact)
                         * (nb - max_exact)).astype(np.int32)
    large = np.minimum(large, nb - 1)
    return ret + np.where(n < max_exact, n, large).astype(np.int32)


def _attention_bias(rel_bias):
    c = np.arange(BIAS_LEN)
    off = np.where(c < KWIN, np.clip(c - HALF_WINDOW, -HALF_WINDOW, HALF_WINDOW), -HALF_WINDOW)
    per_group = []
    for gi, (_, dil) in enumerate(DIL_GROUPS):
        bt = rel_bias[_t5_bucket_np(off * dil)]
        per_group.append(bt[:, gi * DIL_HEADS_PER_GROUP:(gi + 1) * DIL_HEADS_PER_GROUP])
    return jnp.transpose(jnp.stack(per_group, axis=0), (2, 0, 1))[:, :, None, :].astype(F32)


def _gdn_params(a_log_f, a_log_b, dt_bias_f, dt_bias_b):
    side = lambda f, bk: jnp.concatenate([jnp.broadcast_to(f[:, None], (GDN_HEADS, CHUNK)),
                                          jnp.broadcast_to(bk[:, None], (GDN_HEADS, CHUNK))], axis=1)
    zero = jnp.zeros((GDN_HEADS, LANES), F32)
    par = jnp.stack([side(a_log_f, a_log_b), side(dt_bias_f, dt_bias_b)] + [zero] * 6, axis=1).astype(F32)
    pad = lambda f, bk: jnp.pad(jnp.concatenate([f, bk]), (0, LANES - 2 * GDN_HEADS))
    lanepar = jnp.stack([pad(a_log_f, a_log_b), pad(dt_bias_f, dt_bias_b)]
                        + [jnp.zeros((LANES,), F32)] * 6, axis=0).astype(F32)
    return par, lanepar


def kernel(x, rel_bias, ln_mix_pre, w_in, conv_w, a_log_f, a_log_b, dt_bias_f, dt_bias_b, norm_a,
           w_branch_a, w_branch_b, w_out, ln_mix_post, ln_mlp_pre, w_ff1, w_ff2, ln_mlp_post):
    b, s, d = x.shape
    t = b * s
    nchunk = s // CHUNK
    c_small = 4 * GDN_WIDTH
    bias = _attention_bias(rel_bias)
    for l in range(ln_mix_pre.shape[0]):
        x2d = x.reshape(t, d)
        w = w_in[l]
        c_qb = c_small + N_SMALL
        c_ga = c_qb + 3 * DIL_WIDTH
        w_wide = jnp.concatenate([w[:, :c_small], w[:, c_ga:], w[:, c_qb:c_ga]], axis=1).astype(BF16)
        w_small = jnp.pad(w[:, c_small:c_qb], ((0, 0), (0, LANES - N_SMALL))).astype(BF16)
        gain = ln_mix_pre[l][None, :]
        proj = _norm_matmul(x2d, gain, w_wide, BF16, 1024, 512, "in_proj_wide")
        small = _norm_matmul(x2d, gain, w_small, F32, 1024, LANES, "in_proj_small")

        proj3 = proj.reshape(b, s, N_WIDE)
        small3 = small.reshape(b, s, LANES)
        rows = small3[:, :, :N_SMALL].reshape(b, nchunk, CHUNK, 2, 2, GDN_HEADS)
        rows = jnp.transpose(rows, (0, 5, 3, 1, 4, 2)).reshape(b, GDN_HEADS, 2, nchunk, 2 * CHUNK)
        par, lanepar = _gdn_params(a_log_f[l], a_log_b[l], dt_bias_f[l], dt_bias_b[l])
        o_a = _gdn(proj3, small3, rows, par, lanepar, conv_w[l].astype(F32), norm_a[l][None, :].astype(F32))
        o_b = _dilated(proj3, bias)

        x2d = _mix_out(x2d, o_a.reshape(t, GDN_WIDTH), o_b.reshape(t, DIL_OUT_WIDTH), proj,
                       w_branch_a[l].astype(BF16), w_branch_b[l].astype(BF16), w_out[l].astype(BF16),
                       ln_mix_post[l][None, :], 512)
        x2d = _mlp(x2d, ln_mlp_pre[l][None, :], w_ff1[l].astype(BF16), w_ff2[l].astype(BF16),
                   ln_mlp_post[l][None, :], 1024, 1024)
        x = x2d.reshape(b, s, d)
    return x
```

```python
import math

import numpy as np
import jax
import jax.numpy as jnp
from jax import lax
from jax.experimental import pallas as pl
from jax.experimental.pallas import tpu as pltpu

F32 = jnp.float32
BF16 = jnp.bfloat16
HIGHEST = lax.Precision.HIGHEST

D_MODEL = 1024
GDN_HEADS = 8
HEAD_DIM = 128
GDN_WIDTH = GDN_HEADS * HEAD_DIM
CHUNK = 64
CHUNK_GROUP = 8
ATT_GROUP = 4
CONV_K = 5
DIL_GROUPS = ((128, 1), (512, 4), (2048, 16))
DIL_HEADS_PER_GROUP = 4
DIL_HEADS = 12
DIL_WIDTH = DIL_HEADS * HEAD_DIM
DIL_OUT_WIDTH = DIL_HEADS_PER_GROUP * HEAD_DIM
HALF_WINDOW = 64
QBLK = 128
KWIN = QBLK + 2 * HALF_WINDOW
REL_BUCKETS = 32
REL_MAX_DIST = 1024
D_FF = 4 * D_MODEL
EPS = 1e-6
NEG = -1e30
N_SMALL = 4 * GDN_HEADS
LANES = 128
BIAS_LEN = 3 * LANES

CB_QA, CB_KA, CB_VA, CB_ZA = 0, 8, 16, 24
CB_GA, CB_GB = 32, 40
CB_QB, CB_KB, CB_VB = 48, 60, 72
N_WIDE = 84 * LANES

VMEM_LIMIT = 56 * 1024 * 1024


def _mm(a, b):
    return jnp.dot(a, b, preferred_element_type=F32)


def _mm_nt(a, b):
    return lax.dot_general(a, b, (((1,), (1,)), ((), ())), preferred_element_type=F32)


def _mm_hi(a, b):
    return jnp.dot(a, b, preferred_element_type=F32, precision=HIGHEST)


def _rms(x, gain):
    return x * lax.rsqrt(jnp.mean(x * x, axis=-1, keepdims=True) + EPS) * gain


def _sigmoid(x):
    return 1.0 / (1.0 + jnp.exp(-x))


def _softplus(x):
    return jnp.maximum(x, 0.0) + jnp.log(1.0 + jnp.exp(-jnp.abs(x)))


def _norm_matmul_body(x_ref, g_ref, w_ref, o_ref, h_ref):
    @pl.when(pl.program_id(1) == 0)
    def _():
        h_ref[...] = _rms(x_ref[...], g_ref[...]).astype(BF16)

    o_ref[...] = _mm(h_ref[...], w_ref[...]).astype(o_ref.dtype)


def _norm_matmul(x2d, gain, w, out_dtype, tm, tn, name):
    t, d = x2d.shape
    n = w.shape[1]
    return pl.pallas_call(
        _norm_matmul_body,
        grid=(t // tm, n // tn),
        in_specs=[pl.BlockSpec((tm, d), lambda i, j: (i, 0)),
                  pl.BlockSpec((1, d), lambda i, j: (0, 0)),
                  pl.BlockSpec((d, tn), lambda i, j: (0, j))],
        out_specs=pl.BlockSpec((tm, tn), lambda i, j: (i, j)),
        out_shape=jax.ShapeDtypeStruct((t, n), out_dtype),
        scratch_shapes=[pltpu.VMEM((tm, d), BF16)],
        compiler_params=pltpu.CompilerParams(dimension_semantics=("parallel", "arbitrary"),
                                             vmem_limit_bytes=VMEM_LIMIT),
        name=name,
    )(x2d, gain, w)


def _split_bf16(x):
    hi = x.astype(BF16)
    return hi, (x - hi.astype(F32)).astype(BF16)


def _split3_lanes(x):
    hi = x.astype(BF16)
    r = x - hi.astype(F32)
    mid = r.astype(BF16)
    lo = (r - mid.astype(F32)).astype(BF16)
    return jnp.concatenate([hi, mid, lo], axis=1)


def _block_diag(x, lo):
    return jnp.concatenate([jnp.where(lo, x, 0.0), jnp.where(lo, 0.0, x)], axis=0)


def _mm_split(lhs, rhs):
    lh, ll = _split_bf16(lhs)
    rh, rl = _split_bf16(rhs)
    return (_mm(jnp.concatenate([lh, ll], axis=1), jnp.concatenate([rh, rh], axis=0))
            + _mm(lh, rl))


def _unit_tri_inverse_pairs(a2s, eye2, lo):
    xs = [_mm_split(a2, _block_diag(a2, lo)) for a2 in a2s]
    qs = [eye2 - a2 for a2 in a2s]
    levels = int(math.log2(CHUNK)) - 1
    for lvl in range(levels):
        if lvl < levels - 1:
            outs = [_mm_split(jnp.concatenate([q, x], axis=0), _block_diag(x, lo)) for q, x in zip(qs, xs)]
            qs = [q + out[:CHUNK] for q, out in zip(qs, outs)]
            xs = [out[CHUNK:] for out in outs]
        else:
            qs = [q + _mm_split(q, _block_diag(x, lo)) for q, x in zip(qs, xs)]
    return qs


def _gdn_body(par_ref, lanepar_ref, cwq_ref, cwk_ref, cwv_ref, q_ref, k_ref, v_ref, z_ref, small_ref,
              rows_ref, na_ref, o_ref,
              xp, qn, kn, vn, act, gcf, bcf, gcrow, brow, mq_s, c_s, dec_s, of_s, ob_s):
    s = q_ref.shape[0]
    nchunk = s // CHUNK
    h = pl.program_id(1)
    rc = 256

    @pl.when(h == 0)
    def _():
        is_decay = lax.broadcasted_iota(jnp.int32, (rc, LANES), 1) < 2 * GDN_HEADS
        neg_a = -jnp.exp(lanepar_ref[0:1, :])
        dtb = lanepar_ref[1:2, :]
        for c in range(s // rc):
            rows = pl.ds(c * rc, rc)
            slab = small_ref[rows, :]
            act[rows, :] = jnp.where(is_decay, neg_a * _softplus(slab + dtb), _sigmoid(slab))

    zeros8 = jnp.zeros((8, HEAD_DIM), F32)
    xp[pl.ds(0, 8), :] = zeros8
    xp[pl.ds(8 + s, 8), :] = zeros8

    def conv_into(src_ref, cw_ref, dst_ref, l2, scale):
        xp[pl.ds(8, s), :] = src_ref[...].astype(F32)
        for c in range(s // rc):
            acc = None
            for j in range(CONV_K):
                t = xp[pl.ds(c * rc + 8 - CONV_K // 2 + j, rc), :] * cw_ref[j:j + 1, :]
                acc = t if acc is None else acc + t
            y = acc * _sigmoid(acc)
            if l2:
                y = y * lax.rsqrt(jnp.sum(y * y, axis=-1, keepdims=True) + EPS)
                if scale != 1.0:
                    y = y * scale
            dst_ref[pl.ds(c * rc, rc), :] = y

    conv_into(q_ref, cwq_ref, qn, True, HEAD_DIM ** -0.5)
    conv_into(k_ref, cwk_ref, kn, True, 1.0)
    conv_into(v_ref, cwv_ref, vn, False, 1.0)

    sel_k = lax.broadcasted_iota(jnp.int32, (LANES, 4 * LANES), 0)
    sel_c = lax.broadcasted_iota(jnp.int32, (LANES, 4 * LANES), 1)
    sel = jnp.where(sel_k == (sel_c >> 7) * GDN_HEADS + h, 1.0, 0.0).astype(BF16)
    sel3 = jnp.concatenate([sel, sel, sel], axis=0)
    bi = lax.broadcasted_iota(jnp.int32, (rc, rc), 0)
    bj = lax.broadcasted_iota(jnp.int32, (rc, rc), 1)
    same_chunk = jnp.bitwise_xor(bi, bj) < CHUNK
    cums = (jnp.where(same_chunk & (bj <= bi), 1.0, 0.0).astype(BF16),
            jnp.where(same_chunk & (bj >= bi), 1.0, 0.0).astype(BF16))
    for c in range(s // rc):
        rows = pl.ds(c * rc, rc)
        picked = _mm(_split3_lanes(act[rows, :]), sel3)
        for d in range(2):
            g = picked[:, d * LANES:(d + 1) * LANES]
            gc3 = _mm(cums[d], _split3_lanes(g))
            gcf[d, rows, :] = gc3[:, :LANES] + gc3[:, LANES:2 * LANES] + gc3[:, 2 * LANES:]
            bcf[d, rows, :] = picked[:, (2 + d) * LANES:(3 + d) * LANES]
    ri = lax.broadcasted_iota(jnp.int32, (LANES, LANES), 0)
    rj = lax.broadcasted_iota(jnp.int32, (LANES, LANES), 1)
    cum_row = jnp.where(((ri < CHUNK) & (rj < CHUNK) & (ri <= rj))
                        | ((ri >= CHUNK) & (rj >= CHUNK) & (ri >= rj)), 1.0, 0.0)
    g_row = -jnp.exp(par_ref[0:1, :]) * _softplus(rows_ref[0] + par_ref[1:2, :])
    gcrow[...] = _mm_hi(g_row, cum_row)
    brow[...] = _sigmoid(rows_ref[1])

    ii = lax.broadcasted_iota(jnp.int32, (CHUNK, 2 * CHUNK), 0)
    lj = lax.broadcasted_iota(jnp.int32, (CHUNK, 2 * CHUNK), 1)
    lo = lj < CHUNK
    hi = lj >= CHUNK
    jj = jnp.where(lo, lj, lj - CHUNK)
    eye2 = jnp.where(ii == jj, 1.0, 0.0)
    tri2 = (lo & (ii >= jj)) | (hi & (ii <= jj))
    strict2 = (lo & (ii > jj)) | (hi & (ii < jj))
    lo_t = lax.broadcasted_iota(jnp.int32, (2 * CHUNK, 2 * CHUNK), 1) < CHUNK

    def phase1(grp, carry):
        ns = [grp * CHUNK_GROUP + i for i in range(CHUNK_GROUP)]
        rws = [pl.ds(pl.multiple_of(n * CHUNK, CHUNK), CHUNK) for n in ns]
        qv = [qn[r, :] for r in rws]
        kv = [kn[r, :] for r in rws]
        kb = [k.astype(BF16) for k in kv]
        grams = [_mm_nt(jnp.concatenate([kb_, q.astype(BF16)], axis=0), jnp.concatenate([kb_, kb_], axis=0))
                 for kb_, q in zip(kb, qv)]
        gcc = [(gcf[0, r, :], gcf[1, r, :]) for r in rws]
        bcc = [(bcf[0, r, :], bcf[1, r, :]) for r in rws]
        gcr2 = [gcrow[pl.ds(n, 1), :] for n in ns]
        gams = [jnp.where(tri2, jnp.exp(jnp.where(tri2, jnp.where(lo, gf, gb) - gr, 0.0)), 0.0)
                for (gf, gb), gr in zip(gcc, gcr2)]
        a2s = [jnp.where(strict2, gram[:CHUNK] * jnp.where(lo, bf, bb) * gam, 0.0)
               for gram, (bf, bb), gam in zip(grams, bcc, gams)]
        t2s = _unit_tri_inverse_pairs(a2s, eye2, lo)
        uws = []
        for i, r in enumerate(rws):
            v = vn[r, :]
            (gf, gb), (bf, bb) = gcc[i], bcc[i]
            rhs = jnp.concatenate([jnp.concatenate([v * bf, kv[i] * (bf * jnp.exp(gf))], axis=1),
                                   jnp.concatenate([v * bb, kv[i] * (bb * jnp.exp(gb))], axis=1)], axis=0)
            uws.append(_mm(_block_diag(t2s[i], lo).astype(BF16), rhs.astype(BF16)))
        fins = []
        for i in range(CHUNK_GROUP):
            gf, gb = gcc[i]
            gtot_f = gcr2[i][:, CHUNK - 1:CHUNK]
            gtot_b = gcr2[i][:, CHUNK:CHUNK + 1]
            kg = jnp.concatenate([kv[i] * jnp.exp(gtot_f - gf), kv[i] * jnp.exp(gtot_b - gb)], axis=0)
            lhs = jnp.concatenate([_block_diag(kg.T, lo_t), _block_diag(grams[i][CHUNK:] * gams[i], lo)], axis=0)
            fins.append(_mm(lhs.astype(BF16), uws[i].astype(BF16)))
            dec_s[0, ns[i]] = jnp.broadcast_to(jnp.exp(gtot_f), (1, HEAD_DIM))
            dec_s[1, ns[i]] = jnp.broadcast_to(jnp.exp(gtot_b), (1, HEAD_DIM))
        for i, (n, r) in enumerate(zip(ns, rws)):
            fin = fins[i]
            for d in range(2):
                top = fin[d * HEAD_DIM:(d + 1) * HEAD_DIM]
                bot = fin[2 * HEAD_DIM + d * CHUNK:2 * HEAD_DIM + (d + 1) * CHUNK]
                c_s[d, n] = top[:, :HEAD_DIM]
                mq_s[d, n, pl.ds(0, HEAD_DIM), :] = (-top[:, HEAD_DIM:]).astype(BF16)
                mq_s[d, n, pl.ds(HEAD_DIM, CHUNK), :] = (qv[i] * jnp.exp(gcc[i][d]) - bot[:, HEAD_DIM:]).astype(BF16)
                (of_s if d == 0 else ob_s)[r, :] = bot[:, :HEAD_DIM]
        return carry

    lax.fori_loop(0, nchunk // CHUNK_GROUP, phase1, 0)

    def phase2(step, states):
        new = []
        for d, (st, out_s) in enumerate(zip(states, (of_s, ob_s))):
            n = step if d == 0 else nchunk - 1 - step
            r = _mm(mq_s[d, n], st.astype(BF16))
            rows = pl.ds(pl.multiple_of(n * CHUNK, CHUNK), CHUNK)
            out_s[rows, :] = out_s[rows, :] + r[HEAD_DIM:]
            new.append(st * dec_s[d, n] + r[:HEAD_DIM] + c_s[d, n])
        return tuple(new)

    zero_state = jnp.zeros((HEAD_DIM, HEAD_DIM), F32)
    lax.fori_loop(0, nchunk, phase2, (zero_state, zero_state))

    for c in range(s // rc):
        rows = pl.ds(c * rc, rc)
        o = of_s[rows, :] + ob_s[rows, :]
        z = z_ref[rows, :].astype(F32)
        o_ref[rows, :] = (_rms(o, na_ref[...]) * (z * _sigmoid(z))).astype(o_ref.dtype)


def _gdn(proj, small, rows, par, lanepar, conv_w, norm_a):
    b, s, _ = proj.shape
    nchunk = s // CHUNK
    assert 2 * CHUNK == LANES == HEAD_DIM and nchunk % CHUNK_GROUP == 0
    col = lambda cb: pl.BlockSpec((None, s, HEAD_DIM), lambda i, h, cb=cb: (i, 0, cb + h))
    cw = lambda cb: pl.BlockSpec((CONV_K, HEAD_DIM), lambda i, h, cb=cb: (0, cb + h))
    return pl.pallas_call(
        _gdn_body,
        grid=(b, GDN_HEADS),
        in_specs=[pl.BlockSpec((None, 8, LANES), lambda i, h: (h, 0, 0)),
                  pl.BlockSpec((8, LANES), lambda i, h: (0, 0)),
                  cw(0), cw(GDN_HEADS), cw(2 * GDN_HEADS),
                  col(CB_QA), col(CB_KA), col(CB_VA), col(CB_ZA),
                  pl.BlockSpec((None, s, LANES), lambda i, h: (i, 0, 0)),
                  pl.BlockSpec((None, None, 2, nchunk, LANES), lambda i, h: (i, h, 0, 0, 0)),
                  pl.BlockSpec((1, HEAD_DIM), lambda i, h: (0, 0))],
        out_specs=pl.BlockSpec((None, s, HEAD_DIM), lambda i, h: (i, 0, h)),
        out_shape=jax.ShapeDtypeStruct((b, s, GDN_WIDTH), BF16),
        scratch_shapes=[
            pltpu.VMEM((s + 16, HEAD_DIM), F32),
            pltpu.VMEM((s, HEAD_DIM), F32),
            pltpu.VMEM((s, HEAD_DIM), F32),
            pltpu.VMEM((s, HEAD_DIM), F32),
            pltpu.VMEM((s, LANES), F32),
            pltpu.VMEM((2, s, LANES), F32),
            pltpu.VMEM((2, s, LANES), F32),
            pltpu.VMEM((nchunk, LANES), F32),
            pltpu.VMEM((nchunk, LANES), F32),
            pltpu.VMEM((2, nchunk, HEAD_DIM + CHUNK, HEAD_DIM), BF16),
            pltpu.VMEM((2, nchunk, HEAD_DIM, HEAD_DIM), F32),
            pltpu.VMEM((2, nchunk, 1, HEAD_DIM), F32),
            pltpu.VMEM((s, HEAD_DIM), F32),
            pltpu.VMEM((s, HEAD_DIM), F32),
        ],
        compiler_params=pltpu.CompilerParams(dimension_semantics=("parallel", "arbitrary"),
                                             vmem_limit_bytes=VMEM_LIMIT),
        name="gdn",
    )(par, lanepar, conv_w, conv_w, conv_w, proj, proj, proj, proj, small, rows, norm_a)


def _dil_body(q0, k0, v0, q1, k1, v1, q2, k2, v2, bias_ref, o_ref,
              qf, kf, vf, qs, kp, vp, acc_s, m_s, l_s):
    s = q0.shape[0]
    scale = HEAD_DIM ** -0.5
    qi = lax.broadcasted_iota(jnp.int32, (QBLK, KWIN), 0)
    kj = lax.broadcasted_iota(jnp.int32, (QBLK, KWIN), 1)
    off = kj - HALF_WINDOW - qi
    band = (off >= -HALF_WINDOW) & (off <= HALF_WINDOW)
    zpad = jnp.zeros((HALF_WINDOW, HEAD_DIM), F32)

    for g, (q_ref, k_ref, v_ref) in enumerate(((q0, k0, v0), (q1, k1, v1), (q2, k2, v2))):
        dil = DIL_GROUPS[g][1]
        length = s // dil
        nblk = length // QBLK
        qf[...] = q_ref[...].astype(F32)
        kf[...] = k_ref[...].astype(F32)
        vf[...] = v_ref[...].astype(F32)
        seg_q = max(length, QBLK)
        seg_k = seg_q + 2 * HALF_WINDOW
        nseg = ATT_GROUP if nblk < ATT_GROUP else 1
        for c in range(nseg):
            for buf in (kp, vp):
                buf[pl.ds(c * seg_k, HALF_WINDOW), :] = zpad
                buf[pl.ds(c * seg_k + HALF_WINDOW + length, HALF_WINDOW), :] = zpad
        bias = pltpu.roll(jnp.broadcast_to(bias_ref[g], (QBLK, BIAS_LEN)), 0, 1,
                          stride=1, stride_axis=0)[:, :KWIN]

        def stage(c, r, dil=dil, length=length, seg_q=seg_q, seg_k=seg_k):
            cls = pl.ds(r, length, stride=dil) if dil > 1 else pl.ds(0, length)
            qs[pl.ds(c * seg_q, length), :] = qf[cls, :]
            kp[pl.ds(c * seg_k + HALF_WINDOW, length), :] = kf[cls, :]
            vp[pl.ds(c * seg_k + HALF_WINDOW, length), :] = vf[cls, :]

        def attend(items, g=g, dil=dil, length=length, bias=bias):
            logits = []
            for q_off, k_off, j0, r in items:
                kpos = j0 - HALF_WINDOW + kj
                ok = band & (kpos >= 0) & (kpos < length)
                sc = _mm_nt(qs[pl.ds(q_off, QBLK), :].astype(BF16), kp[pl.ds(k_off, KWIN), :].astype(BF16))
                logits.append(jnp.where(ok, sc * scale + bias, NEG))
            ms = [jnp.max(lg, axis=-1, keepdims=True) for lg in logits]
            ps = [jnp.exp(lg - m) for lg, m in zip(logits, ms)]
            ls = [jnp.sum(p, axis=-1, keepdims=True) for p in ps]
            accs = [_mm(p.astype(BF16), vp[pl.ds(it[1], KWIN), :].astype(BF16)) for p, it in zip(ps, items)]
            for (q_off, k_off, j0, r), m, lsum, acc in zip(items, ms, ls, accs):
                tok = pl.ds(r + dil * j0, QBLK, stride=dil) if dil > 1 else pl.ds(j0, QBLK)
                acc_s[g, tok, :] = acc
                m_s[g, tok, :] = m
                l_s[g, tok, :] = lsum

        if nseg == 1:
            def residue(r, carry, nblk=nblk, stage=stage, attend=attend):
                stage(0, r)

                def blocks(it, carry2):
                    j0s = [pl.multiple_of((it * ATT_GROUP + i) * QBLK, QBLK) for i in range(ATT_GROUP)]
                    attend([(j0, j0, j0, r) for j0 in j0s])
                    return carry2

                lax.fori_loop(0, nblk // ATT_GROUP, blocks, 0)
                return carry

            lax.fori_loop(0, dil, residue, 0)
        else:
            def classes(it, carry, seg_q=seg_q, seg_k=seg_k, stage=stage, attend=attend):
                rs = [it * ATT_GROUP + c for c in range(ATT_GROUP)]
                for c, r in enumerate(rs):
                    stage(c, r)
                attend([(c * seg_q, c * seg_k, 0, r) for c, r in enumerate(rs)])
                return carry

            lax.fori_loop(0, dil // ATT_GROUP, classes, 0)

    rc = 256
    for c in range(s // rc):
        rows = pl.ds(c * rc, rc)
        ms = [m_s[g, rows, :] for g in range(3)]
        mx = jnp.maximum(jnp.maximum(ms[0], ms[1]), ms[2])
        ws = [jnp.exp(ms[g] - mx) for g in range(3)]
        den = ws[0] * l_s[0, rows, :] + ws[1] * l_s[1, rows, :] + ws[2] * l_s[2, rows, :]
        num = ws[0] * acc_s[0, rows, :] + ws[1] * acc_s[1, rows, :] + ws[2] * acc_s[2, rows, :]
        o_ref[rows, :] = (num / den).astype(o_ref.dtype)


def _dilated(proj, bias):
    b, s, _ = proj.shape
    specs = []
    for g in range(3):
        for cb in (CB_QB, CB_KB, CB_VB):
            specs.append(pl.BlockSpec((None, s, HEAD_DIM),
                                      lambda i, h, cb=cb, g=g: (i, 0, cb + g * DIL_HEADS_PER_GROUP + h)))
    specs.append(pl.BlockSpec((None, 3, 1, BIAS_LEN), lambda i, h: (h, 0, 0, 0)))
    return pl.pallas_call(
        _dil_body,
        grid=(b, DIL_HEADS_PER_GROUP),
        in_specs=specs,
        out_specs=pl.BlockSpec((None, s, HEAD_DIM), lambda i, h: (i, 0, h)),
        out_shape=jax.ShapeDtypeStruct((b, s, DIL_OUT_WIDTH), BF16),
        scratch_shapes=[
            pltpu.VMEM((s, HEAD_DIM), F32),
            pltpu.VMEM((s, HEAD_DIM), F32),
            pltpu.VMEM((s, HEAD_DIM), F32),
            pltpu.VMEM((s, HEAD_DIM), F32),
            pltpu.VMEM((s + 2 * HALF_WINDOW, HEAD_DIM), F32),
            pltpu.VMEM((s + 2 * HALF_WINDOW, HEAD_DIM), F32),
            pltpu.VMEM((3, s, HEAD_DIM), F32),
            pltpu.VMEM((3, s, 1), F32),
            pltpu.VMEM((3, s, 1), F32),
        ],
        compiler_params=pltpu.CompilerParams(dimension_semantics=("parallel", "arbitrary"),
                                             vmem_limit_bytes=VMEM_LIMIT),
        name="dilated_attn",
    )(*([proj] * 9), bias)


def _mix_out_body(x_ref, oa_ref, ob_ref, ga_ref, gb_ref, wa_ref, wb_ref, wo_ref, ln_ref, o_ref):
    ya = _mm(oa_ref[...], wa_ref[...])
    yb = _mm(ob_ref[...], wb_ref[...])
    merged = _sigmoid(ga_ref[...].astype(F32)) * ya + _sigmoid(gb_ref[...].astype(F32)) * yb
    y = _mm(merged.astype(BF16), wo_ref[...])
    o_ref[...] = x_ref[...] + _rms(y, ln_ref[...])


def _mix_out(x2d, o_a, o_b, proj2d, w_a, w_b, w_o, ln_post, tm):
    t, d = x2d.shape
    assert (CB_GA * LANES) % d == 0 and (CB_GB * LANES) % d == 0
    nga, ngb = CB_GA * LANES // d, CB_GB * LANES // d
    const = lambda shape: pl.BlockSpec(shape, lambda i: (0, 0))
    return pl.pallas_call(
        _mix_out_body,
        grid=(t // tm,),
        in_specs=[pl.BlockSpec((tm, d), lambda i: (i, 0)),
                  pl.BlockSpec((tm, GDN_WIDTH), lambda i: (i, 0)),
                  pl.BlockSpec((tm, DIL_OUT_WIDTH), lambda i: (i, 0)),
                  pl.BlockSpec((tm, d), lambda i: (i, nga)),
                  pl.BlockSpec((tm, d), lambda i: (i, ngb)),
                  const((GDN_WIDTH, d)), const((DIL_OUT_WIDTH, d)), const((d, d)), const((1, d))],
        out_specs=pl.BlockSpec((tm, d), lambda i: (i, 0)),
        out_shape=jax.ShapeDtypeStruct((t, d), F32),
        compiler_params=pltpu.CompilerParams(dimension_semantics=("parallel",),
                                             vmem_limit_bytes=VMEM_LIMIT),
        name="mix_out",
    )(x2d, o_a, o_b, proj2d, proj2d, w_a, w_b, w_o, ln_post)


def _mlp_body(x_ref, g1_ref, w1_ref, w2_ref, g2_ref, o_ref, h_ref, acc_ref):
    k = pl.program_id(1)

    @pl.when(k == 0)
    def _():
        h_ref[...] = _rms(x_ref[...], g1_ref[...]).astype(BF16)

    f = jnp.maximum(_mm(h_ref[...], w1_ref[...]), 0.0)
    part = _mm((f * f).astype(BF16), w2_ref[...])

    @pl.when(k == 0)
    def _():
        acc_ref[...] = part

    @pl.when(k > 0)
    def _():
        acc_ref[...] += part

    @pl.when(k == pl.num_programs(1) - 1)
    def _():
        o_ref[...] = x_ref[...] + _rms(acc_ref[...], g2_ref[...])


def _mlp(x2d, ln_pre, w1, w2, ln_post, tm, tf):
    t, d = x2d.shape
    dff = w1.shape[1]
    return pl.pallas_call(
        _mlp_body,
        grid=(t // tm, dff // tf),
        in_specs=[pl.BlockSpec((tm, d), lambda i, k: (i, 0)),
                  pl.BlockSpec((1, d), lambda i, k: (0, 0)),
                  pl.BlockSpec((d, tf), lambda i, k: (0, k)),
                  pl.BlockSpec((tf, d), lambda i, k: (k, 0)),
                  pl.BlockSpec((1, d), lambda i, k: (0, 0))],
        out_specs=pl.BlockSpec((tm, d), lambda i, k: (i, 0)),
        out_shape=jax.ShapeDtypeStruct((t, d), F32),
        scratch_shapes=[pltpu.VMEM((tm, d), BF16), pltpu.VMEM((tm, d), F32)],
        compiler_params=pltpu.CompilerParams(dimension_semantics=("parallel", "arbitrary"),
                                             vmem_limit_bytes=VMEM_LIMIT),
        name="mlp",
    )(x2d, ln_pre, w1, w2, ln_post)


def _t5_bucket_np(rel):
    nb = REL_BUCKETS // 2
    ret = (rel > 0).astype(np.int32) * nb
    n = np.abs(rel)
    max_exact = nb // 2
    large = max_exact + (np.log(np.maximum(n, 1) / max_exact) / math.log(REL_MAX_DIST / max_exact)
                         * (nb - max_exact)).astype(np.int32)
    large = np.minimum(large, nb - 1)
    return ret + np.where(n < max_exact, n, large).astype(np.int32)


def _attention_bias(rel_bias):
    c = np.arange(BIAS_LEN)
    off = np.where(c < KWIN, np.clip(c - HALF_WINDOW, -HALF_WINDOW, HALF_WINDOW), -HALF_WINDOW)
    per_group = []
    for gi, (_, dil) in enumerate(DIL_GROUPS):
        bt = rel_bias[_t5_bucket_np(off * dil)]
        per_group.append(bt[:, gi * DIL_HEADS_PER_GROUP:(gi + 1) * DIL_HEADS_PER_GROUP])
    return jnp.transpose(jnp.stack(per_group, axis=0), (2, 0, 1))[:, :, None, :].astype(F32)


def _gdn_params(a_log_f, a_log_b, dt_bias_f, dt_bias_b):
    side = lambda f, bk: jnp.concatenate([jnp.broadcast_to(f[:, None], (GDN_HEADS, CHUNK)),
                                          jnp.broadcast_to(bk[:, None], (GDN_HEADS, CHUNK))], axis=1)
    zero = jnp.zeros((GDN_HEADS, LANES), F32)
    par = jnp.stack([side(a_log_f, a_log_b), side(dt_bias_f, dt_bias_b)] + [zero] * 6, axis=1).astype(F32)
    pad = lambda f, bk: jnp.pad(jnp.concatenate([f, bk]), (0, LANES - 2 * GDN_HEADS))
    lanepar = jnp.stack([pad(a_log_f, a_log_b), pad(dt_bias_f, dt_bias_b)]
                        + [jnp.zeros((LANES,), F32)] * 6, axis=0).astype(F32)
    return par, lanepar


def kernel(x, rel_bias, ln_mix_pre, w_in, conv_w, a_log_f, a_log_b, dt_bias_f, dt_bias_b, norm_a,
           w_branch_a, w_branch_b, w_out, ln_mix_post, ln_mlp_pre, w_ff1, w_ff2, ln_mlp_post):
    b, s, d = x.shape
    t = b * s
    nchunk = s // CHUNK
    c_small = 4 * GDN_WIDTH
    bias = _attention_bias(rel_bias)
    for l in range(ln_mix_pre.shape[0]):
        x2d = x.reshape(t, d)
        w = w_in[l]
        c_qb = c_small + N_SMALL
        c_ga = c_qb + 3 * DIL_WIDTH
        w_wide = jnp.concatenate([w[:, :c_small], w[:, c_ga:], w[:, c_qb:c_ga]], axis=1).astype(BF16)
        w_small = jnp.pad(w[:, c_small:c_qb], ((0, 0), (0, LANES - N_SMALL))).astype(BF16)
        gain = ln_mix_pre[l][None, :]
        proj = _norm_matmul(x2d, gain, w_wide, BF16, 2048, 512, "in_proj_wide")
        small = _norm_matmul(x2d, gain, w_small, F32, 1024, LANES, "in_proj_small")

        proj3 = proj.reshape(b, s, N_WIDE)
        small3 = small.reshape(b, s, LANES)
        rows = small3[:, :, :N_SMALL].reshape(b, nchunk, CHUNK, 2, 2, GDN_HEADS)
        rows = jnp.transpose(rows, (0, 5, 3, 1, 4, 2)).reshape(b, GDN_HEADS, 2, nchunk, 2 * CHUNK)
        par, lanepar = _gdn_params(a_log_f[l], a_log_b[l], dt_bias_f[l], dt_bias_b[l])
        o_a = _gdn(proj3, small3, rows, par, lanepar, conv_w[l].astype(F32), norm_a[l][None, :].astype(F32))
        o_b = _dilated(proj3, bias)

        x2d = _mix_out(x2d, o_a.reshape(t, GDN_WIDTH), o_b.reshape(t, DIL_OUT_WIDTH), proj,
                       w_branch_a[l].astype(BF16), w_branch_b[l].astype(BF16), w_out[l].astype(BF16),
                       ln_mix_post[l][None, :], 512)
        x2d = _mlp(x2d, ln_mlp_pre[l][None, :], w_ff1[l].astype(BF16), w_ff2[l].astype(BF16),
                   ln_mlp_post[l][None, :], 1024, 1024)
        x = x2d.reshape(b, s, d)
    return x
```

```python
import math

import numpy as np
import jax
import jax.numpy as jnp
from jax import lax
from jax.experimental import pallas as pl
from jax.experimental.pallas import tpu as pltpu

F32 = jnp.float32
BF16 = jnp.bfloat16
HIGHEST = lax.Precision.HIGHEST

D_MODEL = 1024
GDN_HEADS = 8
HEAD_DIM = 128
GDN_WIDTH = GDN_HEADS * HEAD_DIM
CHUNK = 64
CHUNK_GROUP = 8
ATT_GROUP = 4
CONV_K = 5
DIL_GROUPS = ((128, 1), (512, 4), (2048, 16))
DIL_HEADS_PER_GROUP = 4
DIL_HEADS = 12
DIL_WIDTH = DIL_HEADS * HEAD_DIM
DIL_OUT_WIDTH = DIL_HEADS_PER_GROUP * HEAD_DIM
HALF_WINDOW = 64
QBLK = 128
KWIN = QBLK + 2 * HALF_WINDOW
REL_BUCKETS = 32
REL_MAX_DIST = 1024
D_FF = 4 * D_MODEL
EPS = 1e-6
NEG = -1e30
N_SMALL = 4 * GDN_HEADS
LANES = 128
BIAS_LEN = 3 * LANES

CB_QA, CB_KA, CB_VA, CB_ZA = 0, 8, 16, 24
CB_GA, CB_GB = 32, 40
CB_QB, CB_KB, CB_VB = 48, 60, 72
N_WIDE = 84 * LANES

VMEM_LIMIT = 56 * 1024 * 1024


def _mm(a, b):
    return jnp.dot(a, b, preferred_element_type=F32)


def _mm_nt(a, b):
    return lax.dot_general(a, b, (((1,), (1,)), ((), ())), preferred_element_type=F32)


def _mm_hi(a, b):
    return jnp.dot(a, b, preferred_element_type=F32, precision=HIGHEST)


def _rms(x, gain):
    return x * lax.rsqrt(jnp.mean(x * x, axis=-1, keepdims=True) + EPS) * gain


def _sigmoid(x):
    return 1.0 / (1.0 + jnp.exp(-x))


def _softplus(x):
    return jnp.maximum(x, 0.0) + jnp.log(1.0 + jnp.exp(-jnp.abs(x)))


def _norm_matmul_body(x_ref, g_ref, w_ref, o_ref, h_ref):
    @pl.when(pl.program_id(1) == 0)
    def _():
        h_ref[...] = _rms(x_ref[...], g_ref[...]).astype(BF16)

    res = _mm(h_ref[...], w_ref[...])
    if len(o_ref.shape) == 2:
        o_ref[...] = res.astype(o_ref.dtype)
    else:
        for c in range(o_ref.shape[0]):
            o_ref[c] = res[:, c * LANES:(c + 1) * LANES].astype(o_ref.dtype)


def _norm_matmul_slabs(x3d, gain, w, tn, name):
    b, s, d = x3d.shape
    n = w.shape[1]
    return pl.pallas_call(
        _norm_matmul_body,
        grid=(b, n // tn),
        in_specs=[pl.BlockSpec((None, s, d), lambda i, j: (i, 0, 0)),
                  pl.BlockSpec((1, d), lambda i, j: (0, 0)),
                  pl.BlockSpec((d, tn), lambda i, j: (0, j))],
        out_specs=pl.BlockSpec((None, tn // LANES, s, LANES), lambda i, j: (i, j, 0, 0)),
        out_shape=jax.ShapeDtypeStruct((b, n // LANES, s, LANES), BF16),
        scratch_shapes=[pltpu.VMEM((s, d), BF16)],
        compiler_params=pltpu.CompilerParams(dimension_semantics=("parallel", "arbitrary"),
                                             vmem_limit_bytes=VMEM_LIMIT),
        name=name,
    )(x3d, gain, w)


def _norm_matmul(x2d, gain, w, out_dtype, tm, tn, name):
    t, d = x2d.shape
    n = w.shape[1]
    return pl.pallas_call(
        _norm_matmul_body,
        grid=(t // tm, n // tn),
        in_specs=[pl.BlockSpec((tm, d), lambda i, j: (i, 0)),
                  pl.BlockSpec((1, d), lambda i, j: (0, 0)),
                  pl.BlockSpec((d, tn), lambda i, j: (0, j))],
        out_specs=pl.BlockSpec((tm, tn), lambda i, j: (i, j)),
        out_shape=jax.ShapeDtypeStruct((t, n), out_dtype),
        scratch_shapes=[pltpu.VMEM((tm, d), BF16)],
        compiler_params=pltpu.CompilerParams(dimension_semantics=("parallel", "arbitrary"),
                                             vmem_limit_bytes=VMEM_LIMIT),
        name=name,
    )(x2d, gain, w)


def _split_bf16(x):
    hi = x.astype(BF16)
    return hi, (x - hi.astype(F32)).astype(BF16)


def _split3_lanes(x):
    hi = x.astype(BF16)
    r = x - hi.astype(F32)
    mid = r.astype(BF16)
    lo = (r - mid.astype(F32)).astype(BF16)
    return jnp.concatenate([hi, mid, lo], axis=1)


def _block_diag(x, lo):
    return jnp.concatenate([jnp.where(lo, x, 0.0), jnp.where(lo, 0.0, x)], axis=0)


def _mm_split(lhs, rhs):
    lh, ll = _split_bf16(lhs)
    rh, rl = _split_bf16(rhs)
    return (_mm(jnp.concatenate([lh, ll], axis=1), jnp.concatenate([rh, rh], axis=0))
            + _mm(lh, rl))


def _unit_tri_inverse_pairs(a2s, eye2, lo):
    xs = [_mm_split(a2, _block_diag(a2, lo)) for a2 in a2s]
    qs = [eye2 - a2 for a2 in a2s]
    levels = int(math.log2(CHUNK)) - 1
    for lvl in range(levels):
        if lvl < levels - 1:
            outs = [_mm_split(jnp.concatenate([q, x], axis=0), _block_diag(x, lo)) for q, x in zip(qs, xs)]
            qs = [q + out[:CHUNK] for q, out in zip(qs, outs)]
            xs = [out[CHUNK:] for out in outs]
        else:
            qs = [q + _mm_split(q, _block_diag(x, lo)) for q, x in zip(qs, xs)]
    return qs


def _gdn_body(par_ref, lanepar_ref, cwq_ref, cwk_ref, cwv_ref, q_ref, k_ref, v_ref, z_ref, small_ref,
              rows_ref, na_ref, o_ref,
              xp, qn, kn, vn, act, gcf, bcf, gcrow, brow, mq_s, c_s, dec_s, of_s, ob_s):
    s = q_ref.shape[0]
    nchunk = s // CHUNK
    h = pl.program_id(1)
    rc = 256

    @pl.when(h == 0)
    def _():
        lane = lax.broadcasted_iota(jnp.int32, (rc, LANES), 1)
        bi = lax.broadcasted_iota(jnp.int32, (rc, rc), 0)
        bj = lax.broadcasted_iota(jnp.int32, (rc, rc), 1)
        same_chunk = jnp.bitwise_xor(bi, bj) < CHUNK
        cum_f = jnp.where(same_chunk & (bj <= bi), 1.0, 0.0).astype(BF16)
        cum_b = jnp.where(same_chunk & (bj >= bi), 1.0, 0.0).astype(BF16)
        neg_a = -jnp.exp(lanepar_ref[0:1, :])
        dtb = lanepar_ref[1:2, :]
        sum3 = lambda y: y[:, :LANES] + y[:, LANES:2 * LANES] + y[:, 2 * LANES:]
        for c in range(s // rc):
            rows = pl.ds(c * rc, rc)
            slab = small_ref[rows, :]
            g3 = _split3_lanes(neg_a * _softplus(slab + dtb))
            act[rows, :] = jnp.where(lane < GDN_HEADS, sum3(_mm(cum_f, g3)),
                                     jnp.where(lane < 2 * GDN_HEADS, sum3(_mm(cum_b, g3)), _sigmoid(slab)))

    zeros8 = jnp.zeros((8, HEAD_DIM), F32)
    xp[pl.ds(0, 8), :] = zeros8
    xp[pl.ds(8 + s, 8), :] = zeros8

    def conv_into(src_ref, cw_ref, dst_ref, l2, scale):
        xp[pl.ds(8, s), :] = src_ref[...].astype(F32)
        for c in range(s // rc):
            acc = None
            for j in range(CONV_K):
                t = xp[pl.ds(c * rc + 8 - CONV_K // 2 + j, rc), :] * cw_ref[j:j + 1, :]
                acc = t if acc is None else acc + t
            y = acc * _sigmoid(acc)
            if l2:
                y = y * lax.rsqrt(jnp.sum(y * y, axis=-1, keepdims=True) + EPS)
                if scale != 1.0:
                    y = y * scale
            dst_ref[pl.ds(c * rc, rc), :] = y

    conv_into(q_ref, cwq_ref, qn, True, HEAD_DIM ** -0.5)
    conv_into(k_ref, cwk_ref, kn, True, 1.0)
    conv_into(v_ref, cwv_ref, vn, False, 1.0)

    sel_k = lax.broadcasted_iota(jnp.int32, (LANES, 4 * LANES), 0)
    sel_c = lax.broadcasted_iota(jnp.int32, (LANES, 4 * LANES), 1)
    sel = jnp.where(sel_k == (sel_c >> 7) * GDN_HEADS + h, 1.0, 0.0).astype(BF16)
    sel3 = jnp.concatenate([sel, sel, sel], axis=0)
    for c in range(s // rc):
        rows = pl.ds(c * rc, rc)
        picked = _mm(_split3_lanes(act[rows, :]), sel3)
        for d in range(2):
            gcf[d, rows, :] = picked[:, d * LANES:(d + 1) * LANES]
            bcf[d, rows, :] = picked[:, (2 + d) * LANES:(3 + d) * LANES]
    ri = lax.broadcasted_iota(jnp.int32, (LANES, LANES), 0)
    rj = lax.broadcasted_iota(jnp.int32, (LANES, LANES), 1)
    cum_row = jnp.where(((ri < CHUNK) & (rj < CHUNK) & (ri <= rj))
                        | ((ri >= CHUNK) & (rj >= CHUNK) & (ri >= rj)), 1.0, 0.0)
    g_row = -jnp.exp(par_ref[0:1, :]) * _softplus(rows_ref[0] + par_ref[1:2, :])
    gcrow[...] = _mm_hi(g_row, cum_row)
    brow[...] = _sigmoid(rows_ref[1])

    ii = lax.broadcasted_iota(jnp.int32, (CHUNK, 2 * CHUNK), 0)
    lj = lax.broadcasted_iota(jnp.int32, (CHUNK, 2 * CHUNK), 1)
    lo = lj < CHUNK
    hi = lj >= CHUNK
    jj = jnp.where(lo, lj, lj - CHUNK)
    eye2 = jnp.where(ii == jj, 1.0, 0.0)
    tri2 = (lo & (ii >= jj)) | (hi & (ii <= jj))
    strict2 = (lo & (ii > jj)) | (hi & (ii < jj))
    lo_t = lax.broadcasted_iota(jnp.int32, (2 * CHUNK, 2 * CHUNK), 1) < CHUNK

    def phase1(grp, carry):
        ns = [grp * CHUNK_GROUP + i for i in range(CHUNK_GROUP)]
        rws = [pl.ds(pl.multiple_of(n * CHUNK, CHUNK), CHUNK) for n in ns]
        qv = [qn[r, :] for r in rws]
        kv = [kn[r, :] for r in rws]
        kb = [k.astype(BF16) for k in kv]
        grams = [_mm_nt(jnp.concatenate([kb_, q.astype(BF16)], axis=0), jnp.concatenate([kb_, kb_], axis=0))
                 for kb_, q in zip(kb, qv)]
        gcc = [(gcf[0, r, :], gcf[1, r, :]) for r in rws]
        bcc = [(bcf[0, r, :], bcf[1, r, :]) for r in rws]
        gcr2 = [gcrow[pl.ds(n, 1), :] for n in ns]
        gams = [jnp.where(tri2, jnp.exp(jnp.where(tri2, jnp.where(lo, gf, gb) - gr, 0.0)), 0.0)
                for (gf, gb), gr in zip(gcc, gcr2)]
        a2s = [jnp.where(strict2, gram[:CHUNK] * jnp.where(lo, bf, bb) * gam, 0.0)
               for gram, (bf, bb), gam in zip(grams, bcc, gams)]
        t2s = _unit_tri_inverse_pairs(a2s, eye2, lo)
        uws = []
        for i, r in enumerate(rws):
            v = vn[r, :]
            (gf, gb), (bf, bb) = gcc[i], bcc[i]
            rhs = jnp.concatenate([jnp.concatenate([v * bf, kv[i] * (bf * jnp.exp(gf))], axis=1),
                                   jnp.concatenate([v * bb, kv[i] * (bb * jnp.exp(gb))], axis=1)], axis=0)
            uws.append(_mm(_block_diag(t2s[i], lo).astype(BF16), rhs.astype(BF16)))
        fins = []
        for i in range(CHUNK_GROUP):
            gf, gb = gcc[i]
            gtot_f = gcr2[i][:, CHUNK - 1:CHUNK]
            gtot_b = gcr2[i][:, CHUNK:CHUNK + 1]
            kg = jnp.concatenate([kv[i] * jnp.exp(gtot_f - gf), kv[i] * jnp.exp(gtot_b - gb)], axis=0)
            lhs = jnp.concatenate([_block_diag(kg.T, lo_t), _block_diag(grams[i][CHUNK:] * gams[i], lo)], axis=0)
            fins.append(_mm(lhs.astype(BF16), uws[i].astype(BF16)))
            dec_s[0, ns[i]] = jnp.broadcast_to(jnp.exp(gtot_f), (1, HEAD_DIM))
            dec_s[1, ns[i]] = jnp.broadcast_to(jnp.exp(gtot_b), (1, HEAD_DIM))
        for i, (n, r) in enumerate(zip(ns, rws)):
            fin = fins[i]
            for d in range(2):
                top = fin[d * HEAD_DIM:(d + 1) * HEAD_DIM]
                bot = fin[2 * HEAD_DIM + d * CHUNK:2 * HEAD_DIM + (d + 1) * CHUNK]
                c_s[d, n] = top[:, :HEAD_DIM]
                mq_s[d, n, pl.ds(0, HEAD_DIM), :] = (-top[:, HEAD_DIM:]).astype(BF16)
                mq_s[d, n, pl.ds(HEAD_DIM, CHUNK), :] = (qv[i] * jnp.exp(gcc[i][d]) - bot[:, HEAD_DIM:]).astype(BF16)
                (of_s if d == 0 else ob_s)[r, :] = bot[:, :HEAD_DIM]
        return carry

    lax.fori_loop(0, nchunk // CHUNK_GROUP, phase1, 0)

    def phase2(step, states):
        new = []
        for d, (st, out_s) in enumerate(zip(states, (of_s, ob_s))):
            n = step if d == 0 else nchunk - 1 - step
            r = _mm(mq_s[d, n], st.astype(BF16))
            rows = pl.ds(pl.multiple_of(n * CHUNK, CHUNK), CHUNK)
            out_s[rows, :] = out_s[rows, :] + r[HEAD_DIM:]
            new.append(st * dec_s[d, n] + r[:HEAD_DIM] + c_s[d, n])
        return tuple(new)

    zero_state = jnp.zeros((HEAD_DIM, HEAD_DIM), F32)
    lax.fori_loop(0, nchunk, phase2, (zero_state, zero_state))

    for c in range(s // rc):
        rows = pl.ds(c * rc, rc)
        o = of_s[rows, :] + ob_s[rows, :]
        z = z_ref[rows, :].astype(F32)
        o_ref[rows, :] = (_rms(o, na_ref[...]) * (z * _sigmoid(z))).astype(o_ref.dtype)


def _gdn(proj, small, rows, par, lanepar, conv_w, norm_a):
    b, _, s, _ = proj.shape
    nchunk = s // CHUNK
    assert 2 * CHUNK == LANES == HEAD_DIM and nchunk % CHUNK_GROUP == 0
    col = lambda cb: pl.BlockSpec((None, None, s, HEAD_DIM), lambda i, h, cb=cb: (i, cb + h, 0, 0))
    cw = lambda cb: pl.BlockSpec((CONV_K, HEAD_DIM), lambda i, h, cb=cb: (0, cb + h))
    return pl.pallas_call(
        _gdn_body,
        grid=(b, GDN_HEADS),
        in_specs=[pl.BlockSpec((None, 8, LANES), lambda i, h: (h, 0, 0)),
                  pl.BlockSpec((8, LANES), lambda i, h: (0, 0)),
                  cw(0), cw(GDN_HEADS), cw(2 * GDN_HEADS),
                  col(CB_QA), col(CB_KA), col(CB_VA), col(CB_ZA),
                  pl.BlockSpec((None, s, LANES), lambda i, h: (i, 0, 0)),
                  pl.BlockSpec((None, None, 2, nchunk, LANES), lambda i, h: (i, h, 0, 0, 0)),
                  pl.BlockSpec((1, HEAD_DIM), lambda i, h: (0, 0))],
        out_specs=pl.BlockSpec((None, None, s, HEAD_DIM), lambda i, h: (i, h, 0, 0)),
        out_shape=jax.ShapeDtypeStruct((b, GDN_HEADS, s, HEAD_DIM), BF16),
        scratch_shapes=[
            pltpu.VMEM((s + 16, HEAD_DIM), F32),
            pltpu.VMEM((s, HEAD_DIM), F32),
            pltpu.VMEM((s, HEAD_DIM), F32),
            pltpu.VMEM((s, HEAD_DIM), F32),
            pltpu.VMEM((s, LANES), F32),
            pltpu.VMEM((2, s, LANES), F32),
            pltpu.VMEM((2, s, LANES), F32),
            pltpu.VMEM((nchunk, LANES), F32),
            pltpu.VMEM((nchunk, LANES), F32),
            pltpu.VMEM((2, nchunk, HEAD_DIM + CHUNK, HEAD_DIM), BF16),
            pltpu.VMEM((2, nchunk, HEAD_DIM, HEAD_DIM), F32),
            pltpu.VMEM((2, nchunk, 1, HEAD_DIM), F32),
            pltpu.VMEM((s, HEAD_DIM), F32),
            pltpu.VMEM((s, HEAD_DIM), F32),
        ],
        compiler_params=pltpu.CompilerParams(dimension_semantics=("parallel", "arbitrary"),
                                             vmem_limit_bytes=VMEM_LIMIT),
        name="gdn",
    )(par, lanepar, conv_w, conv_w, conv_w, proj, proj, proj, proj, small, rows, norm_a)


def _dil_body(q0, k0, v0, q1, k1, v1, q2, k2, v2, bias_ref, o_ref,
              qf, kf, vf, qs, kp, vp, acc_s, m_s, l_s):
    s = q0.shape[0]
    scale = HEAD_DIM ** -0.5
    qi = lax.broadcasted_iota(jnp.int32, (QBLK, KWIN), 0)
    kj = lax.broadcasted_iota(jnp.int32, (QBLK, KWIN), 1)
    off = kj - HALF_WINDOW - qi
    band = (off >= -HALF_WINDOW) & (off <= HALF_WINDOW)
    zpad = jnp.zeros((HALF_WINDOW, HEAD_DIM), F32)

    for g, (q_ref, k_ref, v_ref) in enumerate(((q0, k0, v0), (q1, k1, v1), (q2, k2, v2))):
        dil = DIL_GROUPS[g][1]
        length = s // dil
        nblk = length // QBLK
        qf[...] = q_ref[...].astype(F32)
        kf[...] = k_ref[...].astype(F32)
        vf[...] = v_ref[...].astype(F32)
        seg_q = max(length, QBLK)
        seg_k = seg_q + 2 * HALF_WINDOW
        nseg = ATT_GROUP if nblk < ATT_GROUP else 1
        for c in range(nseg):
            for buf in (kp, vp):
                buf[pl.ds(c * seg_k, HALF_WINDOW), :] = zpad
                buf[pl.ds(c * seg_k + HALF_WINDOW + length, HALF_WINDOW), :] = zpad
        bias = pltpu.roll(jnp.broadcast_to(bias_ref[g], (QBLK, BIAS_LEN)), 0, 1,
                          stride=1, stride_axis=0)[:, :KWIN]

        def stage(c, r, dil=dil, length=length, seg_q=seg_q, seg_k=seg_k):
            cls = pl.ds(r, length, stride=dil) if dil > 1 else pl.ds(0, length)
            qs[pl.ds(c * seg_q, length), :] = qf[cls, :]
            kp[pl.ds(c * seg_k + HALF_WINDOW, length), :] = kf[cls, :]
            vp[pl.ds(c * seg_k + HALF_WINDOW, length), :] = vf[cls, :]

        def attend(items, g=g, dil=dil, length=length, bias=bias):
            logits = []
            for q_off, k_off, j0, r in items:
                kpos = j0 - HALF_WINDOW + kj
                ok = band & (kpos >= 0) & (kpos < length)
                sc = _mm_nt(qs[pl.ds(q_off, QBLK), :].astype(BF16), kp[pl.ds(k_off, KWIN), :].astype(BF16))
                logits.append(jnp.where(ok, sc * scale + bias, NEG))
            ms = [jnp.max(lg, axis=-1, keepdims=True) for lg in logits]
            ps = [jnp.exp(lg - m) for lg, m in zip(logits, ms)]
            ls = [jnp.sum(p, axis=-1, keepdims=True) for p in ps]
            accs = [_mm(p.astype(BF16), vp[pl.ds(it[1], KWIN), :].astype(BF16)) for p, it in zip(ps, items)]
            for (q_off, k_off, j0, r), m, lsum, acc in zip(items, ms, ls, accs):
                tok = pl.ds(r + dil * j0, QBLK, stride=dil) if dil > 1 else pl.ds(j0, QBLK)
                acc_s[g, tok, :] = acc
                m_s[g, tok, :] = m
                l_s[g, tok, :] = lsum

        if nseg == 1:
            def residue(r, carry, nblk=nblk, stage=stage, attend=attend):
                stage(0, r)

                def blocks(it, carry2):
                    j0s = [pl.multiple_of((it * ATT_GROUP + i) * QBLK, QBLK) for i in range(ATT_GROUP)]
                    attend([(j0, j0, j0, r) for j0 in j0s])
                    return carry2

                lax.fori_loop(0, nblk // ATT_GROUP, blocks, 0)
                return carry

            lax.fori_loop(0, dil, residue, 0)
        else:
            def classes(it, carry, seg_q=seg_q, seg_k=seg_k, stage=stage, attend=attend):
                rs = [it * ATT_GROUP + c for c in range(ATT_GROUP)]
                for c, r in enumerate(rs):
                    stage(c, r)
                attend([(c * seg_q, c * seg_k, 0, r) for c, r in enumerate(rs)])
                return carry

            lax.fori_loop(0, dil // ATT_GROUP, classes, 0)

    rc = 256
    for c in range(s // rc):
        rows = pl.ds(c * rc, rc)
        ms = [m_s[g, rows, :] for g in range(3)]
        mx = jnp.maximum(jnp.maximum(ms[0], ms[1]), ms[2])
        ws = [jnp.exp(ms[g] - mx) for g in range(3)]
        den = ws[0] * l_s[0, rows, :] + ws[1] * l_s[1, rows, :] + ws[2] * l_s[2, rows, :]
        num = ws[0] * acc_s[0, rows, :] + ws[1] * acc_s[1, rows, :] + ws[2] * acc_s[2, rows, :]
        o_ref[rows, :] = (num / den).astype(o_ref.dtype)


def _dilated(proj, bias):
    b, _, s, _ = proj.shape
    specs = []
    for g in range(3):
        for cb in (CB_QB, CB_KB, CB_VB):
            specs.append(pl.BlockSpec((None, None, s, HEAD_DIM),
                                      lambda i, h, cb=cb, g=g: (i, cb + g * DIL_HEADS_PER_GROUP + h, 0, 0)))
    specs.append(pl.BlockSpec((None, 3, 1, BIAS_LEN), lambda i, h: (h, 0, 0, 0)))
    return pl.pallas_call(
        _dil_body,
        grid=(b, DIL_HEADS_PER_GROUP),
        in_specs=specs,
        out_specs=pl.BlockSpec((None, None, s, HEAD_DIM), lambda i, h: (i, h, 0, 0)),
        out_shape=jax.ShapeDtypeStruct((b, DIL_HEADS_PER_GROUP, s, HEAD_DIM), BF16),
        scratch_shapes=[
            pltpu.VMEM((s, HEAD_DIM), F32),
            pltpu.VMEM((s, HEAD_DIM), F32),
            pltpu.VMEM((s, HEAD_DIM), F32),
            pltpu.VMEM((s, HEAD_DIM), F32),
            pltpu.VMEM((s + 2 * HALF_WINDOW, HEAD_DIM), F32),
            pltpu.VMEM((s + 2 * HALF_WINDOW, HEAD_DIM), F32),
            pltpu.VMEM((3, s, HEAD_DIM), F32),
            pltpu.VMEM((3, s, 1), F32),
            pltpu.VMEM((3, s, 1), F32),
        ],
        compiler_params=pltpu.CompilerParams(dimension_semantics=("parallel", "arbitrary"),
                                             vmem_limit_bytes=VMEM_LIMIT),
        name="dilated_attn",
    )(*([proj] * 9), bias)


def _mix_out_body(x_ref, oa_ref, ob_ref, ga_ref, gb_ref, wa_ref, wb_ref, wo_ref, ln_ref, o_ref):
    wide = lambda ref: jnp.concatenate([ref[c] for c in range(ref.shape[0])], axis=1)
    ya = _mm(wide(oa_ref), wa_ref[...])
    yb = _mm(wide(ob_ref), wb_ref[...])
    merged = _sigmoid(wide(ga_ref).astype(F32)) * ya + _sigmoid(wide(gb_ref).astype(F32)) * yb
    y = _mm(merged.astype(BF16), wo_ref[...])
    o_ref[...] = x_ref[...] + _rms(y, ln_ref[...])


def _mix_out(x3d, o_a, o_b, proj, w_a, w_b, w_o, ln_post, tm):
    b, s, d = x3d.shape
    nslab = d // LANES
    assert CB_GA % nslab == 0 and CB_GB % nslab == 0 and s % tm == 0
    const = lambda shape: pl.BlockSpec(shape, lambda i, j: (0, 0))
    slabs = lambda n, blk: pl.BlockSpec((None, n, tm, LANES), lambda i, j, blk=blk: (i, blk, j, 0))
    return pl.pallas_call(
        _mix_out_body,
        grid=(b, s // tm),
        in_specs=[pl.BlockSpec((None, tm, d), lambda i, j: (i, j, 0)),
                  slabs(GDN_HEADS, 0), slabs(DIL_HEADS_PER_GROUP, 0),
                  slabs(nslab, CB_GA // nslab), slabs(nslab, CB_GB // nslab),
                  const((GDN_WIDTH, d)), const((DIL_OUT_WIDTH, d)), const((d, d)), const((1, d))],
        out_specs=pl.BlockSpec((None, tm, d), lambda i, j: (i, j, 0)),
        out_shape=jax.ShapeDtypeStruct((b, s, d), F32),
        compiler_params=pltpu.CompilerParams(dimension_semantics=("parallel", "parallel"),
                                             vmem_limit_bytes=VMEM_LIMIT),
        name="mix_out",
    )(x3d, o_a, o_b, proj, proj, w_a, w_b, w_o, ln_post)


def _mlp_body(x_ref, g1_ref, w1_ref, w2_ref, g2_ref, o_ref, h_ref, acc_ref):
    k = pl.program_id(1)

    @pl.when(k == 0)
    def _():
        h_ref[...] = _rms(x_ref[...], g1_ref[...]).astype(BF16)

    f = jnp.maximum(_mm(h_ref[...], w1_ref[...]), 0.0)
    part = _mm((f * f).astype(BF16), w2_ref[...])

    @pl.when(k == 0)
    def _():
        acc_ref[...] = part

    @pl.when(k > 0)
    def _():
        acc_ref[...] += part

    @pl.when(k == pl.num_programs(1) - 1)
    def _():
        o_ref[...] = x_ref[...] + _rms(acc_ref[...], g2_ref[...])


def _mlp(x2d, ln_pre, w1, w2, ln_post, tm, tf):
    t, d = x2d.shape
    dff = w1.shape[1]
    return pl.pallas_call(
        _mlp_body,
        grid=(t // tm, dff // tf),
        in_specs=[pl.BlockSpec((tm, d), lambda i, k: (i, 0)),
                  pl.BlockSpec((1, d), lambda i, k: (0, 0)),
                  pl.BlockSpec((d, tf), lambda i, k: (0, k)),
                  pl.BlockSpec((tf, d), lambda i, k: (k, 0)),
                  pl.BlockSpec((1, d), lambda i, k: (0, 0))],
        out_specs=pl.BlockSpec((tm, d), lambda i, k: (i, 0)),
        out_shape=jax.ShapeDtypeStruct((t, d), F32),
        scratch_shapes=[pltpu.VMEM((tm, d), BF16), pltpu.VMEM((tm, d), F32)],
        compiler_params=pltpu.CompilerParams(dimension_semantics=("parallel", "arbitrary"),
                                             vmem_limit_bytes=VMEM_LIMIT),
        name="mlp",
    )(x2d, ln_pre, w1, w2, ln_post)


def _t5_bucket_np(rel):
    nb = REL_BUCKETS // 2
    ret = (rel > 0).astype(np.int32) * nb
    n = np.abs(rel)
    max_exact = nb // 2
    large = max_exact + (np.log(np.maximum(n, 1) / max_exact) / math.log(REL_MAX_DIST / max_exact)
                         * (nb - max_exact)).astype(np.int32)
    large = np.minimum(large, nb - 1)
    return ret + np.where(n < max_exact, n, large).astype(np.int32)


def _attention_bias(rel_bias):
    c = np.arange(BIAS_LEN)
    off = np.where(c < KWIN, np.clip(c - HALF_WINDOW, -HALF_WINDOW, HALF_WINDOW), -HALF_WINDOW)
    per_group = []
    for gi, (_, dil) in enumerate(DIL_GROUPS):
        bt = rel_bias[_t5_bucket_np(off * dil)]
        per_group.append(bt[:, gi * DIL_HEADS_PER_GROUP:(gi + 1) * DIL_HEADS_PER_GROUP])
    return jnp.transpose(jnp.stack(per_group, axis=0), (2, 0, 1))[:, :, None, :].astype(F32)


def _gdn_params(a_log_f, a_log_b, dt_bias_f, dt_bias_b):
    side = lambda f, bk: jnp.concatenate([jnp.broadcast_to(f[:, None], (GDN_HEADS, CHUNK)),
                                          jnp.broadcast_to(bk[:, None], (GDN_HEADS, CHUNK))], axis=1)
    zero = jnp.zeros((GDN_HEADS, LANES), F32)
    par = jnp.stack([side(a_log_f, a_log_b), side(dt_bias_f, dt_bias_b)] + [zero] * 6, axis=1).astype(F32)
    pad = lambda f, bk: jnp.pad(jnp.concatenate([f, bk]), (0, LANES - 2 * GDN_HEADS))
    lanepar = jnp.stack([pad(a_log_f, a_log_b), pad(dt_bias_f, dt_bias_b)]
                        + [jnp.zeros((LANES,), F32)] * 6, axis=0).astype(F32)
    return par, lanepar


def kernel(x, rel_bias, ln_mix_pre, w_in, conv_w, a_log_f, a_log_b, dt_bias_f, dt_bias_b, norm_a,
           w_branch_a, w_branch_b, w_out, ln_mix_post, ln_mlp_pre, w_ff1, w_ff2, ln_mlp_post):
    b, s, d = x.shape
    t = b * s
    nchunk = s // CHUNK
    c_small = 4 * GDN_WIDTH
    bias = _attention_bias(rel_bias)
    for l in range(ln_mix_pre.shape[0]):
        x2d = x.reshape(t, d)
        w = w_in[l]
        c_qb = c_small + N_SMALL
        c_ga = c_qb + 3 * DIL_WIDTH
        w_wide = jnp.concatenate([w[:, :c_small], w[:, c_ga:], w[:, c_qb:c_ga]], axis=1).astype(BF16)
        w_small = jnp.pad(w[:, c_small:c_qb], ((0, 0), (0, LANES - N_SMALL))).astype(BF16)
        gain = ln_mix_pre[l][None, :]
        proj = _norm_matmul_slabs(x, gain, w_wide, 512, "in_proj_wide")
        small = _norm_matmul(x2d, gain, w_small, F32, 1024, LANES, "in_proj_small")

        small3 = small.reshape(b, s, LANES)
        rows = small3[:, :, :N_SMALL].reshape(b, nchunk, CHUNK, 2, 2, GDN_HEADS)
        rows = jnp.transpose(rows, (0, 5, 3, 1, 4, 2)).reshape(b, GDN_HEADS, 2, nchunk, 2 * CHUNK)
        par, lanepar = _gdn_params(a_log_f[l], a_log_b[l], dt_bias_f[l], dt_bias_b[l])
        o_a = _gdn(proj, small3, rows, par, lanepar, conv_w[l].astype(F32), norm_a[l][None, :].astype(F32))
        o_b = _dilated(proj, bias)

        x1 = _mix_out(x, o_a, o_b, proj,
                      w_branch_a[l].astype(BF16), w_branch_b[l].astype(BF16), w_out[l].astype(BF16),
                      ln_mix_post[l][None, :], 512)
        x2d = _mlp(x1.reshape(t, d), ln_mlp_pre[l][None, :], w_ff1[l].astype(BF16), w_ff2[l].astype(BF16),
                   ln_mlp_post[l][None, :], 1024, 1024)
        x = x2d.reshape(b, s, d)
    return x
```

```python
import math

import numpy as np
import jax
import jax.numpy as jnp
from jax import lax
from jax.experimental import pallas as pl
from jax.experimental.pallas import tpu as pltpu

F32 = jnp.float32
BF16 = jnp.bfloat16
HIGHEST = lax.Precision.HIGHEST

D_MODEL = 1024
GDN_HEADS = 8
HEAD_DIM = 128
GDN_WIDTH = GDN_HEADS * HEAD_DIM
CHUNK = 64
CHUNK_GROUP = 8
GDN_HEADS_PER_STEP = 2
ATT_GROUP = 4
CONV_K = 5
DIL_GROUPS = ((128, 1), (512, 4), (2048, 16))
DIL_HEADS_PER_GROUP = 4
DIL_HEADS = 12
DIL_WIDTH = DIL_HEADS * HEAD_DIM
DIL_OUT_WIDTH = DIL_HEADS_PER_GROUP * HEAD_DIM
HALF_WINDOW = 64
QBLK = 128
KWIN = QBLK + 2 * HALF_WINDOW
REL_BUCKETS = 32
REL_MAX_DIST = 1024
D_FF = 4 * D_MODEL
EPS = 1e-6
NEG = -1e30
N_SMALL = 4 * GDN_HEADS
LANES = 128
BIAS_LEN = 3 * LANES

CB_QA, CB_KA, CB_VA, CB_ZA = 0, 8, 16, 24
CB_GA, CB_GB = 32, 40
CB_QB, CB_KB, CB_VB = 48, 60, 72
N_WIDE = 84 * LANES

VMEM_LIMIT = 56 * 1024 * 1024


def _mm(a, b):
    return jnp.dot(a, b, preferred_element_type=F32)


def _mm_nt(a, b):
    return lax.dot_general(a, b, (((1,), (1,)), ((), ())), preferred_element_type=F32)


def _mm_hi(a, b):
    return jnp.dot(a, b, preferred_element_type=F32, precision=HIGHEST)


def _rms(x, gain):
    return x * lax.rsqrt(jnp.mean(x * x, axis=-1, keepdims=True) + EPS) * gain


def _sigmoid(x):
    return 1.0 / (1.0 + jnp.exp(-x))


def _softplus(x):
    return jnp.maximum(x, 0.0) + jnp.log(1.0 + jnp.exp(-jnp.abs(x)))


def _norm_matmul_body(x_ref, g_ref, w_ref, o_ref, h_ref):
    @pl.when(pl.program_id(1) == 0)
    def _():
        h_ref[...] = _rms(x_ref[...], g_ref[...]).astype(BF16)

    res = _mm(h_ref[...], w_ref[...])
    if len(o_ref.shape) == 2:
        o_ref[...] = res.astype(o_ref.dtype)
    else:
        for c in range(o_ref.shape[0]):
            o_ref[c] = res[:, c * LANES:(c + 1) * LANES].astype(o_ref.dtype)


def _norm_matmul_slabs(x3d, gain, w, tn, name):
    b, s, d = x3d.shape
    n = w.shape[1]
    return pl.pallas_call(
        _norm_matmul_body,
        grid=(b, n // tn),
        in_specs=[pl.BlockSpec((None, s, d), lambda i, j: (i, 0, 0)),
                  pl.BlockSpec((1, d), lambda i, j: (0, 0)),
                  pl.BlockSpec((d, tn), lambda i, j: (0, j))],
        out_specs=pl.BlockSpec((None, tn // LANES, s, LANES), lambda i, j: (i, j, 0, 0)),
        out_shape=jax.ShapeDtypeStruct((b, n // LANES, s, LANES), BF16),
        scratch_shapes=[pltpu.VMEM((s, d), BF16)],
        compiler_params=pltpu.CompilerParams(dimension_semantics=("parallel", "arbitrary"),
                                             vmem_limit_bytes=VMEM_LIMIT),
        name=name,
    )(x3d, gain, w)


def _norm_matmul(x2d, gain, w, out_dtype, tm, tn, name):
    t, d = x2d.shape
    n = w.shape[1]
    return pl.pallas_call(
        _norm_matmul_body,
        grid=(t // tm, n // tn),
        in_specs=[pl.BlockSpec((tm, d), lambda i, j: (i, 0)),
                  pl.BlockSpec((1, d), lambda i, j: (0, 0)),
                  pl.BlockSpec((d, tn), lambda i, j: (0, j))],
        out_specs=pl.BlockSpec((tm, tn), lambda i, j: (i, j)),
        out_shape=jax.ShapeDtypeStruct((t, n), out_dtype),
        scratch_shapes=[pltpu.VMEM((tm, d), BF16)],
        compiler_params=pltpu.CompilerParams(dimension_semantics=("parallel", "arbitrary"),
                                             vmem_limit_bytes=VMEM_LIMIT),
        name=name,
    )(x2d, gain, w)


def _split_bf16(x):
    hi = x.astype(BF16)
    return hi, (x - hi.astype(F32)).astype(BF16)


def _split3_lanes(x):
    hi = x.astype(BF16)
    r = x - hi.astype(F32)
    mid = r.astype(BF16)
    lo = (r - mid.astype(F32)).astype(BF16)
    return jnp.concatenate([hi, mid, lo], axis=1)


def _block_diag(x, lo):
    return jnp.concatenate([jnp.where(lo, x, 0.0), jnp.where(lo, 0.0, x)], axis=0)


def _mm_split(lhs, rhs):
    lh, ll = _split_bf16(lhs)
    rh, rl = _split_bf16(rhs)
    return (_mm(jnp.concatenate([lh, ll], axis=1), jnp.concatenate([rh, rh], axis=0))
            + _mm(lh, rl))


def _unit_tri_inverse_pairs(a2s, eye2, lo):
    xs = [_mm_split(a2, _block_diag(a2, lo)) for a2 in a2s]
    qs = [eye2 - a2 for a2 in a2s]
    levels = int(math.log2(CHUNK)) - 1
    for lvl in range(levels):
        if lvl < levels - 1:
            outs = [_mm_split(jnp.concatenate([q, x], axis=0), _block_diag(x, lo)) for q, x in zip(qs, xs)]
            qs = [q + out[:CHUNK] for q, out in zip(qs, outs)]
            xs = [out[CHUNK:] for out in outs]
        else:
            qs = [q + _mm_split(q, _block_diag(x, lo)) for q, x in zip(qs, xs)]
    return qs


def _gdn_body(par_ref, lanepar_ref, cwq_ref, cwk_ref, cwv_ref, q_ref, k_ref, v_ref, z_ref, small_ref,
              rows_ref, na_ref, o_ref,
              xp, qn, kn, vn, act, gcf, bcf, gcrow, brow, mq_s, c_s, dec_s, of_s, ob_s):
    nh, s, _ = q_ref.shape
    nchunk = s // CHUNK
    hp = pl.program_id(1)
    rc = 256

    @pl.when(hp == 0)
    def _():
        lane = lax.broadcasted_iota(jnp.int32, (rc, LANES), 1)
        bi = lax.broadcasted_iota(jnp.int32, (rc, rc), 0)
        bj = lax.broadcasted_iota(jnp.int32, (rc, rc), 1)
        same_chunk = jnp.bitwise_xor(bi, bj) < CHUNK
        cum_f = jnp.where(same_chunk & (bj <= bi), 1.0, 0.0).astype(BF16)
        cum_b = jnp.where(same_chunk & (bj >= bi), 1.0, 0.0).astype(BF16)
        neg_a = -jnp.exp(lanepar_ref[0:1, :])
        dtb = lanepar_ref[1:2, :]
        sum3 = lambda y: y[:, :LANES] + y[:, LANES:2 * LANES] + y[:, 2 * LANES:]
        for c in range(s // rc):
            rows = pl.ds(c * rc, rc)
            slab = small_ref[rows, :]
            g3 = _split3_lanes(neg_a * _softplus(slab + dtb))
            act[rows, :] = jnp.where(lane < GDN_HEADS, sum3(_mm(cum_f, g3)),
                                     jnp.where(lane < 2 * GDN_HEADS, sum3(_mm(cum_b, g3)), _sigmoid(slab)))

    for hh in range(nh):
        lanes = pl.ds(hh * HEAD_DIM, HEAD_DIM)
        _gdn_prepare(hp * nh + hh, par_ref.at[hh], cwq_ref.at[:, lanes], cwk_ref.at[:, lanes], cwv_ref.at[:, lanes],
                     q_ref.at[hh], k_ref.at[hh], v_ref.at[hh], rows_ref.at[hh],
                     xp, qn, kn, vn, act, gcf, bcf, gcrow, brow,
                     mq_s.at[hh], c_s.at[hh], dec_s.at[hh], of_s.at[hh], ob_s.at[hh])

    def phase2(step, states):
        new = []
        for idx, st in enumerate(states):
            hh, d = divmod(idx, 2)
            n = step if d == 0 else nchunk - 1 - step
            r = _mm(mq_s[hh, d, n], st.astype(BF16))
            rows = pl.ds(pl.multiple_of(n * CHUNK, CHUNK), CHUNK)
            out_s = of_s if d == 0 else ob_s
            out_s[hh, rows, :] = out_s[hh, rows, :] + r[HEAD_DIM:]
            new.append(st * dec_s[hh, d, n] + r[:HEAD_DIM] + c_s[hh, d, n])
        return tuple(new)

    zero_state = jnp.zeros((HEAD_DIM, HEAD_DIM), F32)
    lax.fori_loop(0, nchunk, phase2, (zero_state,) * (2 * nh))

    for hh in range(nh):
        for c in range(s // rc):
            rows = pl.ds(c * rc, rc)
            o = of_s[hh, rows, :] + ob_s[hh, rows, :]
            z = z_ref[hh, rows, :].astype(F32)
            o_ref[hh, rows, :] = (_rms(o, na_ref[...]) * (z * _sigmoid(z))).astype(o_ref.dtype)


def _gdn_prepare(h, par_ref, cwq_ref, cwk_ref, cwv_ref, q_ref, k_ref, v_ref, rows_ref,
                 xp, qn, kn, vn, act, gcf, bcf, gcrow, brow, mq_s, c_s, dec_s, of_s, ob_s):
    s = q_ref.shape[0]
    nchunk = s // CHUNK
    rc = 256

    zeros8 = jnp.zeros((8, HEAD_DIM), F32)
    xp[pl.ds(0, 8), :] = zeros8
    xp[pl.ds(8 + s, 8), :] = zeros8

    def conv_into(src_ref, cw_ref, dst_ref, l2, scale):
        xp[pl.ds(8, s), :] = src_ref[...].astype(F32)
        for c in range(s // rc):
            acc = None
            for j in range(CONV_K):
                t = xp[pl.ds(c * rc + 8 - CONV_K // 2 + j, rc), :] * cw_ref[j:j + 1, :]
                acc = t if acc is None else acc + t
            y = acc * _sigmoid(acc)
            if l2:
                y = y * lax.rsqrt(jnp.sum(y * y, axis=-1, keepdims=True) + EPS)
                if scale != 1.0:
                    y = y * scale
            dst_ref[pl.ds(c * rc, rc), :] = y

    conv_into(q_ref, cwq_ref, qn, True, HEAD_DIM ** -0.5)
    conv_into(k_ref, cwk_ref, kn, True, 1.0)
    conv_into(v_ref, cwv_ref, vn, False, 1.0)

    sel_k = lax.broadcasted_iota(jnp.int32, (LANES, 4 * LANES), 0)
    sel_c = lax.broadcasted_iota(jnp.int32, (LANES, 4 * LANES), 1)
    sel = jnp.where(sel_k == (sel_c >> 7) * GDN_HEADS + h, 1.0, 0.0).astype(BF16)
    sel3 = jnp.concatenate([sel, sel, sel], axis=0)
    for c in range(s // rc):
        rows = pl.ds(c * rc, rc)
        picked = _mm(_split3_lanes(act[rows, :]), sel3)
        for d in range(2):
            gcf[d, rows, :] = picked[:, d * LANES:(d + 1) * LANES]
            bcf[d, rows, :] = picked[:, (2 + d) * LANES:(3 + d) * LANES]
    ri = lax.broadcasted_iota(jnp.int32, (LANES, LANES), 0)
    rj = lax.broadcasted_iota(jnp.int32, (LANES, LANES), 1)
    cum_row = jnp.where(((ri < CHUNK) & (rj < CHUNK) & (ri <= rj))
                        | ((ri >= CHUNK) & (rj >= CHUNK) & (ri >= rj)), 1.0, 0.0)
    g_row = -jnp.exp(par_ref[0:1, :]) * _softplus(rows_ref[0] + par_ref[1:2, :])
    gcrow[...] = _mm_hi(g_row, cum_row)
    brow[...] = _sigmoid(rows_ref[1])

    ii = lax.broadcasted_iota(jnp.int32, (CHUNK, 2 * CHUNK), 0)
    lj = lax.broadcasted_iota(jnp.int32, (CHUNK, 2 * CHUNK), 1)
    lo = lj < CHUNK
    hi = lj >= CHUNK
    jj = jnp.where(lo, lj, lj - CHUNK)
    eye2 = jnp.where(ii == jj, 1.0, 0.0)
    tri2 = (lo & (ii >= jj)) | (hi & (ii <= jj))
    strict2 = (lo & (ii > jj)) | (hi & (ii < jj))
    lo_t = lax.broadcasted_iota(jnp.int32, (2 * CHUNK, 2 * CHUNK), 1) < CHUNK

    def phase1(grp, carry):
        ns = [grp * CHUNK_GROUP + i for i in range(CHUNK_GROUP)]
        rws = [pl.ds(pl.multiple_of(n * CHUNK, CHUNK), CHUNK) for n in ns]
        qv = [qn[r, :] for r in rws]
        kv = [kn[r, :] for r in rws]
        kb = [k.astype(BF16) for k in kv]
        grams = [_mm_nt(jnp.concatenate([kb_, q.astype(BF16)], axis=0), jnp.concatenate([kb_, kb_], axis=0))
                 for kb_, q in zip(kb, qv)]
        gcc = [(gcf[0, r, :], gcf[1, r, :]) for r in rws]
        bcc = [(bcf[0, r, :], bcf[1, r, :]) for r in rws]
        gcr2 = [gcrow[pl.ds(n, 1), :] for n in ns]
        gams = [jnp.where(tri2, jnp.exp(jnp.where(tri2, jnp.where(lo, gf, gb) - gr, 0.0)), 0.0)
                for (gf, gb), gr in zip(gcc, gcr2)]
        a2s = [jnp.where(strict2, gram[:CHUNK] * jnp.where(lo, bf, bb) * gam, 0.0)
               for gram, (bf, bb), gam in zip(grams, bcc, gams)]
        t2s = _unit_tri_inverse_pairs(a2s, eye2, lo)
        uws = []
        for i, r in enumerate(rws):
            v = vn[r, :]
            (gf, gb), (bf, bb) = gcc[i], bcc[i]
            rhs = jnp.concatenate([jnp.concatenate([v * bf, kv[i] * (bf * jnp.exp(gf))], axis=1),
                                   jnp.concatenate([v * bb, kv[i] * (bb * jnp.exp(gb))], axis=1)], axis=0)
            uws.append(_mm(_block_diag(t2s[i], lo).astype(BF16), rhs.astype(BF16)))
        fins = []
        for i in range(CHUNK_GROUP):
            gf, gb = gcc[i]
            gtot_f = gcr2[i][:, CHUNK - 1:CHUNK]
            gtot_b = gcr2[i][:, CHUNK:CHUNK + 1]
            kg = jnp.concatenate([kv[i] * jnp.exp(gtot_f - gf), kv[i] * jnp.exp(gtot_b - gb)], axis=0)
            lhs = jnp.concatenate([_block_diag(kg.T, lo_t), _block_diag(grams[i][CHUNK:] * gams[i], lo)], axis=0)
            fins.append(_mm(lhs.astype(BF16), uws[i].astype(BF16)))
            dec_s[0, ns[i]] = jnp.broadcast_to(jnp.exp(gtot_f), (1, HEAD_DIM))
            dec_s[1, ns[i]] = jnp.broadcast_to(jnp.exp(gtot_b), (1, HEAD_DIM))
        for i, (n, r) in enumerate(zip(ns, rws)):
            fin = fins[i]
            for d in range(2):
                top = fin[d * HEAD_DIM:(d + 1) * HEAD_DIM]
                bot = fin[2 * HEAD_DIM + d * CHUNK:2 * HEAD_DIM + (d + 1) * CHUNK]
                c_s[d, n] = top[:, :HEAD_DIM]
                mq_s[d, n, pl.ds(0, HEAD_DIM), :] = (-top[:, HEAD_DIM:]).astype(BF16)
                mq_s[d, n, pl.ds(HEAD_DIM, CHUNK), :] = (qv[i] * jnp.exp(gcc[i][d]) - bot[:, HEAD_DIM:]).astype(BF16)
                (of_s if d == 0 else ob_s)[r, :] = bot[:, :HEAD_DIM]
        return carry

    lax.fori_loop(0, nchunk // CHUNK_GROUP, phase1, 0)


def _gdn(proj, small, rows, par, lanepar, conv_w, norm_a):
    b, _, s, _ = proj.shape
    nchunk = s // CHUNK
    nh = GDN_HEADS_PER_STEP
    assert 2 * CHUNK == LANES == HEAD_DIM and nchunk % CHUNK_GROUP == 0 and GDN_HEADS % nh == 0
    assert all(cb % nh == 0 for cb in (CB_QA, CB_KA, CB_VA, CB_ZA))
    col = lambda cb: pl.BlockSpec((None, nh, s, HEAD_DIM), lambda i, h, cb=cb: (i, cb // nh + h, 0, 0))
    cw = lambda cb: pl.BlockSpec((CONV_K, nh * HEAD_DIM), lambda i, h, cb=cb: (0, cb // nh + h))
    return pl.pallas_call(
        _gdn_body,
        grid=(b, GDN_HEADS // nh),
        in_specs=[pl.BlockSpec((nh, 8, LANES), lambda i, h: (h, 0, 0)),
                  pl.BlockSpec((8, LANES), lambda i, h: (0, 0)),
                  cw(0), cw(GDN_HEADS), cw(2 * GDN_HEADS),
                  col(CB_QA), col(CB_KA), col(CB_VA), col(CB_ZA),
                  pl.BlockSpec((None, s, LANES), lambda i, h: (i, 0, 0)),
                  pl.BlockSpec((None, nh, 2, nchunk, LANES), lambda i, h: (i, h, 0, 0, 0)),
                  pl.BlockSpec((1, HEAD_DIM), lambda i, h: (0, 0))],
        out_specs=pl.BlockSpec((None, nh, s, HEAD_DIM), lambda i, h: (i, h, 0, 0)),
        out_shape=jax.ShapeDtypeStruct((b, GDN_HEADS, s, HEAD_DIM), BF16),
        scratch_shapes=[
            pltpu.VMEM((s + 16, HEAD_DIM), F32),
            pltpu.VMEM((s, HEAD_DIM), F32),
            pltpu.VMEM((s, HEAD_DIM), F32),
            pltpu.VMEM((s, HEAD_DIM), F32),
            pltpu.VMEM((s, LANES), F32),
            pltpu.VMEM((2, s, LANES), F32),
            pltpu.VMEM((2, s, LANES), F32),
            pltpu.VMEM((nchunk, LANES), F32),
            pltpu.VMEM((nchunk, LANES), F32),
            pltpu.VMEM((nh, 2, nchunk, HEAD_DIM + CHUNK, HEAD_DIM), BF16),
            pltpu.VMEM((nh, 2, nchunk, HEAD_DIM, HEAD_DIM), F32),
            pltpu.VMEM((nh, 2, nchunk, 1, HEAD_DIM), F32),
            pltpu.VMEM((nh, s, HEAD_DIM), F32),
            pltpu.VMEM((nh, s, HEAD_DIM), F32),
        ],
        compiler_params=pltpu.CompilerParams(dimension_semantics=("parallel", "arbitrary"),
                                             vmem_limit_bytes=VMEM_LIMIT),
        name="gdn",
    )(par, lanepar, conv_w, conv_w, conv_w, proj, proj, proj, proj, small, rows, norm_a)


def _dil_body(q0, k0, v0, q1, k1, v1, q2, k2, v2, bias_ref, o_ref,
              qf, kf, vf, qs, kp, vp, acc_s, m_s, l_s):
    s = q0.shape[0]
    scale = HEAD_DIM ** -0.5
    qi = lax.broadcasted_iota(jnp.int32, (QBLK, KWIN), 0)
    kj = lax.broadcasted_iota(jnp.int32, (QBLK, KWIN), 1)
    off = kj - HALF_WINDOW - qi
    band = (off >= -HALF_WINDOW) & (off <= HALF_WINDOW)
    zpad = jnp.zeros((HALF_WINDOW, HEAD_DIM), F32)

    for g, (q_ref, k_ref, v_ref) in enumerate(((q0, k0, v0), (q1, k1, v1), (q2, k2, v2))):
        dil = DIL_GROUPS[g][1]
        length = s // dil
        nblk = length // QBLK
        qf[...] = q_ref[...].astype(F32)
        kf[...] = k_ref[...].astype(F32)
        vf[...] = v_ref[...].astype(F32)
        seg_q = max(length, QBLK)
        seg_k = seg_q + 2 * HALF_WINDOW
        nseg = ATT_GROUP if nblk < ATT_GROUP else 1
        for c in range(nseg):
            for buf in (kp, vp):
                buf[pl.ds(c * seg_k, HALF_WINDOW), :] = zpad
                buf[pl.ds(c * seg_k + HALF_WINDOW + length, HALF_WINDOW), :] = zpad
        bias = pltpu.roll(jnp.broadcast_to(bias_ref[g], (QBLK, BIAS_LEN)), 0, 1,
                          stride=1, stride_axis=0)[:, :KWIN]

        def stage(c, r, dil=dil, length=length, seg_q=seg_q, seg_k=seg_k):
            cls = pl.ds(r, length, stride=dil) if dil > 1 else pl.ds(0, length)
            qs[pl.ds(c * seg_q, length), :] = qf[cls, :]
            kp[pl.ds(c * seg_k + HALF_WINDOW, length), :] = kf[cls, :]
            vp[pl.ds(c * seg_k + HALF_WINDOW, length), :] = vf[cls, :]

        def attend(items, g=g, dil=dil, length=length, bias=bias):
            logits = []
            for q_off, k_off, j0, r in items:
                kpos = j0 - HALF_WINDOW + kj
                ok = band & (kpos >= 0) & (kpos < length)
                sc = _mm_nt(qs[pl.ds(q_off, QBLK), :].astype(BF16), kp[pl.ds(k_off, KWIN), :].astype(BF16))
                logits.append(jnp.where(ok, sc * scale + bias, NEG))
            ms = [jnp.max(lg, axis=-1, keepdims=True) for lg in logits]
            ps = [jnp.exp(lg - m) for lg, m in zip(logits, ms)]
            ls = [jnp.sum(p, axis=-1, keepdims=True) for p in ps]
            accs = [_mm(p.astype(BF16), vp[pl.ds(it[1], KWIN), :].astype(BF16)) for p, it in zip(ps, items)]
            for (q_off, k_off, j0, r), m, lsum, acc in zip(items, ms, ls, accs):
                tok = pl.ds(r + dil * j0, QBLK, stride=dil) if dil > 1 else pl.ds(j0, QBLK)
                acc_s[g, tok, :] = acc
                m_s[g, tok, :] = m
                l_s[g, tok, :] = lsum

        if nseg == 1:
            def residue(r, carry, nblk=nblk, stage=stage, attend=attend):
                stage(0, r)

                def blocks(it, carry2):
                    j0s = [pl.multiple_of((it * ATT_GROUP + i) * QBLK, QBLK) for i in range(ATT_GROUP)]
                    attend([(j0, j0, j0, r) for j0 in j0s])
                    return carry2

                lax.fori_loop(0, nblk // ATT_GROUP, blocks, 0)
                return carry

            lax.fori_loop(0, dil, residue, 0)
        else:
            def classes(it, carry, seg_q=seg_q, seg_k=seg_k, stage=stage, attend=attend):
                rs = [it * ATT_GROUP + c for c in range(ATT_GROUP)]
                for c, r in enumerate(rs):
                    stage(c, r)
                attend([(c * seg_q, c * seg_k, 0, r) for c, r in enumerate(rs)])
                return carry

            lax.fori_loop(0, dil // ATT_GROUP, classes, 0)

    rc = 256
    for c in range(s // rc):
        rows = pl.ds(c * rc, rc)
        ms = [m_s[g, rows, :] for g in range(3)]
        mx = jnp.maximum(jnp.maximum(ms[0], ms[1]), ms[2])
        ws = [jnp.exp(ms[g] - mx) for g in range(3)]
        den = ws[0] * l_s[0, rows, :] + ws[1] * l_s[1, rows, :] + ws[2] * l_s[2, rows, :]
        num = ws[0] * acc_s[0, rows, :] + ws[1] * acc_s[1, rows, :] + ws[2] * acc_s[2, rows, :]
        o_ref[rows, :] = (num / den).astype(o_ref.dtype)


def _dilated(proj, bias):
    b, _, s, _ = proj.shape
    specs = []
    for g in range(3):
        for cb in (CB_QB, CB_KB, CB_VB):
            specs.append(pl.BlockSpec((None, None, s, HEAD_DIM),
                                      lambda i, h, cb=cb, g=g: (i, cb + g * DIL_HEADS_PER_GROUP + h, 0, 0)))
    specs.append(pl.BlockSpec((None, 3, 1, BIAS_LEN), lambda i, h: (h, 0, 0, 0)))
    return pl.pallas_call(
        _dil_body,
        grid=(b, DIL_HEADS_PER_GROUP),
        in_specs=specs,
        out_specs=pl.BlockSpec((None, None, s, HEAD_DIM), lambda i, h: (i, h, 0, 0)),
        out_shape=jax.ShapeDtypeStruct((b, DIL_HEADS_PER_GROUP, s, HEAD_DIM), BF16),
        scratch_shapes=[
            pltpu.VMEM((s, HEAD_DIM), F32),
            pltpu.VMEM((s, HEAD_DIM), F32),
            pltpu.VMEM((s, HEAD_DIM), F32),
            pltpu.VMEM((s, HEAD_DIM), F32),
            pltpu.VMEM((s + 2 * HALF_WINDOW, HEAD_DIM), F32),
            pltpu.VMEM((s + 2 * HALF_WINDOW, HEAD_DIM), F32),
            pltpu.VMEM((3, s, HEAD_DIM), F32),
            pltpu.VMEM((3, s, 1), F32),
            pltpu.VMEM((3, s, 1), F32),
        ],
        compiler_params=pltpu.CompilerParams(dimension_semantics=("parallel", "arbitrary"),
                                             vmem_limit_bytes=VMEM_LIMIT),
        name="dilated_attn",
    )(*([proj] * 9), bias)


def _mix_out_body(x_ref, oa_ref, ob_ref, ga_ref, gb_ref, wa_ref, wb_ref, wo_ref, ln_ref, o_ref):
    wide = lambda ref: jnp.concatenate([ref[c] for c in range(ref.shape[0])], axis=1)
    ya = _mm(wide(oa_ref), wa_ref[...])
    yb = _mm(wide(ob_ref), wb_ref[...])
    merged = _sigmoid(wide(ga_ref).astype(F32)) * ya + _sigmoid(wide(gb_ref).astype(F32)) * yb
    y = _mm(merged.astype(BF16), wo_ref[...])
    o_ref[...] = x_ref[...] + _rms(y, ln_ref[...])


def _mix_out(x3d, o_a, o_b, proj, w_a, w_b, w_o, ln_post, tm):
    b, s, d = x3d.shape
    nslab = d // LANES
    assert CB_GA % nslab == 0 and CB_GB % nslab == 0 and s % tm == 0
    const = lambda shape: pl.BlockSpec(shape, lambda i, j: (0, 0))
    slabs = lambda n, blk: pl.BlockSpec((None, n, tm, LANES), lambda i, j, blk=blk: (i, blk, j, 0))
    return pl.pallas_call(
        _mix_out_body,
        grid=(b, s // tm),
        in_specs=[pl.BlockSpec((None, tm, d), lambda i, j: (i, j, 0)),
                  slabs(GDN_HEADS, 0), slabs(DIL_HEADS_PER_GROUP, 0),
                  slabs(nslab, CB_GA // nslab), slabs(nslab, CB_GB // nslab),
                  const((GDN_WIDTH, d)), const((DIL_OUT_WIDTH, d)), const((d, d)), const((1, d))],
        out_specs=pl.BlockSpec((None, tm, d), lambda i, j: (i, j, 0)),
        out_shape=jax.ShapeDtypeStruct((b, s, d), F32),
        compiler_params=pltpu.CompilerParams(dimension_semantics=("parallel", "parallel"),
                                             vmem_limit_bytes=VMEM_LIMIT),
        name="mix_out",
    )(x3d, o_a, o_b, proj, proj, w_a, w_b, w_o, ln_post)


def _mlp_body(x_ref, g1_ref, w1_ref, w2_ref, g2_ref, o_ref, h_ref, acc_ref):
    k = pl.program_id(1)

    @pl.when(k == 0)
    def _():
        h_ref[...] = _rms(x_ref[...], g1_ref[...]).astype(BF16)

    f = jnp.maximum(_mm(h_ref[...], w1_ref[...]), 0.0)
    part = _mm((f * f).astype(BF16), w2_ref[...])

    @pl.when(k == 0)
    def _():
        acc_ref[...] = part

    @pl.when(k > 0)
    def _():
        acc_ref[...] += part

    @pl.when(k == pl.num_programs(1) - 1)
    def _():
        o_ref[...] = x_ref[...] + _rms(acc_ref[...], g2_ref[...])


def _mlp(x2d, ln_pre, w1, w2, ln_post, tm, tf):
    t, d = x2d.shape
    dff = w1.shape[1]
    return pl.pallas_call(
        _mlp_body,
        grid=(t // tm, dff // tf),
        in_specs=[pl.BlockSpec((tm, d), lambda i, k: (i, 0)),
                  pl.BlockSpec((1, d), lambda i, k: (0, 0)),
                  pl.BlockSpec((d, tf), lambda i, k: (0, k)),
                  pl.BlockSpec((tf, d), lambda i, k: (k, 0)),
                  pl.BlockSpec((1, d), lambda i, k: (0, 0))],
        out_specs=pl.BlockSpec((tm, d), lambda i, k: (i, 0)),
        out_shape=jax.ShapeDtypeStruct((t, d), F32),
        scratch_shapes=[pltpu.VMEM((tm, d), BF16), pltpu.VMEM((tm, d), F32)],
        compiler_params=pltpu.CompilerParams(dimension_semantics=("parallel", "arbitrary"),
                                             vmem_limit_bytes=VMEM_LIMIT),
        name="mlp",
    )(x2d, ln_pre, w1, w2, ln_post)


def _t5_bucket_np(rel):
    nb = REL_BUCKETS // 2
    ret = (rel > 0).astype(np.int32) * nb
    n = np.abs(rel)
    max_exact = nb // 2
    large = max_exact + (np.log(np.maximum(n, 1) / max_exact) / math.log(REL_MAX_DIST / max_exact)
                         * (nb - max_exact)).astype(np.int32)
    large = np.minimum(large, nb - 1)
    return ret + np.where(n < max_exact, n, large).astype(np.int32)


def _attention_bias(rel_bias):
    c = np.arange(BIAS_LEN)
    off = np.where(c < KWIN, np.clip(c - HALF_WINDOW, -HALF_WINDOW, HALF_WINDOW), -HALF_WINDOW)
    per_group = []
    for gi, (_, dil) in enumerate(DIL_GROUPS):
        bt = rel_bias[_t5_bucket_np(off * dil)]
        per_group.append(bt[:, gi * DIL_HEADS_PER_GROUP:(gi + 1) * DIL_HEADS_PER_GROUP])
    return jnp.transpose(jnp.stack(per_group, axis=0), (2, 0, 1))[:, :, None, :].astype(F32)


def _gdn_params(a_log_f, a_log_b, dt_bias_f, dt_bias_b):
    side = lambda f, bk: jnp.concatenate([jnp.broadcast_to(f[:, None], (GDN_HEADS, CHUNK)),
                                          jnp.broadcast_to(bk[:, None], (GDN_HEADS, CHUNK))], axis=1)
    zero = jnp.zeros((GDN_HEADS, LANES), F32)
    par = jnp.stack([side(a_log_f, a_log_b), side(dt_bias_f, dt_bias_b)] + [zero] * 6, axis=1).astype(F32)
    pad = lambda f, bk: jnp.pad(jnp.concatenate([f, bk]), (0, LANES - 2 * GDN_HEADS))
    lanepar = jnp.stack([pad(a_log_f, a_log_b), pad(dt_bias_f, dt_bias_b)]
                        + [jnp.zeros((LANES,), F32)] * 6, axis=0).astype(F32)
    return par, lanepar


def kernel(x, rel_bias, ln_mix_pre, w_in, conv_w, a_log_f, a_log_b, dt_bias_f, dt_bias_b, norm_a,
           w_branch_a, w_branch_b, w_out, ln_mix_post, ln_mlp_pre, w_ff1, w_ff2, ln_mlp_post):
    b, s, d = x.shape
    t = b * s
    nchunk = s // CHUNK
    c_small = 4 * GDN_WIDTH
    bias = _attention_bias(rel_bias)
    for l in range(ln_mix_pre.shape[0]):
        x2d = x.reshape(t, d)
        w = w_in[l]
        c_qb = c_small + N_SMALL
        c_ga = c_qb + 3 * DIL_WIDTH
        w_wide = jnp.concatenate([w[:, :c_small], w[:, c_ga:], w[:, c_qb:c_ga]], axis=1).astype(BF16)
        w_small = jnp.pad(w[:, c_small:c_qb], ((0, 0), (0, LANES - N_SMALL))).astype(BF16)
        gain = ln_mix_pre[l][None, :]
        proj = _norm_matmul_slabs(x, gain, w_wide, 512, "in_proj_wide")
        small = _norm_matmul(x2d, gain, w_small, F32, 1024, LANES, "in_proj_small")

        small3 = small.reshape(b, s, LANES)
        rows = small3[:, :, :N_SMALL].reshape(b, nchunk, CHUNK, 2, 2, GDN_HEADS)
        rows = jnp.transpose(rows, (0, 5, 3, 1, 4, 2)).reshape(b, GDN_HEADS, 2, nchunk, 2 * CHUNK)
        par, lanepar = _gdn_params(a_log_f[l], a_log_b[l], dt_bias_f[l], dt_bias_b[l])
        o_a = _gdn(proj, small3, rows, par, lanepar, conv_w[l].astype(F32), norm_a[l][None, :].astype(F32))
        o_b = _dilated(proj, bias)

        x1 = _mix_out(x, o_a, o_b, proj,
                      w_branch_a[l].astype(BF16), w_branch_b[l].astype(BF16), w_out[l].astype(BF16),
                      ln_mix_post[l][None, :], 512)
        x2d = _mlp(x1.reshape(t, d), ln_mlp_pre[l][None, :], w_ff1[l].astype(BF16), w_ff2[l].astype(BF16),
                   ln_mlp_post[l][None, :], 1024, 1024)
        x = x2d.reshape(b, s, d)
    return x
```

```python
import math

import numpy as np
import jax
import jax.numpy as jnp
from jax import lax
from jax.experimental import pallas as pl
from jax.experimental.pallas import tpu as pltpu

F32 = jnp.float32
BF16 = jnp.bfloat16
HIGHEST = lax.Precision.HIGHEST

D_MODEL = 1024
GDN_HEADS = 8
HEAD_DIM = 128
GDN_WIDTH = GDN_HEADS * HEAD_DIM
CHUNK = 64
TRI_BASE = 16
CHUNK_GROUP = 8
GDN_HEADS_PER_STEP = 2
ATT_GROUP = 8
CONV_K = 5
DIL_GROUPS = ((128, 1), (512, 4), (2048, 16))
DIL_HEADS_PER_GROUP = 4
DIL_HEADS = 12
DIL_WIDTH = DIL_HEADS * HEAD_DIM
DIL_OUT_WIDTH = DIL_HEADS_PER_GROUP * HEAD_DIM
HALF_WINDOW = 64
QBLK = 128
KWIN = QBLK + 2 * HALF_WINDOW
REL_BUCKETS = 32
REL_MAX_DIST = 1024
D_FF = 4 * D_MODEL
EPS = 1e-6
NEG = -1e30
N_SMALL = 4 * GDN_HEADS
LANES = 128
BIAS_LEN = 3 * LANES

CB_QA, CB_KA, CB_VA, CB_ZA = 0, 8, 16, 24
CB_GA, CB_GB = 32, 40
CB_QB, CB_KB, CB_VB = 48, 60, 72
N_WIDE = 84 * LANES

VMEM_LIMIT = 56 * 1024 * 1024


def _mm(a, b):
    return jnp.dot(a, b, preferred_element_type=F32)


def _mm_nt(a, b):
    return lax.dot_general(a, b, (((1,), (1,)), ((), ())), preferred_element_type=F32)


def _mm_hi(a, b):
    return jnp.dot(a, b, preferred_element_type=F32, precision=HIGHEST)


def _rms(x, gain):
    return x * lax.rsqrt(jnp.mean(x * x, axis=-1, keepdims=True) + EPS) * gain


def _sigmoid(x):
    return 1.0 / (1.0 + jnp.exp(-x))


def _softplus(x):
    return jnp.maximum(x, 0.0) + jnp.log(1.0 + jnp.exp(-jnp.abs(x)))


def _norm_matmul_body(x_ref, g_ref, w_ref, o_ref, h_ref):
    @pl.when(pl.program_id(1) == 0)
    def _():
        h_ref[...] = _rms(x_ref[...], g_ref[...]).astype(BF16)

    res = _mm(h_ref[...], w_ref[...])
    if len(o_ref.shape) == 2:
        o_ref[...] = res.astype(o_ref.dtype)
    else:
        for c in range(o_ref.shape[0]):
            o_ref[c] = res[:, c * LANES:(c + 1) * LANES].astype(o_ref.dtype)


def _norm_matmul_slabs(x3d, gain, w, tn, name):
    b, s, d = x3d.shape
    n = w.shape[1]
    return pl.pallas_call(
        _norm_matmul_body,
        grid=(b, n // tn),
        in_specs=[pl.BlockSpec((None, s, d), lambda i, j: (i, 0, 0)),
                  pl.BlockSpec((1, d), lambda i, j: (0, 0)),
                  pl.BlockSpec((d, tn), lambda i, j: (0, j))],
        out_specs=pl.BlockSpec((None, tn // LANES, s, LANES), lambda i, j: (i, j, 0, 0)),
        out_shape=jax.ShapeDtypeStruct((b, n // LANES, s, LANES), BF16),
        scratch_shapes=[pltpu.VMEM((s, d), BF16)],
        compiler_params=pltpu.CompilerParams(dimension_semantics=("parallel", "arbitrary"),
                                             vmem_limit_bytes=VMEM_LIMIT),
        name=name,
    )(x3d, gain, w)


def _norm_matmul(x2d, gain, w, out_dtype, tm, tn, name):
    t, d = x2d.shape
    n = w.shape[1]
    return pl.pallas_call(
        _norm_matmul_body,
        grid=(t // tm, n // tn),
        in_specs=[pl.BlockSpec((tm, d), lambda i, j: (i, 0)),
                  pl.BlockSpec((1, d), lambda i, j: (0, 0)),
                  pl.BlockSpec((d, tn), lambda i, j: (0, j))],
        out_specs=pl.BlockSpec((tm, tn), lambda i, j: (i, j)),
        out_shape=jax.ShapeDtypeStruct((t, n), out_dtype),
        scratch_shapes=[pltpu.VMEM((tm, d), BF16)],
        compiler_params=pltpu.CompilerParams(dimension_semantics=("parallel", "arbitrary"),
                                             vmem_limit_bytes=VMEM_LIMIT),
        name=name,
    )(x2d, gain, w)


def _split_bf16(x):
    hi = x.astype(BF16)
    return hi, (x - hi.astype(F32)).astype(BF16)


def _split3_lanes(x):
    hi = x.astype(BF16)
    r = x - hi.astype(F32)
    mid = r.astype(BF16)
    lo = (r - mid.astype(F32)).astype(BF16)
    return jnp.concatenate([hi, mid, lo], axis=1)


def _block_diag(x, lo):
    return jnp.concatenate([jnp.where(lo, x, 0.0), jnp.where(lo, 0.0, x)], axis=0)


def _mm_split(lhs, rhs):
    lh, ll = _split_bf16(lhs)
    rh, rl = _split_bf16(rhs)
    return (_mm(jnp.concatenate([lh, ll], axis=1), jnp.concatenate([rh, rh], axis=0))
            + _mm(lh, rl))


def _unit_tri_inverse_pairs(a2s, eye2, lo, same_blk):
    ds = [jnp.where(same_blk[0], a2, 0.0) for a2 in a2s]
    xs = [_mm_split(d, _block_diag(d, lo)) for d in ds]
    ts = [eye2 - d for d in ds]
    levels = int(math.log2(TRI_BASE)) - 1
    for lvl in range(levels):
        if lvl < levels - 1:
            outs = [_mm_split(jnp.concatenate([t, x], axis=0), _block_diag(x, lo)) for t, x in zip(ts, xs)]
            ts = [t + out[:CHUNK] for t, out in zip(ts, outs)]
            xs = [out[CHUNK:] for out in outs]
        else:
            ts = [t + _mm_split(t, _block_diag(x, lo)) for t, x in zip(ts, xs)]
    inner = same_blk[0]
    for outer in tuple(same_blk[1:]) + (None,):
        between = [jnp.where(inner, 0.0, a2) if outer is None else jnp.where(outer & ~inner, a2, 0.0)
                   for a2 in a2s]
        tls = [_mm_split(t, _block_diag(l, lo)) for t, l in zip(ts, between)]
        ts = [t - _mm_split(tl, _block_diag(t, lo)) for t, tl in zip(ts, tls)]
        inner = outer
    return ts


def _gdn_body(par_ref, lanepar_ref, cwq_ref, cwk_ref, cwv_ref, q_ref, k_ref, v_ref, z_ref, small_ref,
              rows_ref, na_ref, o_ref,
              xp, qn, kn, vn, act, gcf, bcf, gcrow, brow, mq_s, c_s, dec_s, of_s, ob_s):
    nh, s, _ = q_ref.shape
    nchunk = s // CHUNK
    hp = pl.program_id(1)
    rc = 256

    @pl.when(hp == 0)
    def _():
        lane = lax.broadcasted_iota(jnp.int32, (rc, LANES), 1)
        bi = lax.broadcasted_iota(jnp.int32, (rc, rc), 0)
        bj = lax.broadcasted_iota(jnp.int32, (rc, rc), 1)
        same_chunk = jnp.bitwise_xor(bi, bj) < CHUNK
        cum_f = jnp.where(same_chunk & (bj <= bi), 1.0, 0.0).astype(BF16)
        cum_b = jnp.where(same_chunk & (bj >= bi), 1.0, 0.0).astype(BF16)
        neg_a = -jnp.exp(lanepar_ref[0:1, :])
        dtb = lanepar_ref[1:2, :]
        sum3 = lambda y: y[:, :LANES] + y[:, LANES:2 * LANES] + y[:, 2 * LANES:]
        for c in range(s // rc):
            rows = pl.ds(c * rc, rc)
            slab = small_ref[rows, :]
            g3 = _split3_lanes(neg_a * _softplus(slab + dtb))
            act[rows, :] = jnp.where(lane < GDN_HEADS, sum3(_mm(cum_f, g3)),
                                     jnp.where(lane < 2 * GDN_HEADS, sum3(_mm(cum_b, g3)), _sigmoid(slab)))

    for hh in range(nh):
        lanes = pl.ds(hh * HEAD_DIM, HEAD_DIM)
        _gdn_prepare(hp * nh + hh, par_ref.at[hh], cwq_ref.at[:, lanes], cwk_ref.at[:, lanes], cwv_ref.at[:, lanes],
                     q_ref.at[hh], k_ref.at[hh], v_ref.at[hh], rows_ref.at[hh],
                     xp, qn, kn, vn, act, gcf, bcf, gcrow, brow,
                     mq_s.at[hh], c_s.at[hh], dec_s.at[hh], of_s.at[hh], ob_s.at[hh])

    def phase2(step, states):
        new = []
        for idx, st in enumerate(states):
            hh, d = divmod(idx, 2)
            n = step if d == 0 else nchunk - 1 - step
            r = _mm(mq_s[hh, d, n], st.astype(BF16))
            rows = pl.ds(pl.multiple_of(n * CHUNK, CHUNK), CHUNK)
            out_s = of_s if d == 0 else ob_s
            out_s[hh, rows, :] = out_s[hh, rows, :] + r[HEAD_DIM:]
            new.append(st * dec_s[hh, d, n] + r[:HEAD_DIM] + c_s[hh, d, n])
        return tuple(new)

    zero_state = jnp.zeros((HEAD_DIM, HEAD_DIM), F32)
    lax.fori_loop(0, nchunk, phase2, (zero_state,) * (2 * nh))

    for hh in range(nh):
        for c in range(s // rc):
            rows = pl.ds(c * rc, rc)
            o = of_s[hh, rows, :] + ob_s[hh, rows, :]
            z = z_ref[hh, rows, :].astype(F32)
            o_ref[hh, rows, :] = (_rms(o, na_ref[...]) * (z * _sigmoid(z))).astype(o_ref.dtype)


def _gdn_prepare(h, par_ref, cwq_ref, cwk_ref, cwv_ref, q_ref, k_ref, v_ref, rows_ref,
                 xp, qn, kn, vn, act, gcf, bcf, gcrow, brow, mq_s, c_s, dec_s, of_s, ob_s):
    s = q_ref.shape[0]
    nchunk = s // CHUNK
    rc = 256

    zeros8 = jnp.zeros((8, HEAD_DIM), F32)
    xp[pl.ds(0, 8), :] = zeros8
    xp[pl.ds(8 + s, 8), :] = zeros8

    def conv_into(src_ref, cw_ref, dst_ref, l2, scale):
        xp[pl.ds(8, s), :] = src_ref[...].astype(F32)
        for c in range(s // rc):
            acc = None
            for j in range(CONV_K):
                t = xp[pl.ds(c * rc + 8 - CONV_K // 2 + j, rc), :] * cw_ref[j:j + 1, :]
                acc = t if acc is None else acc + t
            y = acc * _sigmoid(acc)
            if l2:
                y = y * lax.rsqrt(jnp.sum(y * y, axis=-1, keepdims=True) + EPS)
                if scale != 1.0:
                    y = y * scale
            dst_ref[pl.ds(c * rc, rc), :] = y

    conv_into(q_ref, cwq_ref, qn, True, HEAD_DIM ** -0.5)
    conv_into(k_ref, cwk_ref, kn, True, 1.0)
    conv_into(v_ref, cwv_ref, vn, False, 1.0)

    sel_k = lax.broadcasted_iota(jnp.int32, (LANES, 4 * LANES), 0)
    sel_c = lax.broadcasted_iota(jnp.int32, (LANES, 4 * LANES), 1)
    sel = jnp.where(sel_k == (sel_c >> 7) * GDN_HEADS + h, 1.0, 0.0).astype(BF16)
    sel3 = jnp.concatenate([sel, sel, sel], axis=0)
    for c in range(s // rc):
        rows = pl.ds(c * rc, rc)
        picked = _mm(_split3_lanes(act[rows, :]), sel3)
        for d in range(2):
            gcf[d, rows, :] = picked[:, d * LANES:(d + 1) * LANES]
            bcf[d, rows, :] = picked[:, (2 + d) * LANES:(3 + d) * LANES]
    ri = lax.broadcasted_iota(jnp.int32, (LANES, LANES), 0)
    rj = lax.broadcasted_iota(jnp.int32, (LANES, LANES), 1)
    cum_row = jnp.where(((ri < CHUNK) & (rj < CHUNK) & (ri <= rj))
                        | ((ri >= CHUNK) & (rj >= CHUNK) & (ri >= rj)), 1.0, 0.0)
    g_row = -jnp.exp(par_ref[0:1, :]) * _softplus(rows_ref[0] + par_ref[1:2, :])
    gcrow[...] = _mm_hi(g_row, cum_row)
    brow[...] = _sigmoid(rows_ref[1])

    ii = lax.broadcasted_iota(jnp.int32, (CHUNK, 2 * CHUNK), 0)
    lj = lax.broadcasted_iota(jnp.int32, (CHUNK, 2 * CHUNK), 1)
    lo = lj < CHUNK
    hi = lj >= CHUNK
    jj = jnp.where(lo, lj, lj - CHUNK)
    eye2 = jnp.where(ii == jj, 1.0, 0.0)
    tri2 = (lo & (ii >= jj)) | (hi & (ii <= jj))
    strict2 = (lo & (ii > jj)) | (hi & (ii < jj))
    blk_bits = range(int(math.log2(TRI_BASE)), int(math.log2(CHUNK)))
    same_blk = tuple((ii >> sh) == (jj >> sh) for sh in blk_bits)
    lo_t = lax.broadcasted_iota(jnp.int32, (2 * CHUNK, 2 * CHUNK), 1) < CHUNK

    def phase1(grp, carry):
        ns = [grp * CHUNK_GROUP + i for i in range(CHUNK_GROUP)]
        rws = [pl.ds(pl.multiple_of(n * CHUNK, CHUNK), CHUNK) for n in ns]
        qv = [qn[r, :] for r in rws]
        kv = [kn[r, :] for r in rws]
        kb = [k.astype(BF16) for k in kv]
        grams = [_mm_nt(jnp.concatenate([kb_, q.astype(BF16)], axis=0), jnp.concatenate([kb_, kb_], axis=0))
                 for kb_, q in zip(kb, qv)]
        gcc = [(gcf[0, r, :], gcf[1, r, :]) for r in rws]
        bcc = [(bcf[0, r, :], bcf[1, r, :]) for r in rws]
        gcr2 = [gcrow[pl.ds(n, 1), :] for n in ns]
        gams = [jnp.where(tri2, jnp.exp(jnp.where(tri2, jnp.where(lo, gf, gb) - gr, 0.0)), 0.0)
                for (gf, gb), gr in zip(gcc, gcr2)]
        a2s = [jnp.where(strict2, gram[:CHUNK] * jnp.where(lo, bf, bb) * gam, 0.0)
               for gram, (bf, bb), gam in zip(grams, bcc, gams)]
        t2s = _unit_tri_inverse_pairs(a2s, eye2, lo, same_blk)
        uws = []
        for i, r in enumerate(rws):
            v = vn[r, :]
            (gf, gb), (bf, bb) = gcc[i], bcc[i]
            rhs = jnp.concatenate([jnp.concatenate([v * bf, kv[i] * (bf * jnp.exp(gf))], axis=1),
                                   jnp.concatenate([v * bb, kv[i] * (bb * jnp.exp(gb))], axis=1)], axis=0)
            uws.append(_mm(_block_diag(t2s[i], lo).astype(BF16), rhs.astype(BF16)))
        fins = []
        for i in range(CHUNK_GROUP):
            gf, gb = gcc[i]
            gtot_f = gcr2[i][:, CHUNK - 1:CHUNK]
            gtot_b = gcr2[i][:, CHUNK:CHUNK + 1]
            kg = jnp.concatenate([kv[i] * jnp.exp(gtot_f - gf), kv[i] * jnp.exp(gtot_b - gb)], axis=0)
            lhs = jnp.concatenate([_block_diag(kg.T, lo_t), _block_diag(grams[i][CHUNK:] * gams[i], lo)], axis=0)
            fins.append(_mm(lhs.astype(BF16), uws[i].astype(BF16)))
            dec_s[0, ns[i]] = jnp.broadcast_to(jnp.exp(gtot_f), (1, HEAD_DIM))
            dec_s[1, ns[i]] = jnp.broadcast_to(jnp.exp(gtot_b), (1, HEAD_DIM))
        for i, (n, r) in enumerate(zip(ns, rws)):
            fin = fins[i]
            for d in range(2):
                top = fin[d * HEAD_DIM:(d + 1) * HEAD_DIM]
                bot = fin[2 * HEAD_DIM + d * CHUNK:2 * HEAD_DIM + (d + 1) * CHUNK]
                c_s[d, n] = top[:, :HEAD_DIM]
                mq_s[d, n, pl.ds(0, HEAD_DIM), :] = (-top[:, HEAD_DIM:]).astype(BF16)
                mq_s[d, n, pl.ds(HEAD_DIM, CHUNK), :] = (qv[i] * jnp.exp(gcc[i][d]) - bot[:, HEAD_DIM:]).astype(BF16)
                (of_s if d == 0 else ob_s)[r, :] = bot[:, :HEAD_DIM]
        return carry

    lax.fori_loop(0, nchunk // CHUNK_GROUP, phase1, 0)


def _gdn(proj, small, rows, par, lanepar, conv_w, norm_a):
    b, _, s, _ = proj.shape
    nchunk = s // CHUNK
    nh = GDN_HEADS_PER_STEP
    assert 2 * CHUNK == LANES == HEAD_DIM and nchunk % CHUNK_GROUP == 0 and GDN_HEADS % nh == 0
    assert all(cb % nh == 0 for cb in (CB_QA, CB_KA, CB_VA, CB_ZA))
    col = lambda cb: pl.BlockSpec((None, nh, s, HEAD_DIM), lambda i, h, cb=cb: (i, cb // nh + h, 0, 0))
    cw = lambda cb: pl.BlockSpec((CONV_K, nh * HEAD_DIM), lambda i, h, cb=cb: (0, cb // nh + h))
    return pl.pallas_call(
        _gdn_body,
        grid=(b, GDN_HEADS // nh),
        in_specs=[pl.BlockSpec((nh, 8, LANES), lambda i, h: (h, 0, 0)),
                  pl.BlockSpec((8, LANES), lambda i, h: (0, 0)),
                  cw(0), cw(GDN_HEADS), cw(2 * GDN_HEADS),
                  col(CB_QA), col(CB_KA), col(CB_VA), col(CB_ZA),
                  pl.BlockSpec((None, s, LANES), lambda i, h: (i, 0, 0)),
                  pl.BlockSpec((None, nh, 2, nchunk, LANES), lambda i, h: (i, h, 0, 0, 0)),
                  pl.BlockSpec((1, HEAD_DIM), lambda i, h: (0, 0))],
        out_specs=pl.BlockSpec((None, nh, s, HEAD_DIM), lambda i, h: (i, h, 0, 0)),
        out_shape=jax.ShapeDtypeStruct((b, GDN_HEADS, s, HEAD_DIM), BF16),
        scratch_shapes=[
            pltpu.VMEM((s + 16, HEAD_DIM), F32),
            pltpu.VMEM((s, HEAD_DIM), F32),
            pltpu.VMEM((s, HEAD_DIM), F32),
            pltpu.VMEM((s, HEAD_DIM), F32),
            pltpu.VMEM((s, LANES), F32),
            pltpu.VMEM((2, s, LANES), F32),
            pltpu.VMEM((2, s, LANES), F32),
            pltpu.VMEM((nchunk, LANES), F32),
            pltpu.VMEM((nchunk, LANES), F32),
            pltpu.VMEM((nh, 2, nchunk, HEAD_DIM + CHUNK, HEAD_DIM), BF16),
            pltpu.VMEM((nh, 2, nchunk, HEAD_DIM, HEAD_DIM), F32),
            pltpu.VMEM((nh, 2, nchunk, 1, HEAD_DIM), F32),
            pltpu.VMEM((nh, s, HEAD_DIM), F32),
            pltpu.VMEM((nh, s, HEAD_DIM), F32),
        ],
        compiler_params=pltpu.CompilerParams(dimension_semantics=("parallel", "arbitrary"),
                                             vmem_limit_bytes=VMEM_LIMIT),
        name="gdn",
    )(par, lanepar, conv_w, conv_w, conv_w, proj, proj, proj, proj, small, rows, norm_a)


def _dil_body(q0, k0, v0, q1, k1, v1, q2, k2, v2, bias_ref, o_ref,
              qf, kf, vf, qs, kp, vp, acc_s, m_s, l_s):
    s = q0.shape[0]
    scale = HEAD_DIM ** -0.5
    qi = lax.broadcasted_iota(jnp.int32, (QBLK, KWIN), 0)
    kj = lax.broadcasted_iota(jnp.int32, (QBLK, KWIN), 1)
    off = kj - HALF_WINDOW - qi
    band = (off >= -HALF_WINDOW) & (off <= HALF_WINDOW)
    zpad = jnp.zeros((HALF_WINDOW, HEAD_DIM), F32)

    for g, (q_ref, k_ref, v_ref) in enumerate(((q0, k0, v0), (q1, k1, v1), (q2, k2, v2))):
        dil = DIL_GROUPS[g][1]
        length = s // dil
        nblk = length // QBLK
        qf[...] = q_ref[...].astype(F32)
        kf[...] = k_ref[...].astype(F32)
        vf[...] = v_ref[...].astype(F32)
        seg_q = max(length, QBLK)
        seg_k = seg_q + 2 * HALF_WINDOW
        nseg = min(dil, max(1, ATT_GROUP // nblk))
        bpi = min(nblk, ATT_GROUP)
        for c in range(nseg):
            for buf in (kp, vp):
                buf[pl.ds(c * seg_k, HALF_WINDOW), :] = zpad
                buf[pl.ds(c * seg_k + HALF_WINDOW + length, HALF_WINDOW), :] = zpad
        bias = pltpu.roll(jnp.broadcast_to(bias_ref[g], (QBLK, BIAS_LEN)), 0, 1,
                          stride=1, stride_axis=0)[:, :KWIN]

        def stage(c, r, dil=dil, length=length, seg_q=seg_q, seg_k=seg_k):
            cls = pl.ds(r, length, stride=dil) if dil > 1 else pl.ds(0, length)
            qs[pl.ds(c * seg_q, length), :] = qf[cls, :]
            kp[pl.ds(c * seg_k + HALF_WINDOW, length), :] = kf[cls, :]
            vp[pl.ds(c * seg_k + HALF_WINDOW, length), :] = vf[cls, :]

        def attend(items, g=g, dil=dil, length=length, bias=bias):
            logits = []
            for q_off, k_off, j0, r in items:
                kpos = j0 - HALF_WINDOW + kj
                ok = band & (kpos >= 0) & (kpos < length)
                sc = _mm_nt(qs[pl.ds(q_off, QBLK), :].astype(BF16), kp[pl.ds(k_off, KWIN), :].astype(BF16))
                logits.append(jnp.where(ok, sc * scale + bias, NEG))
            ms = [jnp.max(lg, axis=-1, keepdims=True) for lg in logits]
            ps = [jnp.exp(lg - m) for lg, m in zip(logits, ms)]
            ls = [jnp.sum(p, axis=-1, keepdims=True) for p in ps]
            accs = [_mm(p.astype(BF16), vp[pl.ds(it[1], KWIN), :].astype(BF16)) for p, it in zip(ps, items)]
            for (q_off, k_off, j0, r), m, lsum, acc in zip(items, ms, ls, accs):
                tok = pl.ds(r + dil * j0, QBLK, stride=dil) if dil > 1 else pl.ds(j0, QBLK)
                acc_s[g, tok, :] = acc
                m_s[g, tok, :] = m
                l_s[g, tok, :] = lsum

        def classes(it, carry, nblk=nblk, nseg=nseg, bpi=bpi, seg_q=seg_q, seg_k=seg_k,
                    stage=stage, attend=attend):
            rs = [it * nseg + c for c in range(nseg)]
            for c, r in enumerate(rs):
                stage(c, r)

            def blocks(jt, carry2):
                items = []
                for c, r in enumerate(rs):
                    for i in range(bpi):
                        j0 = (jt * bpi + i) * QBLK
                        if not isinstance(j0, int):
                            j0 = pl.multiple_of(j0, QBLK)
                        items.append((c * seg_q + j0, c * seg_k + j0, j0, r))
                attend(items)
                return carry2

            if nblk == bpi:
                blocks(0, 0)
            else:
                lax.fori_loop(0, nblk // bpi, blocks, 0)
            return carry

        lax.fori_loop(0, dil // nseg, classes, 0)

    rc = 256
    for c in range(s // rc):
        rows = pl.ds(c * rc, rc)
        ms = [m_s[g, rows, :] for g in range(3)]
        mx = jnp.maximum(jnp.maximum(ms[0], ms[1]), ms[2])
        ws = [jnp.exp(ms[g] - mx) for g in range(3)]
        den = ws[0] * l_s[0, rows, :] + ws[1] * l_s[1, rows, :] + ws[2] * l_s[2, rows, :]
        num = ws[0] * acc_s[0, rows, :] + ws[1] * acc_s[1, rows, :] + ws[2] * acc_s[2, rows, :]
        o_ref[rows, :] = (num / den).astype(o_ref.dtype)


def _dilated(proj, bias):
    b, _, s, _ = proj.shape
    specs = []
    for g in range(3):
        for cb in (CB_QB, CB_KB, CB_VB):
            specs.append(pl.BlockSpec((None, None, s, HEAD_DIM),
                                      lambda i, h, cb=cb, g=g: (i, cb + g * DIL_HEADS_PER_GROUP + h, 0, 0)))
    specs.append(pl.BlockSpec((None, 3, 1, BIAS_LEN), lambda i, h: (h, 0, 0, 0)))
    return pl.pallas_call(
        _dil_body,
        grid=(b, DIL_HEADS_PER_GROUP),
        in_specs=specs,
        out_specs=pl.BlockSpec((None, None, s, HEAD_DIM), lambda i, h: (i, h, 0, 0)),
        out_shape=jax.ShapeDtypeStruct((b, DIL_HEADS_PER_GROUP, s, HEAD_DIM), BF16),
        scratch_shapes=[
            pltpu.VMEM((s, HEAD_DIM), F32),
            pltpu.VMEM((s, HEAD_DIM), F32),
            pltpu.VMEM((s, HEAD_DIM), F32),
            pltpu.VMEM((s, HEAD_DIM), F32),
            pltpu.VMEM((s + 2 * HALF_WINDOW, HEAD_DIM), F32),
            pltpu.VMEM((s + 2 * HALF_WINDOW, HEAD_DIM), F32),
            pltpu.VMEM((3, s, HEAD_DIM), F32),
            pltpu.VMEM((3, s, 1), F32),
            pltpu.VMEM((3, s, 1), F32),
        ],
        compiler_params=pltpu.CompilerParams(dimension_semantics=("parallel", "arbitrary"),
                                             vmem_limit_bytes=VMEM_LIMIT),
        name="dilated_attn",
    )(*([proj] * 9), bias)


def _mix_out_body(x_ref, oa_ref, ob_ref, ga_ref, gb_ref, wa_ref, wb_ref, wo_ref, ln_ref, o_ref):
    wide = lambda ref: jnp.concatenate([ref[c] for c in range(ref.shape[0])], axis=1)
    ya = _mm(wide(oa_ref), wa_ref[...])
    yb = _mm(wide(ob_ref), wb_ref[...])
    merged = _sigmoid(wide(ga_ref).astype(F32)) * ya + _sigmoid(wide(gb_ref).astype(F32)) * yb
    y = _mm(merged.astype(BF16), wo_ref[...])
    o_ref[...] = x_ref[...] + _rms(y, ln_ref[...])


def _mix_out(x3d, o_a, o_b, proj, w_a, w_b, w_o, ln_post, tm):
    b, s, d = x3d.shape
    nslab = d // LANES
    assert CB_GA % nslab == 0 and CB_GB % nslab == 0 and s % tm == 0
    const = lambda shape: pl.BlockSpec(shape, lambda i, j: (0, 0))
    slabs = lambda n, blk: pl.BlockSpec((None, n, tm, LANES), lambda i, j, blk=blk: (i, blk, j, 0))
    return pl.pallas_call(
        _mix_out_body,
        grid=(b, s // tm),
        in_specs=[pl.BlockSpec((None, tm, d), lambda i, j: (i, j, 0)),
                  slabs(GDN_HEADS, 0), slabs(DIL_HEADS_PER_GROUP, 0),
                  slabs(nslab, CB_GA // nslab), slabs(nslab, CB_GB // nslab),
                  const((GDN_WIDTH, d)), const((DIL_OUT_WIDTH, d)), const((d, d)), const((1, d))],
        out_specs=pl.BlockSpec((None, tm, d), lambda i, j: (i, j, 0)),
        out_shape=jax.ShapeDtypeStruct((b, s, d), F32),
        compiler_params=pltpu.CompilerParams(dimension_semantics=("parallel", "parallel"),
                                             vmem_limit_bytes=VMEM_LIMIT),
        name="mix_out",
    )(x3d, o_a, o_b, proj, proj, w_a, w_b, w_o, ln_post)


def _mlp_body(x_ref, g1_ref, w1_ref, w2_ref, g2_ref, o_ref, h_ref, acc_ref):
    k = pl.program_id(1)

    @pl.when(k == 0)
    def _():
        h_ref[...] = _rms(x_ref[...], g1_ref[...]).astype(BF16)

    f = jnp.maximum(_mm(h_ref[...], w1_ref[...]), 0.0)
    part = _mm((f * f).astype(BF16), w2_ref[...])

    @pl.when(k == 0)
    def _():
        acc_ref[...] = part

    @pl.when(k > 0)
    def _():
        acc_ref[...] += part

    @pl.when(k == pl.num_programs(1) - 1)
    def _():
        o_ref[...] = x_ref[...] + _rms(acc_ref[...], g2_ref[...])


def _mlp(x2d, ln_pre, w1, w2, ln_post, tm, tf):
    t, d = x2d.shape
    dff = w1.shape[1]
    return pl.pallas_call(
        _mlp_body,
        grid=(t // tm, dff // tf),
        in_specs=[pl.BlockSpec((tm, d), lambda i, k: (i, 0)),
                  pl.BlockSpec((1, d), lambda i, k: (0, 0)),
                  pl.BlockSpec((d, tf), lambda i, k: (0, k)),
                  pl.BlockSpec((tf, d), lambda i, k: (k, 0)),
                  pl.BlockSpec((1, d), lambda i, k: (0, 0))],
        out_specs=pl.BlockSpec((tm, d), lambda i, k: (i, 0)),
        out_shape=jax.ShapeDtypeStruct((t, d), F32),
        scratch_shapes=[pltpu.VMEM((tm, d), BF16), pltpu.VMEM((tm, d), F32)],
        compiler_params=pltpu.CompilerParams(dimension_semantics=("parallel", "arbitrary"),
                                             vmem_limit_bytes=VMEM_LIMIT),
        name="mlp",
    )(x2d, ln_pre, w1, w2, ln_post)


def _t5_bucket_np(rel):
    nb = REL_BUCKETS // 2
    ret = (rel > 0).astype(np.int32) * nb
    n = np.abs(rel)
    max_exact = nb // 2
    large = max_exact + (np.log(np.maximum(n, 1) / max_exact) / math.log(REL_MAX_DIST / max_exact)
                         * (nb - max_exact)).astype(np.int32)
    large = np.minimum(large, nb - 1)
    return ret + np.where(n < max_exact, n, large).astype(np.int32)


def _attention_bias(rel_bias):
    c = np.arange(BIAS_LEN)
    off = np.where(c < KWIN, np.clip(c - HALF_WINDOW, -HALF_WINDOW, HALF_WINDOW), -HALF_WINDOW)
    per_group = []
    for gi, (_, dil) in enumerate(DIL_GROUPS):
        bt = rel_bias[_t5_bucket_np(off * dil)]
        per_group.append(bt[:, gi * DIL_HEADS_PER_GROUP:(gi + 1) * DIL_HEADS_PER_GROUP])
    return jnp.transpose(jnp.stack(per_group, axis=0), (2, 0, 1))[:, :, None, :].astype(F32)


def _gdn_params(a_log_f, a_log_b, dt_bias_f, dt_bias_b):
    side = lambda f, bk: jnp.concatenate([jnp.broadcast_to(f[:, None], (GDN_HEADS, CHUNK)),
                                          jnp.broadcast_to(bk[:, None], (GDN_HEADS, CHUNK))], axis=1)
    zero = jnp.zeros((GDN_HEADS, LANES), F32)
    par = jnp.stack([side(a_log_f, a_log_b), side(dt_bias_f, dt_bias_b)] + [zero] * 6, axis=1).astype(F32)
    pad = lambda f, bk: jnp.pad(jnp.concatenate([f, bk]), (0, LANES - 2 * GDN_HEADS))
    lanepar = jnp.stack([pad(a_log_f, a_log_b), pad(dt_bias_f, dt_bias_b)]
                        + [jnp.zeros((LANES,), F32)] * 6, axis=0).astype(F32)
    return par, lanepar


def kernel(x, rel_bias, ln_mix_pre, w_in, conv_w, a_log_f, a_log_b, dt_bias_f, dt_bias_b, norm_a,
           w_branch_a, w_branch_b, w_out, ln_mix_post, ln_mlp_pre, w_ff1, w_ff2, ln_mlp_post):
    b, s, d = x.shape
    t = b * s
    nchunk = s // CHUNK
    c_small = 4 * GDN_WIDTH
    bias = _attention_bias(rel_bias)
    for l in range(ln_mix_pre.shape[0]):
        x2d = x.reshape(t, d)
        w = w_in[l]
        c_qb = c_small + N_SMALL
        c_ga = c_qb + 3 * DIL_WIDTH
        w_wide = jnp.concatenate([w[:, :c_small], w[:, c_ga:], w[:, c_qb:c_ga]], axis=1).astype(BF16)
        w_small = jnp.pad(w[:, c_small:c_qb], ((0, 0), (0, LANES - N_SMALL))).astype(BF16)
        gain = ln_mix_pre[l][None, :]
        proj = _norm_matmul_slabs(x, gain, w_wide, 768, "in_proj_wide")
        small = _norm_matmul(x2d, gain, w_small, F32, 1024, LANES, "in_proj_small")

        small3 = small.reshape(b, s, LANES)
        rows = small3[:, :, :N_SMALL].reshape(b, nchunk, CHUNK, 2, 2, GDN_HEADS)
        rows = jnp.transpose(rows, (0, 5, 3, 1, 4, 2)).reshape(b, GDN_HEADS, 2, nchunk, 2 * CHUNK)
        par, lanepar = _gdn_params(a_log_f[l], a_log_b[l], dt_bias_f[l], dt_bias_b[l])
        o_a = _gdn(proj, small3, rows, par, lanepar, conv_w[l].astype(F32), norm_a[l][None, :].astype(F32))
        o_b = _dilated(proj, bias)

        x1 = _mix_out(x, o_a, o_b, proj,
                      w_branch_a[l].astype(BF16), w_branch_b[l].astype(BF16), w_out[l].astype(BF16),
                      ln_mix_post[l][None, :], 512)
        x2d = _mlp(x1.reshape(t, d), ln_mlp_pre[l][None, :], w_ff1[l].astype(BF16), w_ff2[l].astype(BF16),
                   ln_mlp_post[l][None, :], 1024, 1024)
        x = x2d.reshape(b, s, d)
    return x
```

```python
import math

import numpy as np
import jax
import jax.numpy as jnp
from jax import lax
from jax.experimental import pallas as pl
from jax.experimental.pallas import tpu as pltpu

F32 = jnp.float32
BF16 = jnp.bfloat16
HIGHEST = lax.Precision.HIGHEST

D_MODEL = 1024
GDN_HEADS = 8
HEAD_DIM = 128
GDN_WIDTH = GDN_HEADS * HEAD_DIM
CHUNK = 64
TRI_BASE = 16
CHUNK_GROUP = 8
GDN_HEADS_PER_STEP = 2
ATT_GROUP = 8
CONV_K = 5
DIL_GROUPS = ((128, 1), (512, 4), (2048, 16))
DIL_HEADS_PER_GROUP = 4
DIL_HEADS = 12
DIL_WIDTH = DIL_HEADS * HEAD_DIM
DIL_OUT_WIDTH = DIL_HEADS_PER_GROUP * HEAD_DIM
HALF_WINDOW = 64
QBLK = 128
KWIN = QBLK + 2 * HALF_WINDOW
REL_BUCKETS = 32
REL_MAX_DIST = 1024
D_FF = 4 * D_MODEL
EPS = 1e-6
NEG = -1e30
N_SMALL = 4 * GDN_HEADS
LANES = 128
BIAS_LEN = 3 * LANES

CB_QA, CB_KA, CB_VA, CB_ZA = 0, 8, 16, 24
CB_GA, CB_GB = 32, 40
CB_QB, CB_KB, CB_VB = 48, 60, 72
N_WIDE = 84 * LANES

VMEM_LIMIT = 56 * 1024 * 1024


def _mm(a, b):
    return jnp.dot(a, b, preferred_element_type=F32)


def _mm_nt(a, b):
    return lax.dot_general(a, b, (((1,), (1,)), ((), ())), preferred_element_type=F32)


def _mm_hi(a, b):
    return jnp.dot(a, b, preferred_element_type=F32, precision=HIGHEST)


def _rms(x, gain):
    return x * lax.rsqrt(jnp.mean(x * x, axis=-1, keepdims=True) + EPS) * gain


def _sigmoid(x):
    return 1.0 / (1.0 + jnp.exp(-x))


def _softplus(x):
    return jnp.maximum(x, 0.0) + jnp.log(1.0 + jnp.exp(-jnp.abs(x)))


def _in_proj_body(x_ref, g_ref, w_ref, ws_ref, o_ref, small_ref, h_ref):
    @pl.when(pl.program_id(1) == 0)
    def _():
        h_ref[...] = _rms(x_ref[...], g_ref[...]).astype(BF16)
        small_ref[...] = _mm(h_ref[...], ws_ref[...])

    res = _mm(h_ref[...], w_ref[...])
    for c in range(o_ref.shape[0]):
        o_ref[c] = res[:, c * LANES:(c + 1) * LANES].astype(o_ref.dtype)


def _in_proj(x3d, gain, w, w_small, tn):
    b, s, d = x3d.shape
    n = w.shape[1]
    return pl.pallas_call(
        _in_proj_body,
        grid=(b, n // tn),
        in_specs=[pl.BlockSpec((None, s, d), lambda i, j: (i, 0, 0)),
                  pl.BlockSpec((1, d), lambda i, j: (0, 0)),
                  pl.BlockSpec((d, tn), lambda i, j: (0, j)),
                  pl.BlockSpec((d, LANES), lambda i, j: (0, 0))],
        out_specs=[pl.BlockSpec((None, tn // LANES, s, LANES), lambda i, j: (i, j, 0, 0)),
                   pl.BlockSpec((None, s, LANES), lambda i, j: (i, 0, 0))],
        out_shape=[jax.ShapeDtypeStruct((b, n // LANES, s, LANES), BF16),
                   jax.ShapeDtypeStruct((b, s, LANES), F32)],
        scratch_shapes=[pltpu.VMEM((s, d), BF16)],
        compiler_params=pltpu.CompilerParams(dimension_semantics=("parallel", "arbitrary"),
                                             vmem_limit_bytes=VMEM_LIMIT),
        name="in_proj",
    )(x3d, gain, w, w_small)


def _split_bf16(x):
    hi = x.astype(BF16)
    return hi, (x - hi.astype(F32)).astype(BF16)


def _split3_lanes(x):
    hi = x.astype(BF16)
    r = x - hi.astype(F32)
    mid = r.astype(BF16)
    lo = (r - mid.astype(F32)).astype(BF16)
    return jnp.concatenate([hi, mid, lo], axis=1)


def _block_diag(x, lo):
    return jnp.concatenate([jnp.where(lo, x, 0.0), jnp.where(lo, 0.0, x)], axis=0)


def _mm_split(lhs, rhs):
    lh, ll = _split_bf16(lhs)
    rh, rl = _split_bf16(rhs)
    return (_mm(jnp.concatenate([lh, ll], axis=1), jnp.concatenate([rh, rh], axis=0))
            + _mm(lh, rl))


def _unit_tri_inverse_pairs(a2s, eye2, lo, same_blk):
    ds = [jnp.where(same_blk[0], a2, 0.0) for a2 in a2s]
    xs = [_mm_split(d, _block_diag(d, lo)) for d in ds]
    ts = [eye2 - d for d in ds]
    levels = int(math.log2(TRI_BASE)) - 1
    for lvl in range(levels):
        if lvl < levels - 1:
            outs = [_mm_split(jnp.concatenate([t, x], axis=0), _block_diag(x, lo)) for t, x in zip(ts, xs)]
            ts = [t + out[:CHUNK] for t, out in zip(ts, outs)]
            xs = [out[CHUNK:] for out in outs]
        else:
            ts = [t + _mm_split(t, _block_diag(x, lo)) for t, x in zip(ts, xs)]
    inner = same_blk[0]
    for outer in tuple(same_blk[1:]) + (None,):
        between = [jnp.where(inner, 0.0, a2) if outer is None else jnp.where(outer & ~inner, a2, 0.0)
                   for a2 in a2s]
        tls = [_mm_split(t, _block_diag(l, lo)) for t, l in zip(ts, between)]
        ts = [t - _mm_split(tl, _block_diag(t, lo)) for t, tl in zip(ts, tls)]
        inner = outer
    return ts


def _gdn_body(lanepar_ref, cwq_ref, cwk_ref, cwv_ref, q_ref, k_ref, v_ref, z_ref, small_ref,
              na_ref, o_ref,
              xp, qn, kn, vn, act, gcf, bcf, mq_s, c_s, dec_s, of_s, ob_s):
    nh, s, _ = q_ref.shape
    nchunk = s // CHUNK
    hp = pl.program_id(1)
    rc = 256

    @pl.when(hp == 0)
    def _():
        lane = lax.broadcasted_iota(jnp.int32, (rc, LANES), 1)
        bi = lax.broadcasted_iota(jnp.int32, (rc, rc), 0)
        bj = lax.broadcasted_iota(jnp.int32, (rc, rc), 1)
        same_chunk = jnp.bitwise_xor(bi, bj) < CHUNK
        cum_f = jnp.where(same_chunk & (bj <= bi), 1.0, 0.0).astype(BF16)
        cum_b = jnp.where(same_chunk & (bj >= bi), 1.0, 0.0).astype(BF16)
        neg_a = -jnp.exp(lanepar_ref[0:1, :])
        dtb = lanepar_ref[1:2, :]
        sum3 = lambda y: y[:, :LANES] + y[:, LANES:2 * LANES] + y[:, 2 * LANES:]
        for c in range(s // rc):
            rows = pl.ds(c * rc, rc)
            slab = small_ref[rows, :]
            g3 = _split3_lanes(neg_a * _softplus(slab + dtb))
            act[rows, :] = jnp.where(lane < GDN_HEADS, sum3(_mm(cum_f, g3)),
                                     jnp.where(lane < 2 * GDN_HEADS, sum3(_mm(cum_b, g3)), _sigmoid(slab)))

    for hh in range(nh):
        lanes = pl.ds(hh * HEAD_DIM, HEAD_DIM)
        _gdn_prepare(hp * nh + hh, cwq_ref.at[:, lanes], cwk_ref.at[:, lanes], cwv_ref.at[:, lanes],
                     q_ref.at[hh], k_ref.at[hh], v_ref.at[hh],
                     xp, qn, kn, vn, act, gcf, bcf,
                     mq_s.at[hh], c_s.at[hh], dec_s.at[hh], of_s.at[hh], ob_s.at[hh])

    def phase2(step, states):
        new = []
        for idx, st in enumerate(states):
            hh, d = divmod(idx, 2)
            n = step if d == 0 else nchunk - 1 - step
            r = _mm(mq_s[hh, d, n], st.astype(BF16))
            rows = pl.ds(pl.multiple_of(n * CHUNK, CHUNK), CHUNK)
            out_s = of_s if d == 0 else ob_s
            out_s[hh, rows, :] = out_s[hh, rows, :] + r[HEAD_DIM:]
            new.append(st * dec_s[hh, d, n] + r[:HEAD_DIM] + c_s[hh, d, n])
        return tuple(new)

    zero_state = jnp.zeros((HEAD_DIM, HEAD_DIM), F32)
    lax.fori_loop(0, nchunk, phase2, (zero_state,) * (2 * nh))

    for hh in range(nh):
        for c in range(s // rc):
            rows = pl.ds(c * rc, rc)
            o = of_s[hh, rows, :] + ob_s[hh, rows, :]
            z = z_ref[hh, rows, :].astype(F32)
            o_ref[hh, rows, :] = (_rms(o, na_ref[...]) * (z * _sigmoid(z))).astype(o_ref.dtype)


def _gdn_prepare(h, cwq_ref, cwk_ref, cwv_ref, q_ref, k_ref, v_ref,
                 xp, qn, kn, vn, act, gcf, bcf, mq_s, c_s, dec_s, of_s, ob_s):
    s = q_ref.shape[0]
    nchunk = s // CHUNK
    rc = 256

    zeros8 = jnp.zeros((8, HEAD_DIM), F32)
    xp[pl.ds(0, 8), :] = zeros8
    xp[pl.ds(8 + s, 8), :] = zeros8

    def conv_into(src_ref, cw_ref, dst_ref, l2, scale):
        xp[pl.ds(8, s), :] = src_ref[...].astype(F32)
        for c in range(s // rc):
            acc = None
            for j in range(CONV_K):
                t = xp[pl.ds(c * rc + 8 - CONV_K // 2 + j, rc), :] * cw_ref[j:j + 1, :]
                acc = t if acc is None else acc + t
            y = acc * _sigmoid(acc)
            if l2:
                y = y * lax.rsqrt(jnp.sum(y * y, axis=-1, keepdims=True) + EPS)
                if scale != 1.0:
                    y = y * scale
            dst_ref[pl.ds(c * rc, rc), :] = y

    conv_into(q_ref, cwq_ref, qn, True, HEAD_DIM ** -0.5)
    conv_into(k_ref, cwk_ref, kn, True, 1.0)
    conv_into(v_ref, cwv_ref, vn, False, 1.0)

    sel_k = lax.broadcasted_iota(jnp.int32, (LANES, 4 * LANES), 0)
    sel_c = lax.broadcasted_iota(jnp.int32, (LANES, 4 * LANES), 1)
    sel = jnp.where(sel_k == (sel_c >> 7) * GDN_HEADS + h, 1.0, 0.0).astype(BF16)
    sel3 = jnp.concatenate([sel, sel, sel], axis=0)
    for c in range(s // rc):
        rows = pl.ds(c * rc, rc)
        picked = _mm(_split3_lanes(act[rows, :]), sel3)
        for d in range(2):
            gcf[d, rows, :] = picked[:, d * LANES:(d + 1) * LANES]
            bcf[d, rows, :] = picked[:, (2 + d) * LANES:(3 + d) * LANES]
    ii = lax.broadcasted_iota(jnp.int32, (CHUNK, 2 * CHUNK), 0)
    lj = lax.broadcasted_iota(jnp.int32, (CHUNK, 2 * CHUNK), 1)
    lo = lj < CHUNK
    hi = lj >= CHUNK
    jj = jnp.where(lo, lj, lj - CHUNK)
    is_diag = ii == jj
    eye2 = jnp.where(is_diag, 1.0, 0.0)
    tri2 = (lo & (ii >= jj)) | (hi & (ii <= jj))
    strict2 = (lo & (ii > jj)) | (hi & (ii < jj))
    blk_bits = range(int(math.log2(TRI_BASE)), int(math.log2(CHUNK)))
    same_blk = tuple((ii >> sh) == (jj >> sh) for sh in blk_bits)
    lo_t = lax.broadcasted_iota(jnp.int32, (2 * CHUNK, 2 * CHUNK), 1) < CHUNK

    def phase1(grp, carry):
        ns = [grp * CHUNK_GROUP + i for i in range(CHUNK_GROUP)]
        rws = [pl.ds(pl.multiple_of(n * CHUNK, CHUNK), CHUNK) for n in ns]
        qv = [qn[r, :] for r in rws]
        kv = [kn[r, :] for r in rws]
        kb = [k.astype(BF16) for k in kv]
        grams = [_mm_nt(jnp.concatenate([kb_, q.astype(BF16)], axis=0), jnp.concatenate([kb_, kb_], axis=0))
                 for kb_, q in zip(kb, qv)]
        gcc = [(gcf[0, r, :], gcf[1, r, :]) for r in rws]
        bcc = [(bcf[0, r, :], bcf[1, r, :]) for r in rws]
        gcc2 = [jnp.where(lo, gf, gb) for gf, gb in gcc]
        gcr2 = [jnp.sum(jnp.where(is_diag, g2, 0.0), axis=0, keepdims=True) for g2 in gcc2]
        gams = [jnp.where(tri2, jnp.exp(jnp.where(tri2, g2 - gr, 0.0)), 0.0) for g2, gr in zip(gcc2, gcr2)]
        a2s = [jnp.where(strict2, gram[:CHUNK] * jnp.where(lo, bf, bb) * gam, 0.0)
               for gram, (bf, bb), gam in zip(grams, bcc, gams)]
        t2s = _unit_tri_inverse_pairs(a2s, eye2, lo, same_blk)
        uws = []
        for i, r in enumerate(rws):
            v = vn[r, :]
            (gf, gb), (bf, bb) = gcc[i], bcc[i]
            rhs = jnp.concatenate([jnp.concatenate([v * bf, kv[i] * (bf * jnp.exp(gf))], axis=1),
                                   jnp.concatenate([v * bb, kv[i] * (bb * jnp.exp(gb))], axis=1)], axis=0)
            uws.append(_mm(_block_diag(t2s[i], lo).astype(BF16), rhs.astype(BF16)))
        fins = []
        for i in range(CHUNK_GROUP):
            gf, gb = gcc[i]
            gtot_f = gcr2[i][:, CHUNK - 1:CHUNK]
            gtot_b = gcr2[i][:, CHUNK:CHUNK + 1]
            kg = jnp.concatenate([kv[i] * jnp.exp(gtot_f - gf), kv[i] * jnp.exp(gtot_b - gb)], axis=0)
            lhs = jnp.concatenate([_block_diag(kg.T, lo_t), _block_diag(grams[i][CHUNK:] * gams[i], lo)], axis=0)
            fins.append(_mm(lhs.astype(BF16), uws[i].astype(BF16)))
            dec_s[0, ns[i]] = jnp.broadcast_to(jnp.exp(gtot_f), (1, HEAD_DIM))
            dec_s[1, ns[i]] = jnp.broadcast_to(jnp.exp(gtot_b), (1, HEAD_DIM))
        for i, (n, r) in enumerate(zip(ns, rws)):
            fin = fins[i]
            for d in range(2):
                top = fin[d * HEAD_DIM:(d + 1) * HEAD_DIM]
                bot = fin[2 * HEAD_DIM + d * CHUNK:2 * HEAD_DIM + (d + 1) * CHUNK]
                c_s[d, n] = top[:, :HEAD_DIM]
                mq_s[d, n, pl.ds(0, HEAD_DIM), :] = (-top[:, HEAD_DIM:]).astype(BF16)
                mq_s[d, n, pl.ds(HEAD_DIM, CHUNK), :] = (qv[i] * jnp.exp(gcc[i][d]) - bot[:, HEAD_DIM:]).astype(BF16)
                (of_s if d == 0 else ob_s)[r, :] = bot[:, :HEAD_DIM]
        return carry

    lax.fori_loop(0, nchunk // CHUNK_GROUP, phase1, 0)


def _gdn(proj, small, lanepar, conv_w, norm_a):
    b, _, s, _ = proj.shape
    nchunk = s // CHUNK
    nh = GDN_HEADS_PER_STEP
    assert 2 * CHUNK == LANES == HEAD_DIM and nchunk % CHUNK_GROUP == 0 and GDN_HEADS % nh == 0
    assert all(cb % nh == 0 for cb in (CB_QA, CB_KA, CB_VA, CB_ZA))
    col = lambda cb: pl.BlockSpec((None, nh, s, HEAD_DIM), lambda i, h, cb=cb: (i, cb // nh + h, 0, 0))
    cw = lambda cb: pl.BlockSpec((CONV_K, nh * HEAD_DIM), lambda i, h, cb=cb: (0, cb // nh + h))
    return pl.pallas_call(
        _gdn_body,
        grid=(b, GDN_HEADS // nh),
        in_specs=[pl.BlockSpec((8, LANES), lambda i, h: (0, 0)),
                  cw(0), cw(GDN_HEADS), cw(2 * GDN_HEADS),
                  col(CB_QA), col(CB_KA), col(CB_VA), col(CB_ZA),
                  pl.BlockSpec((None, s, LANES), lambda i, h: (i, 0, 0)),
                  pl.BlockSpec((1, HEAD_DIM), lambda i, h: (0, 0))],
        out_specs=pl.BlockSpec((None, nh, s, HEAD_DIM), lambda i, h: (i, h, 0, 0)),
        out_shape=jax.ShapeDtypeStruct((b, GDN_HEADS, s, HEAD_DIM), BF16),
        scratch_shapes=[
            pltpu.VMEM((s + 16, HEAD_DIM), F32),
            pltpu.VMEM((s, HEAD_DIM), F32),
            pltpu.VMEM((s, HEAD_DIM), F32),
            pltpu.VMEM((s, HEAD_DIM), F32),
            pltpu.VMEM((s, LANES), F32),
            pltpu.VMEM((2, s, LANES), F32),
            pltpu.VMEM((2, s, LANES), F32),
            pltpu.VMEM((nh, 2, nchunk, HEAD_DIM + CHUNK, HEAD_DIM), BF16),
            pltpu.VMEM((nh, 2, nchunk, HEAD_DIM, HEAD_DIM), F32),
            pltpu.VMEM((nh, 2, nchunk, 1, HEAD_DIM), F32),
            pltpu.VMEM((nh, s, HEAD_DIM), F32),
            pltpu.VMEM((nh, s, HEAD_DIM), F32),
        ],
        compiler_params=pltpu.CompilerParams(dimension_semantics=("parallel", "arbitrary"),
                                             vmem_limit_bytes=VMEM_LIMIT),
        name="gdn",
    )(lanepar, conv_w, conv_w, conv_w, proj, proj, proj, proj, small, norm_a)


def _dil_body(q0, k0, v0, q1, k1, v1, q2, k2, v2, bias_ref, o_ref,
              qf, kf, vf, qs, kp, vp, acc_s, m_s, l_s):
    s = q0.shape[0]
    scale = HEAD_DIM ** -0.5
    qi = lax.broadcasted_iota(jnp.int32, (QBLK, KWIN), 0)
    kj = lax.broadcasted_iota(jnp.int32, (QBLK, KWIN), 1)
    off = kj - HALF_WINDOW - qi
    band = (off >= -HALF_WINDOW) & (off <= HALF_WINDOW)
    zpad = jnp.zeros((HALF_WINDOW, HEAD_DIM), F32)

    for g, (q_ref, k_ref, v_ref) in enumerate(((q0, k0, v0), (q1, k1, v1), (q2, k2, v2))):
        dil = DIL_GROUPS[g][1]
        length = s // dil
        nblk = length // QBLK
        qf[...] = q_ref[...].astype(F32)
        kf[...] = k_ref[...].astype(F32)
        vf[...] = v_ref[...].astype(F32)
        seg_q = max(length, QBLK)
        seg_k = seg_q + 2 * HALF_WINDOW
        nseg = min(dil, max(1, ATT_GROUP // nblk))
        bpi = min(nblk, ATT_GROUP)
        for c in range(nseg):
            for buf in (kp, vp):
                buf[pl.ds(c * seg_k, HALF_WINDOW), :] = zpad
                buf[pl.ds(c * seg_k + HALF_WINDOW + length, HALF_WINDOW), :] = zpad
        bias = pltpu.roll(jnp.broadcast_to(bias_ref[g], (QBLK, BIAS_LEN)), 0, 1,
                          stride=1, stride_axis=0)[:, :KWIN]

        def stage(c, r, dil=dil, length=length, seg_q=seg_q, seg_k=seg_k):
            cls = pl.ds(r, length, stride=dil) if dil > 1 else pl.ds(0, length)
            qs[pl.ds(c * seg_q, length), :] = qf[cls, :]
            kp[pl.ds(c * seg_k + HALF_WINDOW, length), :] = kf[cls, :]
            vp[pl.ds(c * seg_k + HALF_WINDOW, length), :] = vf[cls, :]

        def attend(items, g=g, dil=dil, length=length, bias=bias):
            logits = []
            for q_off, k_off, j0, r in items:
                kpos = j0 - HALF_WINDOW + kj
                ok = band & (kpos >= 0) & (kpos < length)
                sc = _mm_nt(qs[pl.ds(q_off, QBLK), :].astype(BF16), kp[pl.ds(k_off, KWIN), :].astype(BF16))
                logits.append(jnp.where(ok, sc * scale + bias, NEG))
            ms = [jnp.max(lg, axis=-1, keepdims=True) for lg in logits]
            ps = [jnp.exp(lg - m) for lg, m in zip(logits, ms)]
            ls = [jnp.sum(p, axis=-1, keepdims=True) for p in ps]
            accs = [_mm(p.astype(BF16), vp[pl.ds(it[1], KWIN), :].astype(BF16)) for p, it in zip(ps, items)]
            for (q_off, k_off, j0, r), m, lsum, acc in zip(items, ms, ls, accs):
                tok = pl.ds(r + dil * j0, QBLK, stride=dil) if dil > 1 else pl.ds(j0, QBLK)
                acc_s[g, tok, :] = acc
                m_s[g, tok, :] = m
                l_s[g, tok, :] = lsum

        def classes(it, carry, nblk=nblk, nseg=nseg, bpi=bpi, seg_q=seg_q, seg_k=seg_k,
                    stage=stage, attend=attend):
            rs = [it * nseg + c for c in range(nseg)]
            for c, r in enumerate(rs):
                stage(c, r)

            def blocks(jt, carry2):
                items = []
                for c, r in enumerate(rs):
                    for i in range(bpi):
                        j0 = (jt * bpi + i) * QBLK
                        if not isinstance(j0, int):
                            j0 = pl.multiple_of(j0, QBLK)
                        items.append((c * seg_q + j0, c * seg_k + j0, j0, r))
                attend(items)
                return carry2

            if nblk == bpi:
                blocks(0, 0)
            else:
                lax.fori_loop(0, nblk // bpi, blocks, 0)
            return carry

        lax.fori_loop(0, dil // nseg, classes, 0)

    rc = 256
    for c in range(s // rc):
        rows = pl.ds(c * rc, rc)
        ms = [m_s[g, rows, :] for g in range(3)]
        mx = jnp.maximum(jnp.maximum(ms[0], ms[1]), ms[2])
        ws = [jnp.exp(ms[g] - mx) for g in range(3)]
        den = ws[0] * l_s[0, rows, :] + ws[1] * l_s[1, rows, :] + ws[2] * l_s[2, rows, :]
        num = ws[0] * acc_s[0, rows, :] + ws[1] * acc_s[1, rows, :] + ws[2] * acc_s[2, rows, :]
        o_ref[rows, :] = (num / den).astype(o_ref.dtype)


def _dilated(proj, bias):
    b, _, s, _ = proj.shape
    specs = []
    for g in range(3):
        for cb in (CB_QB, CB_KB, CB_VB):
            specs.append(pl.BlockSpec((None, None, s, HEAD_DIM),
                                      lambda i, h, cb=cb, g=g: (i, cb + g * DIL_HEADS_PER_GROUP + h, 0, 0)))
    specs.append(pl.BlockSpec((None, 3, 1, BIAS_LEN), lambda i, h: (h, 0, 0, 0)))
    return pl.pallas_call(
        _dil_body,
        grid=(b, DIL_HEADS_PER_GROUP),
        in_specs=specs,
        out_specs=pl.BlockSpec((None, None, s, HEAD_DIM), lambda i, h: (i, h, 0, 0)),
        out_shape=jax.ShapeDtypeStruct((b, DIL_HEADS_PER_GROUP, s, HEAD_DIM), BF16),
        scratch_shapes=[
            pltpu.VMEM((s, HEAD_DIM), F32),
            pltpu.VMEM((s, HEAD_DIM), F32),
            pltpu.VMEM((s, HEAD_DIM), F32),
            pltpu.VMEM((s, HEAD_DIM), F32),
            pltpu.VMEM((s + 2 * HALF_WINDOW, HEAD_DIM), F32),
            pltpu.VMEM((s + 2 * HALF_WINDOW, HEAD_DIM), F32),
            pltpu.VMEM((3, s, HEAD_DIM), F32),
            pltpu.VMEM((3, s, 1), F32),
            pltpu.VMEM((3, s, 1), F32),
        ],
        compiler_params=pltpu.CompilerParams(dimension_semantics=("parallel", "arbitrary"),
                                             vmem_limit_bytes=VMEM_LIMIT),
        name="dilated_attn",
    )(*([proj] * 9), bias)


def _mix_out_body(x_ref, oa_ref, ob_ref, ga_ref, gb_ref, wa_ref, wb_ref, wo_ref, ln_ref, o_ref):
    wide = lambda ref: jnp.concatenate([ref[c] for c in range(ref.shape[0])], axis=1)
    ya = _mm(wide(oa_ref), wa_ref[...])
    yb = _mm(wide(ob_ref), wb_ref[...])
    merged = _sigmoid(wide(ga_ref).astype(F32)) * ya + _sigmoid(wide(gb_ref).astype(F32)) * yb
    y = _mm(merged.astype(BF16), wo_ref[...])
    o_ref[...] = x_ref[...] + _rms(y, ln_ref[...])


def _mix_out(x3d, o_a, o_b, proj, w_a, w_b, w_o, ln_post, tm):
    b, s, d = x3d.shape
    nslab = d // LANES
    assert CB_GA % nslab == 0 and CB_GB % nslab == 0 and s % tm == 0
    const = lambda shape: pl.BlockSpec(shape, lambda i, j: (0, 0))
    slabs = lambda n, blk: pl.BlockSpec((None, n, tm, LANES), lambda i, j, blk=blk: (i, blk, j, 0))
    return pl.pallas_call(
        _mix_out_body,
        grid=(b, s // tm),
        in_specs=[pl.BlockSpec((None, tm, d), lambda i, j: (i, j, 0)),
                  slabs(GDN_HEADS, 0), slabs(DIL_HEADS_PER_GROUP, 0),
                  slabs(nslab, CB_GA // nslab), slabs(nslab, CB_GB // nslab),
                  const((GDN_WIDTH, d)), const((DIL_OUT_WIDTH, d)), const((d, d)), const((1, d))],
        out_specs=pl.BlockSpec((None, tm, d), lambda i, j: (i, j, 0)),
        out_shape=jax.ShapeDtypeStruct((b, s, d), F32),
        compiler_params=pltpu.CompilerParams(dimension_semantics=("parallel", "parallel"),
                                             vmem_limit_bytes=VMEM_LIMIT),
        name="mix_out",
    )(x3d, o_a, o_b, proj, proj, w_a, w_b, w_o, ln_post)


def _mlp_body(x_ref, g1_ref, w1_ref, w2_ref, g2_ref, o_ref, h_ref, acc_ref):
    k = pl.program_id(1)

    @pl.when(k == 0)
    def _():
        h_ref[...] = _rms(x_ref[...], g1_ref[...]).astype(BF16)

    f = jnp.maximum(_mm(h_ref[...], w1_ref[...]), 0.0)
    part = _mm((f * f).astype(BF16), w2_ref[...])

    @pl.when(k == 0)
    def _():
        acc_ref[...] = part

    @pl.when(k > 0)
    def _():
        acc_ref[...] += part

    @pl.when(k == pl.num_programs(1) - 1)
    def _():
        o_ref[...] = x_ref[...] + _rms(acc_ref[...], g2_ref[...])


def _mlp(x2d, ln_pre, w1, w2, ln_post, tm, tf):
    t, d = x2d.shape
    dff = w1.shape[1]
    return pl.pallas_call(
        _mlp_body,
        grid=(t // tm, dff // tf),
        in_specs=[pl.BlockSpec((tm, d), lambda i, k: (i, 0)),
                  pl.BlockSpec((1, d), lambda i, k: (0, 0)),
                  pl.BlockSpec((d, tf), lambda i, k: (0, k)),
                  pl.BlockSpec((tf, d), lambda i, k: (k, 0)),
                  pl.BlockSpec((1, d), lambda i, k: (0, 0))],
        out_specs=pl.BlockSpec((tm, d), lambda i, k: (i, 0)),
        out_shape=jax.ShapeDtypeStruct((t, d), F32),
        scratch_shapes=[pltpu.VMEM((tm, d), BF16), pltpu.VMEM((tm, d), F32)],
        compiler_params=pltpu.CompilerParams(dimension_semantics=("parallel", "arbitrary"),
                                             vmem_limit_bytes=VMEM_LIMIT),
        name="mlp",
    )(x2d, ln_pre, w1, w2, ln_post)


def _t5_bucket_np(rel):
    nb = REL_BUCKETS // 2
    ret = (rel > 0).astype(np.int32) * nb
    n = np.abs(rel)
    max_exact = nb // 2
    large = max_exact + (np.log(np.maximum(n, 1) / max_exact) / math.log(REL_MAX_DIST / max_exact)
                         * (nb - max_exact)).astype(np.int32)
    large = np.minimum(large, nb - 1)
    return ret + np.where(n < max_exact, n, large).astype(np.int32)


def _attention_bias(rel_bias):
    c = np.arange(BIAS_LEN)
    off = np.where(c < KWIN, np.clip(c - HALF_WINDOW, -HALF_WINDOW, HALF_WINDOW), -HALF_WINDOW)
    per_group = []
    for gi, (_, dil) in enumerate(DIL_GROUPS):
        bt = rel_bias[_t5_bucket_np(off * dil)]
        per_group.append(bt[:, gi * DIL_HEADS_PER_GROUP:(gi + 1) * DIL_HEADS_PER_GROUP])
    return jnp.transpose(jnp.stack(per_group, axis=0), (2, 0, 1))[:, :, None, :].astype(F32)


def _gdn_params(a_log_f, a_log_b, dt_bias_f, dt_bias_b):
    pad = lambda f, bk: jnp.pad(jnp.concatenate([f, bk]), (0, LANES - 2 * GDN_HEADS))
    return jnp.stack([pad(a_log_f, a_log_b), pad(dt_bias_f, dt_bias_b)]
                     + [jnp.zeros((LANES,), F32)] * 6, axis=0).astype(F32)


def kernel(x, rel_bias, ln_mix_pre, w_in, conv_w, a_log_f, a_log_b, dt_bias_f, dt_bias_b, norm_a,
           w_branch_a, w_branch_b, w_out, ln_mix_post, ln_mlp_pre, w_ff1, w_ff2, ln_mlp_post):
    b, s, d = x.shape
    t = b * s
    c_small = 4 * GDN_WIDTH
    bias = _attention_bias(rel_bias)
    for l in range(ln_mix_pre.shape[0]):
        w = w_in[l].astype(BF16)
        c_qb = c_small + N_SMALL
        c_ga = c_qb + 3 * DIL_WIDTH
        w_wide = jnp.concatenate([w[:, :c_small], w[:, c_ga:], w[:, c_qb:c_ga]], axis=1)
        w_small = jnp.pad(w[:, c_small:c_qb], ((0, 0), (0, LANES - N_SMALL)))
        proj, small = _in_proj(x, ln_mix_pre[l][None, :], w_wide, w_small, 768)

        lanepar = _gdn_params(a_log_f[l], a_log_b[l], dt_bias_f[l], dt_bias_b[l])
        o_a = _gdn(proj, small, lanepar, conv_w[l].astype(F32), norm_a[l][None, :].astype(F32))
        o_b = _dilated(proj, bias)

        x1 = _mix_out(x, o_a, o_b, proj,
                      w_branch_a[l].astype(BF16), w_branch_b[l].astype(BF16), w_out[l].astype(BF16),
                      ln_mix_post[l][None, :], 512)
        x2d = _mlp(x1.reshape(t, d), ln_mlp_pre[l][None, :], w_ff1[l].astype(BF16), w_ff2[l].astype(BF16),
                   ln_mlp_post[l][None, :], 1024, 1024)
        x = x2d.reshape(b, s, d)
    return x
```

```python
import math

import numpy as np
import jax
import jax.numpy as jnp
from jax import lax
from jax.experimental import pallas as pl
from jax.experimental.pallas import tpu as pltpu

F32 = jnp.float32
BF16 = jnp.bfloat16
HIGHEST = lax.Precision.HIGHEST

D_MODEL = 1024
GDN_HEADS = 8
HEAD_DIM = 128
GDN_WIDTH = GDN_HEADS * HEAD_DIM
CHUNK = 64
TRI_BASE = 8
CHUNK_GROUP = 8
GDN_HEADS_PER_STEP = 2
ATT_GROUP = 8
CONV_K = 5
DIL_GROUPS = ((128, 1), (512, 4), (2048, 16))
DIL_HEADS_PER_GROUP = 4
DIL_HEADS = 12
DIL_WIDTH = DIL_HEADS * HEAD_DIM
DIL_OUT_WIDTH = DIL_HEADS_PER_GROUP * HEAD_DIM
HALF_WINDOW = 64
QBLK = 128
KWIN = QBLK + 2 * HALF_WINDOW
REL_BUCKETS = 32
REL_MAX_DIST = 1024
D_FF = 4 * D_MODEL
EPS = 1e-6
NEG = -1e30
N_SMALL = 4 * GDN_HEADS
LANES = 128
BIAS_LEN = 3 * LANES

CB_QA, CB_KA, CB_VA, CB_ZA = 0, 8, 16, 24
CB_GA, CB_GB = 32, 40
CB_QB, CB_KB, CB_VB = 48, 60, 72
N_WIDE = 84 * LANES

VMEM_LIMIT = 56 * 1024 * 1024


def _mm(a, b):
    return jnp.dot(a, b, preferred_element_type=F32)


def _mm_nt(a, b):
    return lax.dot_general(a, b, (((1,), (1,)), ((), ())), preferred_element_type=F32)


def _mm_hi(a, b):
    return jnp.dot(a, b, preferred_element_type=F32, precision=HIGHEST)


def _rms(x, gain):
    return x * lax.rsqrt(jnp.mean(x * x, axis=-1, keepdims=True) + EPS) * gain


def _sigmoid(x):
    return 1.0 / (1.0 + jnp.exp(-x))


def _silu(x):
    h = 0.5 * x
    return h + h * jnp.tanh(h)


def _softplus(x):
    return jnp.maximum(x, 0.0) + jnp.log(1.0 + jnp.exp(-jnp.abs(x)))


def _in_proj_body(x_ref, g_ref, w_ref, ws_ref, o_ref, small_ref, h_ref):
    @pl.when(pl.program_id(1) == 0)
    def _():
        h_ref[...] = _rms(x_ref[...], g_ref[...]).astype(BF16)
        small_ref[...] = _mm(h_ref[...], ws_ref[...])

    res = _mm(h_ref[...], w_ref[...])
    for c in range(o_ref.shape[0]):
        o_ref[c] = res[:, c * LANES:(c + 1) * LANES].astype(o_ref.dtype)


def _in_proj(x3d, gain, w, w_small, tn):
    b, s, d = x3d.shape
    n = w.shape[1]
    return pl.pallas_call(
        _in_proj_body,
        grid=(b, n // tn),
        in_specs=[pl.BlockSpec((None, s, d), lambda i, j: (i, 0, 0)),
                  pl.BlockSpec((1, d), lambda i, j: (0, 0)),
                  pl.BlockSpec((d, tn), lambda i, j: (0, j)),
                  pl.BlockSpec((d, LANES), lambda i, j: (0, 0))],
        out_specs=[pl.BlockSpec((None, tn // LANES, s, LANES), lambda i, j: (i, j, 0, 0)),
                   pl.BlockSpec((None, s, LANES), lambda i, j: (i, 0, 0))],
        out_shape=[jax.ShapeDtypeStruct((b, n // LANES, s, LANES), BF16),
                   jax.ShapeDtypeStruct((b, s, LANES), F32)],
        scratch_shapes=[pltpu.VMEM((s, d), BF16)],
        compiler_params=pltpu.CompilerParams(dimension_semantics=("parallel", "arbitrary"),
                                             vmem_limit_bytes=VMEM_LIMIT),
        name="in_proj",
    )(x3d, gain, w, w_small)


def _split_bf16(x):
    hi = x.astype(BF16)
    return hi, (x - hi.astype(F32)).astype(BF16)


def _split3_lanes(x):
    hi = x.astype(BF16)
    r = x - hi.astype(F32)
    mid = r.astype(BF16)
    lo = (r - mid.astype(F32)).astype(BF16)
    return jnp.concatenate([hi, mid, lo], axis=1)


def _block_diag(x, groups):
    return jnp.concatenate([jnp.where(g, x, 0.0) for g in groups], axis=0)


def _mm_split(lhs, rhs):
    lh, ll = _split_bf16(lhs)
    rh, rl = _split_bf16(rhs)
    return (_mm(jnp.concatenate([lh, ll], axis=1), jnp.concatenate([rh, rh], axis=0))
            + _mm(lh, rl))


def _unit_tri_inverse_pairs(a2s, eye, lo, same_blk):
    ds = [jnp.where(same_blk[0], a2, 0.0) for a2 in a2s]
    xs = [_mm_split(d, _block_diag(d, lo)) for d in ds]
    ts = [eye - d for d in ds]
    levels = int(math.log2(TRI_BASE)) - 1
    for lvl in range(levels):
        if lvl < levels - 1:
            outs = [_mm_split(jnp.concatenate([t, x], axis=0), _block_diag(x, lo)) for t, x in zip(ts, xs)]
            ts = [t + out[:CHUNK] for t, out in zip(ts, outs)]
            xs = [out[CHUNK:] for out in outs]
        else:
            ts = [t + _mm_split(t, _block_diag(x, lo)) for t, x in zip(ts, xs)]
    inner = same_blk[0]
    for outer in tuple(same_blk[1:]) + (None,):
        between = [jnp.where(inner, 0.0, a2) if outer is None else jnp.where(outer & ~inner, a2, 0.0)
                   for a2 in a2s]
        tls = [_mm_split(t, _block_diag(l, lo)) for t, l in zip(ts, between)]
        ts = [t - _mm_split(tl, _block_diag(t, lo)) for t, tl in zip(ts, tls)]
        inner = outer
    return ts


def _gdn_body(lanepar_ref, cwq_ref, cwk_ref, cwv_ref, q_ref, k_ref, v_ref, z_ref, small_ref,
              na_ref, o_ref,
              xp, qn, kn, vn, act, gcf, bcf, mq_s, c_s, dec_s, of_s, ob_s):
    nh, s, _ = q_ref.shape
    nchunk = s // CHUNK
    hp = pl.program_id(1)
    rc = 256

    @pl.when(hp == 0)
    def _():
        lane = lax.broadcasted_iota(jnp.int32, (rc, LANES), 1)
        bi = lax.broadcasted_iota(jnp.int32, (rc, rc), 0)
        bj = lax.broadcasted_iota(jnp.int32, (rc, rc), 1)
        same_chunk = jnp.bitwise_xor(bi, bj) < CHUNK
        cum_f = jnp.where(same_chunk & (bj <= bi), 1.0, 0.0).astype(BF16)
        cum_b = jnp.where(same_chunk & (bj >= bi), 1.0, 0.0).astype(BF16)
        neg_a = -jnp.exp(lanepar_ref[0:1, :])
        dtb = lanepar_ref[1:2, :]
        sum3 = lambda y: y[:, :LANES] + y[:, LANES:2 * LANES] + y[:, 2 * LANES:]
        for c in range(s // rc):
            rows = pl.ds(c * rc, rc)
            slab = small_ref[rows, :]
            g3 = _split3_lanes(neg_a * _softplus(slab + dtb))
            act[rows, :] = jnp.where(lane < GDN_HEADS, sum3(_mm(cum_f, g3)),
                                     jnp.where(lane < 2 * GDN_HEADS, sum3(_mm(cum_b, g3)), _sigmoid(slab)))

    for hh in range(nh):
        lanes = pl.ds(hh * HEAD_DIM, HEAD_DIM)
        _gdn_prepare(hp * nh + hh, cwq_ref.at[:, lanes], cwk_ref.at[:, lanes], cwv_ref.at[:, lanes],
                     q_ref.at[hh], k_ref.at[hh], v_ref.at[hh],
                     xp, qn, kn, vn, act, gcf, bcf,
                     mq_s.at[hh], c_s.at[hh], dec_s.at[hh], of_s.at[hh], ob_s.at[hh])

    def phase2(step, states):
        new = []
        for idx, st in enumerate(states):
            hh, d = divmod(idx, 2)
            n = step if d == 0 else nchunk - 1 - step
            r = _mm(mq_s[hh, d, n], st.astype(BF16))
            rows = pl.ds(pl.multiple_of(n * CHUNK, CHUNK), CHUNK)
            out_s = of_s if d == 0 else ob_s
            out_s[hh, rows, :] = out_s[hh, rows, :] + r[HEAD_DIM:]
            new.append(st * dec_s[hh, d, n] + r[:HEAD_DIM] + c_s[hh, d, n])
        return tuple(new)

    zero_state = jnp.zeros((HEAD_DIM, HEAD_DIM), F32)
    lax.fori_loop(0, nchunk, phase2, (zero_state,) * (2 * nh))

    for hh in range(nh):
        for c in range(s // rc):
            rows = pl.ds(c * rc, rc)
            o = of_s[hh, rows, :] + ob_s[hh, rows, :]
            z = z_ref[hh, rows, :].astype(F32)
            o_ref[hh, rows, :] = (_rms(o, na_ref[...]) * _silu(z)).astype(o_ref.dtype)


def _gdn_prepare(h, cwq_ref, cwk_ref, cwv_ref, q_ref, k_ref, v_ref,
                 xp, qn, kn, vn, act, gcf, bcf, mq_s, c_s, dec_s, of_s, ob_s):
    s = q_ref.shape[0]
    nchunk = s // CHUNK
    rc = 256

    zeros8 = jnp.zeros((8, HEAD_DIM), F32)
    xp[pl.ds(0, 8), :] = zeros8
    xp[pl.ds(8 + s, 8), :] = zeros8

    def conv_into(src_ref, cw_ref, dst_ref, l2, scale):
        xp[pl.ds(8, s), :] = src_ref[...].astype(F32)
        for c in range(s // rc):
            acc = None
            for j in range(CONV_K):
                t = xp[pl.ds(c * rc + 8 - CONV_K // 2 + j, rc), :] * cw_ref[j:j + 1, :]
                acc = t if acc is None else acc + t
            y = _silu(acc)
            if l2:
                y = y * lax.rsqrt(jnp.sum(y * y, axis=-1, keepdims=True) + EPS)
                if scale != 1.0:
                    y = y * scale
            dst_ref[pl.ds(c * rc, rc), :] = y

    conv_into(q_ref, cwq_ref, qn, True, HEAD_DIM ** -0.5)
    conv_into(k_ref, cwk_ref, kn, True, 1.0)
    conv_into(v_ref, cwv_ref, vn, False, 1.0)

    sel_k = lax.broadcasted_iota(jnp.int32, (LANES, 4 * LANES), 0)
    sel_c = lax.broadcasted_iota(jnp.int32, (LANES, 4 * LANES), 1)
    sel = jnp.where(sel_k == (sel_c >> 7) * GDN_HEADS + h, 1.0, 0.0).astype(BF16)
    sel3 = jnp.concatenate([sel, sel, sel], axis=0)
    for c in range(s // rc):
        rows = pl.ds(c * rc, rc)
        picked = _mm(_split3_lanes(act[rows, :]), sel3)
        for d in range(2):
            gcf[d, rows, :] = picked[:, d * LANES:(d + 1) * LANES]
            bcf[d, rows, :] = picked[:, (2 + d) * LANES:(3 + d) * LANES]
    ii = lax.broadcasted_iota(jnp.int32, (CHUNK, 2 * CHUNK), 0)
    lj = lax.broadcasted_iota(jnp.int32, (CHUNK, 2 * CHUNK), 1)
    lo = lj < CHUNK
    hi = lj >= CHUNK
    jj = jnp.where(lo, lj, lj - CHUNK)
    is_diag = ii == jj
    tri2 = (lo & (ii >= jj)) | (hi & (ii <= jj))
    strict2 = (lo & (ii > jj)) | (hi & (ii < jj))
    pair = (lo, hi)
    lo_t = lax.broadcasted_iota(jnp.int32, (2 * CHUNK, 2 * CHUNK), 1) < CHUNK
    pair_t = (lo_t, ~lo_t)
    eye2 = jnp.where(is_diag, 1.0, 0.0)
    blk_bits = range(int(math.log2(TRI_BASE)), int(math.log2(CHUNK)))
    same_blk = tuple((ii >> sh) == (jj >> sh) for sh in blk_bits)

    def phase1(grp, carry):
        ns = [grp * CHUNK_GROUP + i for i in range(CHUNK_GROUP)]
        rws = [pl.ds(pl.multiple_of(n * CHUNK, CHUNK), CHUNK) for n in ns]
        qv = [qn[r, :] for r in rws]
        kv = [kn[r, :] for r in rws]
        kb = [k.astype(BF16) for k in kv]
        grams = [_mm_nt(jnp.concatenate([kb_, q.astype(BF16)], axis=0), jnp.concatenate([kb_, kb_], axis=0))
                 for kb_, q in zip(kb, qv)]
        gcc = [(gcf[0, r, :], gcf[1, r, :]) for r in rws]
        bcc = [(bcf[0, r, :], bcf[1, r, :]) for r in rws]
        gcc2 = [jnp.where(lo, gf, gb) for gf, gb in gcc]
        gcr2 = [jnp.sum(jnp.where(is_diag, g2, 0.0), axis=0, keepdims=True) for g2 in gcc2]
        gams = [jnp.where(tri2, jnp.exp(jnp.where(tri2, g2 - gr, 0.0)), 0.0) for g2, gr in zip(gcc2, gcr2)]
        a2s = [jnp.where(strict2, gram[:CHUNK] * jnp.where(lo, bf, bb) * gam, 0.0)
               for gram, (bf, bb), gam in zip(grams, bcc, gams)]
        t2s = _unit_tri_inverse_pairs(a2s, eye2, pair, same_blk)
        uws = []
        for i, r in enumerate(rws):
            v = vn[r, :]
            (gf, gb), (bf, bb) = gcc[i], bcc[i]
            rhs = jnp.concatenate([jnp.concatenate([v * bf, kv[i] * (bf * jnp.exp(gf))], axis=1),
                                   jnp.concatenate([v * bb, kv[i] * (bb * jnp.exp(gb))], axis=1)], axis=0)
            uws.append(_mm(_block_diag(t2s[i], pair).astype(BF16), rhs.astype(BF16)))
        fins = []
        for i in range(CHUNK_GROUP):
            gf, gb = gcc[i]
            gtot_f = gcr2[i][:, CHUNK - 1:CHUNK]
            gtot_b = gcr2[i][:, CHUNK:CHUNK + 1]
            kg = jnp.concatenate([kv[i] * jnp.exp(gtot_f - gf), kv[i] * jnp.exp(gtot_b - gb)], axis=0)
            lhs = jnp.concatenate([_block_diag(kg.T, pair_t), _block_diag(grams[i][CHUNK:] * gams[i], pair)], axis=0)
            fins.append(_mm(lhs.astype(BF16), uws[i].astype(BF16)))
            dec_s[0, ns[i]] = jnp.broadcast_to(jnp.exp(gtot_f), (1, HEAD_DIM))
            dec_s[1, ns[i]] = jnp.broadcast_to(jnp.exp(gtot_b), (1, HEAD_DIM))
        for i, (n, r) in enumerate(zip(ns, rws)):
            fin = fins[i]
            for d in range(2):
                top = fin[d * HEAD_DIM:(d + 1) * HEAD_DIM]
                bot = fin[2 * HEAD_DIM + d * CHUNK:2 * HEAD_DIM + (d + 1) * CHUNK]
                c_s[d, n] = top[:, :HEAD_DIM]
                mq_s[d, n, pl.ds(0, HEAD_DIM), :] = (-top[:, HEAD_DIM:]).astype(BF16)
                mq_s[d, n, pl.ds(HEAD_DIM, CHUNK), :] = (qv[i] * jnp.exp(gcc[i][d]) - bot[:, HEAD_DIM:]).astype(BF16)
                (of_s if d == 0 else ob_s)[r, :] = bot[:, :HEAD_DIM]
        return carry

    lax.fori_loop(0, nchunk // CHUNK_GROUP, phase1, 0)


def _gdn(proj, small, lanepar, conv_w, norm_a):
    b, _, s, _ = proj.shape
    nchunk = s // CHUNK
    nh = GDN_HEADS_PER_STEP
    assert 2 * CHUNK == LANES == HEAD_DIM and nchunk % CHUNK_GROUP == 0 and GDN_HEADS % nh == 0
    assert all(cb % nh == 0 for cb in (CB_QA, CB_KA, CB_VA, CB_ZA))
    col = lambda cb: pl.BlockSpec((None, nh, s, HEAD_DIM), lambda i, h, cb=cb: (i, cb // nh + h, 0, 0))
    cw = lambda cb: pl.BlockSpec((CONV_K, nh * HEAD_DIM), lambda i, h, cb=cb: (0, cb // nh + h))
    return pl.pallas_call(
        _gdn_body,
        grid=(b, GDN_HEADS // nh),
        in_specs=[pl.BlockSpec((8, LANES), lambda i, h: (0, 0)),
                  cw(0), cw(GDN_HEADS), cw(2 * GDN_HEADS),
                  col(CB_QA), col(CB_KA), col(CB_VA), col(CB_ZA),
                  pl.BlockSpec((None, s, LANES), lambda i, h: (i, 0, 0)),
                  pl.BlockSpec((1, HEAD_DIM), lambda i, h: (0, 0))],
        out_specs=pl.BlockSpec((None, nh, s, HEAD_DIM), lambda i, h: (i, h, 0, 0)),
        out_shape=jax.ShapeDtypeStruct((b, GDN_HEADS, s, HEAD_DIM), BF16),
        scratch_shapes=[
            pltpu.VMEM((s + 16, HEAD_DIM), F32),
            pltpu.VMEM((s, HEAD_DIM), F32),
            pltpu.VMEM((s, HEAD_DIM), F32),
            pltpu.VMEM((s, HEAD_DIM), F32),
            pltpu.VMEM((s, LANES), F32),
            pltpu.VMEM((2, s, LANES), F32),
            pltpu.VMEM((2, s, LANES), F32),
            pltpu.VMEM((nh, 2, nchunk, HEAD_DIM + CHUNK, HEAD_DIM), BF16),
            pltpu.VMEM((nh, 2, nchunk, HEAD_DIM, HEAD_DIM), F32),
            pltpu.VMEM((nh, 2, nchunk, 1, HEAD_DIM), F32),
            pltpu.VMEM((nh, s, HEAD_DIM), F32),
            pltpu.VMEM((nh, s, HEAD_DIM), F32),
        ],
        compiler_params=pltpu.CompilerParams(dimension_semantics=("parallel", "arbitrary"),
                                             vmem_limit_bytes=VMEM_LIMIT),
        name="gdn",
    )(lanepar, conv_w, conv_w, conv_w, proj, proj, proj, proj, small, norm_a)


def _dil_body(q0, k0, v0, q1, k1, v1, q2, k2, v2, bias_ref, o_ref,
              qf, kf, vf, qs, kp, vp, acc_s, m_s, l_s):
    s = q0.shape[0]
    scale = HEAD_DIM ** -0.5
    qi = lax.broadcasted_iota(jnp.int32, (QBLK, KWIN), 0)
    kj = lax.broadcasted_iota(jnp.int32, (QBLK, KWIN), 1)
    off = kj - HALF_WINDOW - qi
    band = (off >= -HALF_WINDOW) & (off <= HALF_WINDOW)
    zpad = jnp.zeros((HALF_WINDOW, HEAD_DIM), F32)

    for g, (q_ref, k_ref, v_ref) in enumerate(((q0, k0, v0), (q1, k1, v1), (q2, k2, v2))):
        dil = DIL_GROUPS[g][1]
        length = s // dil
        nblk = length // QBLK
        qf[...] = q_ref[...].astype(F32)
        kf[...] = k_ref[...].astype(F32)
        vf[...] = v_ref[...].astype(F32)
        seg_q = max(length, QBLK)
        seg_k = seg_q + 2 * HALF_WINDOW
        nseg = min(dil, max(1, ATT_GROUP // nblk))
        bpi = min(nblk, ATT_GROUP)
        for c in range(nseg):
            for buf in (kp, vp):
                buf[pl.ds(c * seg_k, HALF_WINDOW), :] = zpad
                buf[pl.ds(c * seg_k + HALF_WINDOW + length, HALF_WINDOW), :] = zpad
        bias = pltpu.roll(jnp.broadcast_to(bias_ref[g], (QBLK, BIAS_LEN)), 0, 1,
                          stride=1, stride_axis=0)[:, :KWIN]

        def stage(c, r, dil=dil, length=length, seg_q=seg_q, seg_k=seg_k):
            cls = pl.ds(r, length, stride=dil) if dil > 1 else pl.ds(0, length)
            qs[pl.ds(c * seg_q, length), :] = qf[cls, :]
            kp[pl.ds(c * seg_k + HALF_WINDOW, length), :] = kf[cls, :]
            vp[pl.ds(c * seg_k + HALF_WINDOW, length), :] = vf[cls, :]

        def attend(items, g=g, dil=dil, length=length, bias=bias):
            logits = []
            for q_off, k_off, j0, r in items:
                kpos = j0 - HALF_WINDOW + kj
                ok = band & (kpos >= 0) & (kpos < length)
                sc = _mm_nt(qs[pl.ds(q_off, QBLK), :].astype(BF16), kp[pl.ds(k_off, KWIN), :].astype(BF16))
                logits.append(jnp.where(ok, sc * scale + bias, NEG))
            ms = [jnp.max(lg, axis=-1, keepdims=True) for lg in logits]
            ps = [jnp.exp(lg - m) for lg, m in zip(logits, ms)]
            ls = [jnp.sum(p, axis=-1, keepdims=True) for p in ps]
            accs = [_mm(p.astype(BF16), vp[pl.ds(it[1], KWIN), :].astype(BF16)) for p, it in zip(ps, items)]
            for (q_off, k_off, j0, r), m, lsum, acc in zip(items, ms, ls, accs):
                tok = pl.ds(r + dil * j0, QBLK, stride=dil) if dil > 1 else pl.ds(j0, QBLK)
                acc_s[g, tok, :] = acc
                m_s[g, tok, :] = m
                l_s[g, tok, :] = lsum

        def classes(it, carry, nblk=nblk, nseg=nseg, bpi=bpi, seg_q=seg_q, seg_k=seg_k,
                    stage=stage, attend=attend):
            rs = [it * nseg + c for c in range(nseg)]
            for c, r in enumerate(rs):
                stage(c, r)

            def blocks(jt, carry2):
                items = []
                for c, r in enumerate(rs):
                    for i in range(bpi):
                        j0 = (jt * bpi + i) * QBLK
                        if not isinstance(j0, int):
                            j0 = pl.multiple_of(j0, QBLK)
                        items.append((c * seg_q + j0, c * seg_k + j0, j0, r))
                attend(items)
                return carry2

            if nblk == bpi:
                blocks(0, 0)
            else:
                lax.fori_loop(0, nblk // bpi, blocks, 0)
            return carry

        lax.fori_loop(0, dil // nseg, classes, 0)

    rc = 256
    for c in range(s // rc):
        rows = pl.ds(c * rc, rc)
        ms = [m_s[g, rows, :] for g in range(3)]
        mx = jnp.maximum(jnp.maximum(ms[0], ms[1]), ms[2])
        ws = [jnp.exp(ms[g] - mx) for g in range(3)]
        den = ws[0] * l_s[0, rows, :] + ws[1] * l_s[1, rows, :] + ws[2] * l_s[2, rows, :]
        num = ws[0] * acc_s[0, rows, :] + ws[1] * acc_s[1, rows, :] + ws[2] * acc_s[2, rows, :]
        o_ref[rows, :] = (num / den).astype(o_ref.dtype)


def _dilated(proj, bias):
    b, _, s, _ = proj.shape
    specs = []
    for g in range(3):
        for cb in (CB_QB, CB_KB, CB_VB):
            specs.append(pl.BlockSpec((None, None, s, HEAD_DIM),
                                      lambda i, h, cb=cb, g=g: (i, cb + g * DIL_HEADS_PER_GROUP + h, 0, 0)))
    specs.append(pl.BlockSpec((None, 3, 1, BIAS_LEN), lambda i, h: (h, 0, 0, 0)))
    return pl.pallas_call(
        _dil_body,
        grid=(b, DIL_HEADS_PER_GROUP),
        in_specs=specs,
        out_specs=pl.BlockSpec((None, None, s, HEAD_DIM), lambda i, h: (i, h, 0, 0)),
        out_shape=jax.ShapeDtypeStruct((b, DIL_HEADS_PER_GROUP, s, HEAD_DIM), BF16),
        scratch_shapes=[
            pltpu.VMEM((s, HEAD_DIM), F32),
            pltpu.VMEM((s, HEAD_DIM), F32),
            pltpu.VMEM((s, HEAD_DIM), F32),
            pltpu.VMEM((s, HEAD_DIM), F32),
            pltpu.VMEM((s + 2 * HALF_WINDOW, HEAD_DIM), F32),
            pltpu.VMEM((s + 2 * HALF_WINDOW, HEAD_DIM), F32),
            pltpu.VMEM((3, s, HEAD_DIM), F32),
            pltpu.VMEM((3, s, 1), F32),
            pltpu.VMEM((3, s, 1), F32),
        ],
        compiler_params=pltpu.CompilerParams(dimension_semantics=("parallel", "arbitrary"),
                                             vmem_limit_bytes=VMEM_LIMIT),
        name="dilated_attn",
    )(*([proj] * 9), bias)


def _mix_out_body(x_ref, oa_ref, ob_ref, ga_ref, gb_ref, wa_ref, wb_ref, wo_ref, ln_ref, o_ref):
    wide = lambda ref: jnp.concatenate([ref[c] for c in range(ref.shape[0])], axis=1)
    ya = _mm(wide(oa_ref), wa_ref[...])
    yb = _mm(wide(ob_ref), wb_ref[...])
    merged = _sigmoid(wide(ga_ref).astype(F32)) * ya + _sigmoid(wide(gb_ref).astype(F32)) * yb
    y = _mm(merged.astype(BF16), wo_ref[...])
    o_ref[...] = x_ref[...] + _rms(y, ln_ref[...])


def _mix_out(x3d, o_a, o_b, proj, w_a, w_b, w_o, ln_post, tm):
    b, s, d = x3d.shape
    nslab = d // LANES
    assert CB_GA % nslab == 0 and CB_GB % nslab == 0 and s % tm == 0
    const = lambda shape: pl.BlockSpec(shape, lambda i, j: (0, 0))
    slabs = lambda n, blk: pl.BlockSpec((None, n, tm, LANES), lambda i, j, blk=blk: (i, blk, j, 0))
    return pl.pallas_call(
        _mix_out_body,
        grid=(b, s // tm),
        in_specs=[pl.BlockSpec((None, tm, d), lambda i, j: (i, j, 0)),
                  slabs(GDN_HEADS, 0), slabs(DIL_HEADS_PER_GROUP, 0),
                  slabs(nslab, CB_GA // nslab), slabs(nslab, CB_GB // nslab),
                  const((GDN_WIDTH, d)), const((DIL_OUT_WIDTH, d)), const((d, d)), const((1, d))],
        out_specs=pl.BlockSpec((None, tm, d), lambda i, j: (i, j, 0)),
        out_shape=jax.ShapeDtypeStruct((b, s, d), F32),
        compiler_params=pltpu.CompilerParams(dimension_semantics=("parallel", "parallel"),
                                             vmem_limit_bytes=VMEM_LIMIT),
        name="mix_out",
    )(x3d, o_a, o_b, proj, proj, w_a, w_b, w_o, ln_post)


def _mlp_body(x_ref, g1_ref, w1_ref, w2_ref, g2_ref, o_ref, h_ref, acc_ref):
    k = pl.program_id(1)

    @pl.when(k == 0)
    def _():
        h_ref[...] = _rms(x_ref[...], g1_ref[...]).astype(BF16)

    f = jnp.maximum(_mm(h_ref[...], w1_ref[...]), 0.0)
    part = _mm((f * f).astype(BF16), w2_ref[...])

    @pl.when(k == 0)
    def _():
        acc_ref[...] = part

    @pl.when(k > 0)
    def _():
        acc_ref[...] += part

    @pl.when(k == pl.num_programs(1) - 1)
    def _():
        o_ref[...] = x_ref[...] + _rms(acc_ref[...], g2_ref[...])


def _mlp(x2d, ln_pre, w1, w2, ln_post, tm, tf):
    t, d = x2d.shape
    dff = w1.shape[1]
    return pl.pallas_call(
        _mlp_body,
        grid=(t // tm, dff // tf),
        in_specs=[pl.BlockSpec((tm, d), lambda i, k: (i, 0)),
                  pl.BlockSpec((1, d), lambda i, k: (0, 0)),
                  pl.BlockSpec((d, tf), lambda i, k: (0, k)),
                  pl.BlockSpec((tf, d), lambda i, k: (k, 0)),
                  pl.BlockSpec((1, d), lambda i, k: (0, 0))],
        out_specs=pl.BlockSpec((tm, d), lambda i, k: (i, 0)),
        out_shape=jax.ShapeDtypeStruct((t, d), F32),
        scratch_shapes=[pltpu.VMEM((tm, d), BF16), pltpu.VMEM((tm, d), F32)],
        compiler_params=pltpu.CompilerParams(dimension_semantics=("parallel", "arbitrary"),
                                             vmem_limit_bytes=VMEM_LIMIT),
        name="mlp",
    )(x2d, ln_pre, w1, w2, ln_post)


def _t5_bucket_np(rel):
    nb = REL_BUCKETS // 2
    ret = (rel > 0).astype(np.int32) * nb
    n = np.abs(rel)
    max_exact = nb // 2
    large = max_exact + (np.log(np.maximum(n, 1) / max_exact) / math.log(REL_MAX_DIST / max_exact)
                         * (nb - max_exact)).astype(np.int32)
    large = np.minimum(large, nb - 1)
    return ret + np.where(n < max_exact, n, large).astype(np.int32)


def _attention_bias(rel_bias):
    c = np.arange(BIAS_LEN)
    off = np.where(c < KWIN, np.clip(c - HALF_WINDOW, -HALF_WINDOW, HALF_WINDOW), -HALF_WINDOW)
    per_group = []
    for gi, (_, dil) in enumerate(DIL_GROUPS):
        bt = rel_bias[_t5_bucket_np(off * dil)]
        per_group.append(bt[:, gi * DIL_HEADS_PER_GROUP:(gi + 1) * DIL_HEADS_PER_GROUP])
    return jnp.transpose(jnp.stack(per_group, axis=0), (2, 0, 1))[:, :, None, :].astype(F32)


def _gdn_params(a_log_f, a_log_b, dt_bias_f, dt_bias_b):
    pad = lambda f, bk: jnp.pad(jnp.concatenate([f, bk]), (0, LANES - 2 * GDN_HEADS))
    return jnp.stack([pad(a_log_f, a_log_b), pad(dt_bias_f, dt_bias_b)]
                     + [jnp.zeros((LANES,), F32)] * 6, axis=0).astype(F32)


def kernel(x, rel_bias, ln_mix_pre, w_in, conv_w, a_log_f, a_log_b, dt_bias_f, dt_bias_b, norm_a,
           w_branch_a, w_branch_b, w_out, ln_mix_post, ln_mlp_pre, w_ff1, w_ff2, ln_mlp_post):
    b, s, d = x.shape
    t = b * s
    c_small = 4 * GDN_WIDTH
    bias = _attention_bias(rel_bias)
    for l in range(ln_mix_pre.shape[0]):
        w = w_in[l].astype(BF16)
        c_qb = c_small + N_SMALL
        c_ga = c_qb + 3 * DIL_WIDTH
        w_wide = jnp.concatenate([w[:, :c_small], w[:, c_ga:], w[:, c_qb:c_ga]], axis=1)
        w_small = jnp.pad(w[:, c_small:c_qb], ((0, 0), (0, LANES - N_SMALL)))
        proj, small = _in_proj(x, ln_mix_pre[l][None, :], w_wide, w_small, 768)

        lanepar = _gdn_params(a_log_f[l], a_log_b[l], dt_bias_f[l], dt_bias_b[l])
        o_a = _gdn(proj, small, lanepar, conv_w[l].astype(F32), norm_a[l][None, :].astype(F32))
        o_b = _dilated(proj, bias)

        x1 = _mix_out(x, o_a, o_b, proj,
                      w_branch_a[l].astype(BF16), w_branch_b[l].astype(BF16), w_out[l].astype(BF16),
                      ln_mix_post[l][None, :], 512)
        x2d = _mlp(x1.reshape(t, d), ln_mlp_pre[l][None, :], w_ff1[l].astype(BF16), w_ff2[l].astype(BF16),
                   ln_mlp_post[l][None, :], 1024, 1024)
        x = x2d.reshape(b, s, d)
    return x
```

```python
import functools
import math

import numpy as np
import jax
import jax.numpy as jnp
from jax import lax
from jax.experimental import pallas as pl
from jax.experimental.pallas import tpu as pltpu

F32 = jnp.float32
BF16 = jnp.bfloat16
HIGHEST = lax.Precision.HIGHEST

D_MODEL = 1024
GDN_HEADS = 8
HEAD_DIM = 128
GDN_WIDTH = GDN_HEADS * HEAD_DIM
CHUNK = 64
TRI_BASE = 8
CHUNK_GROUP = 8
GDN_HEADS_PER_STEP = 2
ATT_GROUP = 8
CONV_K = 5
DIL_GROUPS = ((128, 1), (512, 4), (2048, 16))
DIL_HEADS_PER_GROUP = 4
DIL_HEADS = 12
DIL_WIDTH = DIL_HEADS * HEAD_DIM
DIL_OUT_WIDTH = DIL_HEADS_PER_GROUP * HEAD_DIM
HALF_WINDOW = 64
QBLK = 128
KWIN = QBLK + 2 * HALF_WINDOW
REL_BUCKETS = 32
REL_MAX_DIST = 1024
D_FF = 4 * D_MODEL
EPS = 1e-6
NEG = -1e30
N_SMALL = 4 * GDN_HEADS
LANES = 128
BIAS_LEN = 3 * LANES

CB_QA, CB_KA, CB_VA, CB_ZA = 0, 8, 16, 24
CB_GA, CB_GB = 32, 40
CB_QB, CB_KB, CB_VB = 48, 60, 72
N_WIDE = 84 * LANES

VMEM_LIMIT = 56 * 1024 * 1024


def _mm(a, b):
    return jnp.dot(a, b, preferred_element_type=F32)


def _mm_nt(a, b):
    return lax.dot_general(a, b, (((1,), (1,)), ((), ())), preferred_element_type=F32)


def _mm_hi(a, b):
    return jnp.dot(a, b, preferred_element_type=F32, precision=HIGHEST)


def _rms(x, gain):
    return x * lax.rsqrt(jnp.mean(x * x, axis=-1, keepdims=True) + EPS) * gain


def _sigmoid(x):
    return 1.0 / (1.0 + jnp.exp(-x))


def _silu(x):
    h = 0.5 * x
    return h + h * jnp.tanh(h)


def _softplus(x):
    return jnp.maximum(x, 0.0) + jnp.log(1.0 + jnp.exp(-jnp.abs(x)))


def _in_proj_body(x_ref, g_ref, w_ref, ws_ref, o_ref, small_ref, h_ref):
    @pl.when(pl.program_id(1) == 0)
    def _():
        h_ref[...] = _rms(x_ref[...], g_ref[...]).astype(BF16)
        small_ref[...] = _mm(h_ref[...], ws_ref[...])

    h = h_ref[...]
    for c in range(0, o_ref.shape[0], 2):
        res = _mm(h, w_ref[:, c * LANES:(c + 2) * LANES])
        o_ref[c] = res[:, :LANES].astype(o_ref.dtype)
        o_ref[c + 1] = res[:, LANES:].astype(o_ref.dtype)


def _in_proj(x3d, gain, w, w_small, tn):
    b, s, d = x3d.shape
    n = w.shape[1]
    return pl.pallas_call(
        _in_proj_body,
        grid=(b, n // tn),
        in_specs=[pl.BlockSpec((None, s, d), lambda i, j: (i, 0, 0)),
                  pl.BlockSpec((1, d), lambda i, j: (0, 0)),
                  pl.BlockSpec((d, tn), lambda i, j: (0, j)),
                  pl.BlockSpec((d, LANES), lambda i, j: (0, 0))],
        out_specs=[pl.BlockSpec((None, tn // LANES, s, LANES), lambda i, j: (i, j, 0, 0)),
                   pl.BlockSpec((None, s, LANES), lambda i, j: (i, 0, 0))],
        out_shape=[jax.ShapeDtypeStruct((b, n // LANES, s, LANES), BF16),
                   jax.ShapeDtypeStruct((b, s, LANES), F32)],
        scratch_shapes=[pltpu.VMEM((s, d), BF16)],
        compiler_params=pltpu.CompilerParams(dimension_semantics=("parallel", "arbitrary"),
                                             vmem_limit_bytes=VMEM_LIMIT),
        name="in_proj",
    )(x3d, gain, w, w_small)


def _split_bf16(x):
    hi = x.astype(BF16)
    return hi, (x - hi.astype(F32)).astype(BF16)


def _split3_lanes(x):
    hi = x.astype(BF16)
    r = x - hi.astype(F32)
    mid = r.astype(BF16)
    lo = (r - mid.astype(F32)).astype(BF16)
    return jnp.concatenate([hi, mid, lo], axis=1)


def _block_diag(x, groups):
    return jnp.concatenate([jnp.where(g, x, 0.0) for g in groups], axis=0)


def _mm_split(lhs, rhs):
    lh, ll = _split_bf16(lhs)
    rh, rl = _split_bf16(rhs)
    return (_mm(jnp.concatenate([lh, ll], axis=1), jnp.concatenate([rh, rh], axis=0))
            + _mm(lh, rl))


def _unit_tri_inverse_pairs(a2s, eye, lo, same_blk):
    ds = [jnp.where(same_blk[0], a2, 0.0) for a2 in a2s]
    xs = [_mm_split(d, _block_diag(d, lo)) for d in ds]
    ts = [eye - d for d in ds]
    levels = int(math.log2(TRI_BASE)) - 1
    for lvl in range(levels):
        if lvl < levels - 1:
            outs = [_mm_split(jnp.concatenate([t, x], axis=0), _block_diag(x, lo)) for t, x in zip(ts, xs)]
            ts = [t + out[:CHUNK] for t, out in zip(ts, outs)]
            xs = [out[CHUNK:] for out in outs]
        else:
            ts = [t + _mm_split(t, _block_diag(x, lo)) for t, x in zip(ts, xs)]
    inner = same_blk[0]
    for outer in tuple(same_blk[1:]) + (None,):
        between = [jnp.where(inner, 0.0, a2) if outer is None else jnp.where(outer & ~inner, a2, 0.0)
                   for a2 in a2s]
        tls = [_mm_split(t, _block_diag(l, lo)) for t, l in zip(ts, between)]
        ts = [t - _mm_split(tl, _block_diag(t, lo)) for t, tl in zip(ts, tls)]
        inner = outer
    return ts


def _gdn_body(lanepar_ref, cwq_ref, cwk_ref, cwv_ref, q_ref, k_ref, v_ref, z_ref, small_ref,
              na_ref, o_ref,
              xp, qn, kn, vn, act, gcf, bcf, mq_s, c_s, dec_s, of_s, ob_s):
    nh, s, _ = q_ref.shape
    nchunk = s // CHUNK
    hp = pl.program_id(1)
    rc = 256

    @pl.when(hp == 0)
    def _():
        lane = lax.broadcasted_iota(jnp.int32, (rc, LANES), 1)
        bi = lax.broadcasted_iota(jnp.int32, (rc, rc), 0)
        bj = lax.broadcasted_iota(jnp.int32, (rc, rc), 1)
        same_chunk = jnp.bitwise_xor(bi, bj) < CHUNK
        cum_f = jnp.where(same_chunk & (bj <= bi), 1.0, 0.0).astype(BF16)
        cum_b = jnp.where(same_chunk & (bj >= bi), 1.0, 0.0).astype(BF16)
        neg_a = -jnp.exp(lanepar_ref[0:1, :])
        dtb = lanepar_ref[1:2, :]
        sum3 = lambda y: y[:, :LANES] + y[:, LANES:2 * LANES] + y[:, 2 * LANES:]
        for c in range(s // rc):
            rows = pl.ds(c * rc, rc)
            slab = small_ref[rows, :]
            g3 = _split3_lanes(neg_a * _softplus(slab + dtb))
            act[rows, :] = jnp.where(lane < GDN_HEADS, sum3(_mm(cum_f, g3)),
                                     jnp.where(lane < 2 * GDN_HEADS, sum3(_mm(cum_b, g3)), _sigmoid(slab)))

    for hh in range(nh):
        lanes = pl.ds(hh * HEAD_DIM, HEAD_DIM)
        _gdn_prepare(hp * nh + hh, cwq_ref.at[:, lanes], cwk_ref.at[:, lanes], cwv_ref.at[:, lanes],
                     q_ref.at[hh], k_ref.at[hh], v_ref.at[hh],
                     xp, qn, kn, vn, act, gcf, bcf,
                     mq_s.at[hh], c_s.at[hh], dec_s.at[hh], of_s.at[hh], ob_s.at[hh])

    def phase2(step, states):
        new = []
        for idx, st in enumerate(states):
            hh, d = divmod(idx, 2)
            n = step if d == 0 else nchunk - 1 - step
            r = _mm(mq_s[hh, d, n], st.astype(BF16))
            rows = pl.ds(pl.multiple_of(n * CHUNK, CHUNK), CHUNK)
            out_s = of_s if d == 0 else ob_s
            out_s[hh, rows, :] = out_s[hh, rows, :] + r[HEAD_DIM:]
            new.append(st * dec_s[hh, d, n] + r[:HEAD_DIM] + c_s[hh, d, n])
        return tuple(new)

    zero_state = jnp.zeros((HEAD_DIM, HEAD_DIM), F32)
    lax.fori_loop(0, nchunk, phase2, (zero_state,) * (2 * nh))

    for hh in range(nh):
        for c in range(s // rc):
            rows = pl.ds(c * rc, rc)
            o = of_s[hh, rows, :] + ob_s[hh, rows, :]
            z = z_ref[hh, rows, :].astype(F32)
            o_ref[hh, rows, :] = (_rms(o, na_ref[...]) * _silu(z)).astype(o_ref.dtype)


def _gdn_prepare(h, cwq_ref, cwk_ref, cwv_ref, q_ref, k_ref, v_ref,
                 xp, qn, kn, vn, act, gcf, bcf, mq_s, c_s, dec_s, of_s, ob_s):
    s = q_ref.shape[0]
    nchunk = s // CHUNK
    rc = 256

    zeros8 = jnp.zeros((8, HEAD_DIM), F32)
    xp[pl.ds(0, 8), :] = zeros8
    xp[pl.ds(8 + s, 8), :] = zeros8

    def conv_into(src_ref, cw_ref, dst_ref, l2, scale):
        xp[pl.ds(8, s), :] = src_ref[...].astype(F32)
        for c in range(s // rc):
            acc = None
            for j in range(CONV_K):
                t = xp[pl.ds(c * rc + 8 - CONV_K // 2 + j, rc), :] * cw_ref[j:j + 1, :]
                acc = t if acc is None else acc + t
            y = _silu(acc)
            if l2:
                y = y * lax.rsqrt(jnp.sum(y * y, axis=-1, keepdims=True) + EPS)
                if scale != 1.0:
                    y = y * scale
            dst_ref[pl.ds(c * rc, rc), :] = y

    conv_into(q_ref, cwq_ref, qn, True, HEAD_DIM ** -0.5)
    conv_into(k_ref, cwk_ref, kn, True, 1.0)
    conv_into(v_ref, cwv_ref, vn, False, 1.0)

    sel_k = lax.broadcasted_iota(jnp.int32, (LANES, 4 * LANES), 0)
    sel_c = lax.broadcasted_iota(jnp.int32, (LANES, 4 * LANES), 1)
    sel = jnp.where(sel_k == (sel_c >> 7) * GDN_HEADS + h, 1.0, 0.0).astype(BF16)
    sel3 = jnp.concatenate([sel, sel, sel], axis=0)
    for c in range(s // rc):
        rows = pl.ds(c * rc, rc)
        picked = _mm(_split3_lanes(act[rows, :]), sel3)
        for d in range(2):
            gcf[d, rows, :] = picked[:, d * LANES:(d + 1) * LANES]
            bcf[d, rows, :] = picked[:, (2 + d) * LANES:(3 + d) * LANES]
    ii = lax.broadcasted_iota(jnp.int32, (CHUNK, 2 * CHUNK), 0)
    lj = lax.broadcasted_iota(jnp.int32, (CHUNK, 2 * CHUNK), 1)
    lo = lj < CHUNK
    hi = lj >= CHUNK
    jj = jnp.where(lo, lj, lj - CHUNK)
    is_diag = ii == jj
    tri2 = (lo & (ii >= jj)) | (hi & (ii <= jj))
    strict2 = (lo & (ii > jj)) | (hi & (ii < jj))
    pair = (lo, hi)
    lo_t = lax.broadcasted_iota(jnp.int32, (2 * CHUNK, 2 * CHUNK), 1) < CHUNK
    pair_t = (lo_t, ~lo_t)
    eye2 = jnp.where(is_diag, 1.0, 0.0)
    blk_bits = range(int(math.log2(TRI_BASE)), int(math.log2(CHUNK)))
    same_blk = tuple((ii >> sh) == (jj >> sh) for sh in blk_bits)

    def phase1(grp, carry):
        ns = [grp * CHUNK_GROUP + i for i in range(CHUNK_GROUP)]
        rws = [pl.ds(pl.multiple_of(n * CHUNK, CHUNK), CHUNK) for n in ns]
        qv = [qn[r, :] for r in rws]
        kv = [kn[r, :] for r in rws]
        kb = [k.astype(BF16) for k in kv]
        grams = [_mm_nt(jnp.concatenate([kb_, q.astype(BF16)], axis=0), jnp.concatenate([kb_, kb_], axis=0))
                 for kb_, q in zip(kb, qv)]
        gcc = [(gcf[0, r, :], gcf[1, r, :]) for r in rws]
        bcc = [(bcf[0, r, :], bcf[1, r, :]) for r in rws]
        gcc2 = [jnp.where(lo, gf, gb) for gf, gb in gcc]
        gcr2 = [jnp.sum(jnp.where(is_diag, g2, 0.0), axis=0, keepdims=True) for g2 in gcc2]
        gams = [jnp.where(tri2, jnp.exp(jnp.where(tri2, g2 - gr, 0.0)), 0.0) for g2, gr in zip(gcc2, gcr2)]
        a2s = [jnp.where(strict2, gram[:CHUNK] * jnp.where(lo, bf, bb) * gam, 0.0)
               for gram, (bf, bb), gam in zip(grams, bcc, gams)]
        t2s = _unit_tri_inverse_pairs(a2s, eye2, pair, same_blk)
        uws = []
        for i, r in enumerate(rws):
            v = vn[r, :]
            (gf, gb), (bf, bb) = gcc[i], bcc[i]
            rhs = jnp.concatenate([jnp.concatenate([v * bf, kv[i] * (bf * jnp.exp(gf))], axis=1),
                                   jnp.concatenate([v * bb, kv[i] * (bb * jnp.exp(gb))], axis=1)], axis=0)
            uws.append(_mm(_block_diag(t2s[i], pair).astype(BF16), rhs.astype(BF16)))
        fins = []
        for i in range(CHUNK_GROUP):
            gf, gb = gcc[i]
            gtot_f = gcr2[i][:, CHUNK - 1:CHUNK]
            gtot_b = gcr2[i][:, CHUNK:CHUNK + 1]
            kg = jnp.concatenate([kv[i] * jnp.exp(gtot_f - gf), kv[i] * jnp.exp(gtot_b - gb)], axis=0)
            lhs = jnp.concatenate([_block_diag(kg.T, pair_t), _block_diag(grams[i][CHUNK:] * gams[i], pair)], axis=0)
            fins.append(_mm(lhs.astype(BF16), uws[i].astype(BF16)))
            dec_s[0, ns[i]] = jnp.broadcast_to(jnp.exp(gtot_f), (1, HEAD_DIM))
            dec_s[1, ns[i]] = jnp.broadcast_to(jnp.exp(gtot_b), (1, HEAD_DIM))
        for i, (n, r) in enumerate(zip(ns, rws)):
            fin = fins[i]
            for d in range(2):
                top = fin[d * HEAD_DIM:(d + 1) * HEAD_DIM]
                bot = fin[2 * HEAD_DIM + d * CHUNK:2 * HEAD_DIM + (d + 1) * CHUNK]
                c_s[d, n] = top[:, :HEAD_DIM]
                mq_s[d, n, pl.ds(0, HEAD_DIM), :] = (-top[:, HEAD_DIM:]).astype(BF16)
                mq_s[d, n, pl.ds(HEAD_DIM, CHUNK), :] = (qv[i] * jnp.exp(gcc[i][d]) - bot[:, HEAD_DIM:]).astype(BF16)
                (of_s if d == 0 else ob_s)[r, :] = bot[:, :HEAD_DIM]
        return carry

    lax.fori_loop(0, nchunk // CHUNK_GROUP, phase1, 0)


def _gdn(proj, small, lanepar, conv_w, norm_a):
    b, _, s, _ = proj.shape
    nchunk = s // CHUNK
    nh = GDN_HEADS_PER_STEP
    assert 2 * CHUNK == LANES == HEAD_DIM and nchunk % CHUNK_GROUP == 0 and GDN_HEADS % nh == 0
    assert all(cb % nh == 0 for cb in (CB_QA, CB_KA, CB_VA, CB_ZA))
    col = lambda cb: pl.BlockSpec((None, nh, s, HEAD_DIM), lambda i, h, cb=cb: (i, cb // nh + h, 0, 0))
    cw = lambda cb: pl.BlockSpec((CONV_K, nh * HEAD_DIM), lambda i, h, cb=cb: (0, cb // nh + h))
    return pl.pallas_call(
        _gdn_body,
        grid=(b, GDN_HEADS // nh),
        in_specs=[pl.BlockSpec((8, LANES), lambda i, h: (0, 0)),
                  cw(0), cw(GDN_HEADS), cw(2 * GDN_HEADS),
                  col(CB_QA), col(CB_KA), col(CB_VA), col(CB_ZA),
                  pl.BlockSpec((None, s, LANES), lambda i, h: (i, 0, 0)),
                  pl.BlockSpec((1, HEAD_DIM), lambda i, h: (0, 0))],
        out_specs=pl.BlockSpec((None, nh, s, HEAD_DIM), lambda i, h: (i, h, 0, 0)),
        out_shape=jax.ShapeDtypeStruct((b, GDN_HEADS, s, HEAD_DIM), BF16),
        scratch_shapes=[
            pltpu.VMEM((s + 16, HEAD_DIM), F32),
            pltpu.VMEM((s, HEAD_DIM), F32),
            pltpu.VMEM((s, HEAD_DIM), F32),
            pltpu.VMEM((s, HEAD_DIM), F32),
            pltpu.VMEM((s, LANES), F32),
            pltpu.VMEM((2, s, LANES), F32),
            pltpu.VMEM((2, s, LANES), F32),
            pltpu.VMEM((nh, 2, nchunk, HEAD_DIM + CHUNK, HEAD_DIM), BF16),
            pltpu.VMEM((nh, 2, nchunk, HEAD_DIM, HEAD_DIM), F32),
            pltpu.VMEM((nh, 2, nchunk, 1, HEAD_DIM), F32),
            pltpu.VMEM((nh, s, HEAD_DIM), F32),
            pltpu.VMEM((nh, s, HEAD_DIM), F32),
        ],
        compiler_params=pltpu.CompilerParams(dimension_semantics=("parallel", "arbitrary"),
                                             vmem_limit_bytes=VMEM_LIMIT),
        name="gdn",
    )(lanepar, conv_w, conv_w, conv_w, proj, proj, proj, proj, small, norm_a)


def _dil_body(q0, k0, v0, q1, k1, v1, q2, k2, v2, bias_ref, o_ref,
              qf, kf, vf, qs, kp, vp, acc_s, m_s, l_s):
    s = q0.shape[0]
    scale = HEAD_DIM ** -0.5
    qi = lax.broadcasted_iota(jnp.int32, (QBLK, KWIN), 0)
    kj = lax.broadcasted_iota(jnp.int32, (QBLK, KWIN), 1)
    off = kj - HALF_WINDOW - qi
    band = (off >= -HALF_WINDOW) & (off <= HALF_WINDOW)
    zpad = jnp.zeros((HALF_WINDOW, HEAD_DIM), F32)

    for g, (q_ref, k_ref, v_ref) in enumerate(((q0, k0, v0), (q1, k1, v1), (q2, k2, v2))):
        dil = DIL_GROUPS[g][1]
        length = s // dil
        nblk = length // QBLK
        qf[...] = q_ref[...].astype(F32)
        kf[...] = k_ref[...].astype(F32)
        vf[...] = v_ref[...].astype(F32)
        seg_q = max(length, QBLK)
        seg_k = seg_q + 2 * HALF_WINDOW
        nseg = min(dil, max(1, ATT_GROUP // nblk))
        bpi = min(nblk, ATT_GROUP)
        for c in range(nseg):
            for buf in (kp, vp):
                buf[pl.ds(c * seg_k, HALF_WINDOW), :] = zpad
                buf[pl.ds(c * seg_k + HALF_WINDOW + length, HALF_WINDOW), :] = zpad
        bias = pltpu.roll(jnp.broadcast_to(bias_ref[g], (QBLK, BIAS_LEN)), 0, 1,
                          stride=1, stride_axis=0)[:, :KWIN]

        def stage(c, r, dil=dil, length=length, seg_q=seg_q, seg_k=seg_k):
            cls = pl.ds(r, length, stride=dil) if dil > 1 else pl.ds(0, length)
            qs[pl.ds(c * seg_q, length), :] = qf[cls, :]
            kp[pl.ds(c * seg_k + HALF_WINDOW, length), :] = kf[cls, :]
            vp[pl.ds(c * seg_k + HALF_WINDOW, length), :] = vf[cls, :]

        def attend(items, g=g, dil=dil, length=length, bias=bias):
            logits = []
            for q_off, k_off, j0, r in items:
                kpos = j0 - HALF_WINDOW + kj
                ok = band & (kpos >= 0) & (kpos < length)
                sc = _mm_nt(qs[pl.ds(q_off, QBLK), :].astype(BF16), kp[pl.ds(k_off, KWIN), :].astype(BF16))
                logits.append(jnp.where(ok, sc * scale + bias, NEG))
            ms = [jnp.max(lg, axis=-1, keepdims=True) for lg in logits]
            ps = [jnp.exp(lg - m) for lg, m in zip(logits, ms)]
            ls = [jnp.sum(p, axis=-1, keepdims=True) for p in ps]
            accs = [_mm(p.astype(BF16), vp[pl.ds(it[1], KWIN), :].astype(BF16)) for p, it in zip(ps, items)]
            for (q_off, k_off, j0, r), m, lsum, acc in zip(items, ms, ls, accs):
                tok = pl.ds(r + dil * j0, QBLK, stride=dil) if dil > 1 else pl.ds(j0, QBLK)
                acc_s[g, tok, :] = acc
                m_s[g, tok, :] = m
                l_s[g, tok, :] = lsum

        def classes(it, carry, nblk=nblk, nseg=nseg, bpi=bpi, seg_q=seg_q, seg_k=seg_k,
                    stage=stage, attend=attend):
            rs = [it * nseg + c for c in range(nseg)]
            for c, r in enumerate(rs):
                stage(c, r)

            def blocks(jt, carry2):
                items = []
                for c, r in enumerate(rs):
                    for i in range(bpi):
                        j0 = (jt * bpi + i) * QBLK
                        if not isinstance(j0, int):
                            j0 = pl.multiple_of(j0, QBLK)
                        items.append((c * seg_q + j0, c * seg_k + j0, j0, r))
                attend(items)
                return carry2

            if nblk == bpi:
                blocks(0, 0)
            else:
                lax.fori_loop(0, nblk // bpi, blocks, 0)
            return carry

        lax.fori_loop(0, dil // nseg, classes, 0)

    rc = 256
    for c in range(s // rc):
        rows = pl.ds(c * rc, rc)
        ms = [m_s[g, rows, :] for g in range(3)]
        mx = jnp.maximum(jnp.maximum(ms[0], ms[1]), ms[2])
        ws = [jnp.exp(ms[g] - mx) for g in range(3)]
        den = ws[0] * l_s[0, rows, :] + ws[1] * l_s[1, rows, :] + ws[2] * l_s[2, rows, :]
        num = ws[0] * acc_s[0, rows, :] + ws[1] * acc_s[1, rows, :] + ws[2] * acc_s[2, rows, :]
        o_ref[rows, :] = (num / den).astype(o_ref.dtype)


def _dilated(proj, bias):
    b, _, s, _ = proj.shape
    specs = []
    for g in range(3):
        for cb in (CB_QB, CB_KB, CB_VB):
            specs.append(pl.BlockSpec((None, None, s, HEAD_DIM),
                                      lambda i, h, cb=cb, g=g: (i, cb + g * DIL_HEADS_PER_GROUP + h, 0, 0)))
    specs.append(pl.BlockSpec((None, 3, 1, BIAS_LEN), lambda i, h: (h, 0, 0, 0)))
    return pl.pallas_call(
        _dil_body,
        grid=(b, DIL_HEADS_PER_GROUP),
        in_specs=specs,
        out_specs=pl.BlockSpec((None, None, s, HEAD_DIM), lambda i, h: (i, h, 0, 0)),
        out_shape=jax.ShapeDtypeStruct((b, DIL_HEADS_PER_GROUP, s, HEAD_DIM), BF16),
        scratch_shapes=[
            pltpu.VMEM((s, HEAD_DIM), F32),
            pltpu.VMEM((s, HEAD_DIM), F32),
            pltpu.VMEM((s, HEAD_DIM), F32),
            pltpu.VMEM((s, HEAD_DIM), F32),
            pltpu.VMEM((s + 2 * HALF_WINDOW, HEAD_DIM), F32),
            pltpu.VMEM((s + 2 * HALF_WINDOW, HEAD_DIM), F32),
            pltpu.VMEM((3, s, HEAD_DIM), F32),
            pltpu.VMEM((3, s, 1), F32),
            pltpu.VMEM((3, s, 1), F32),
        ],
        compiler_params=pltpu.CompilerParams(dimension_semantics=("parallel", "arbitrary"),
                                             vmem_limit_bytes=VMEM_LIMIT),
        name="dilated_attn",
    )(*([proj] * 9), bias)


def _mix_out_body(x_ref, oa_ref, ob_ref, ga_ref, gb_ref, wa_ref, wb_ref, wo_ref, ln_ref, o_ref):
    wide = lambda ref: jnp.concatenate([ref[c] for c in range(ref.shape[0])], axis=1)
    ya = _mm(wide(oa_ref), wa_ref[...])
    yb = _mm(wide(ob_ref), wb_ref[...])
    merged = _sigmoid(wide(ga_ref).astype(F32)) * ya + _sigmoid(wide(gb_ref).astype(F32)) * yb
    y = _mm(merged.astype(BF16), wo_ref[...])
    o_ref[...] = x_ref[...] + _rms(y, ln_ref[...])


def _mix_out(x3d, o_a, o_b, proj, w_a, w_b, w_o, ln_post, tm):
    b, s, d = x3d.shape
    nslab = d // LANES
    assert CB_GA % nslab == 0 and CB_GB % nslab == 0 and s % tm == 0
    const = lambda shape: pl.BlockSpec(shape, lambda i, j: (0, 0))
    slabs = lambda n, blk: pl.BlockSpec((None, n, tm, LANES), lambda i, j, blk=blk: (i, blk, j, 0))
    return pl.pallas_call(
        _mix_out_body,
        grid=(b, s // tm),
        in_specs=[pl.BlockSpec((None, tm, d), lambda i, j: (i, j, 0)),
                  slabs(GDN_HEADS, 0), slabs(DIL_HEADS_PER_GROUP, 0),
                  slabs(nslab, CB_GA // nslab), slabs(nslab, CB_GB // nslab),
                  const((GDN_WIDTH, d)), const((DIL_OUT_WIDTH, d)), const((d, d)), const((1, d))],
        out_specs=pl.BlockSpec((None, tm, d), lambda i, j: (i, j, 0)),
        out_shape=jax.ShapeDtypeStruct((b, s, d), F32),
        compiler_params=pltpu.CompilerParams(dimension_semantics=("parallel", "parallel"),
                                             vmem_limit_bytes=VMEM_LIMIT),
        name="mix_out",
    )(x3d, o_a, o_b, proj, proj, w_a, w_b, w_o, ln_post)


def _mlp_body(tf, x_ref, g1_ref, w1_ref, w2_ref, g2_ref, o_ref):
    x = x_ref[...]
    h = _rms(x, g1_ref[...]).astype(BF16)
    acc = None
    for k in range(w1_ref.shape[1] // tf):
        cols = pl.ds(k * tf, tf)
        f = jnp.maximum(_mm(h, w1_ref[:, cols]), 0.0)
        part = _mm((f * f).astype(BF16), w2_ref[cols, :])
        acc = part if acc is None else acc + part
    o_ref[...] = x + _rms(acc, g2_ref[...])


def _mlp(x2d, ln_pre, w1, w2, ln_post, tm, tf):
    t, d = x2d.shape
    dff = w1.shape[1]
    resident = lambda shape: pl.BlockSpec(shape, lambda i: (0, 0), pipeline_mode=pl.Buffered(1))
    return pl.pallas_call(
        functools.partial(_mlp_body, tf),
        grid=(t // tm,),
        in_specs=[pl.BlockSpec((tm, d), lambda i: (i, 0)),
                  resident((1, d)), resident((d, dff)), resident((dff, d)), resident((1, d))],
        out_specs=pl.BlockSpec((tm, d), lambda i: (i, 0)),
        out_shape=jax.ShapeDtypeStruct((t, d), F32),
        compiler_params=pltpu.CompilerParams(dimension_semantics=("parallel",),
                                             vmem_limit_bytes=VMEM_LIMIT),
        name="mlp",
    )(x2d, ln_pre, w1, w2, ln_post)


def _t5_bucket_np(rel):
    nb = REL_BUCKETS // 2
    ret = (rel > 0).astype(np.int32) * nb
    n = np.abs(rel)
    max_exact = nb // 2
    large = max_exact + (np.log(np.maximum(n, 1) / max_exact) / math.log(REL_MAX_DIST / max_exact)
                         * (nb - max_exact)).astype(np.int32)
    large = np.minimum(large, nb - 1)
    return ret + np.where(n < max_exact, n, large).astype(np.int32)


def _attention_bias(rel_bias):
    c = np.arange(BIAS_LEN)
    off = np.where(c < KWIN, np.clip(c - HALF_WINDOW, -HALF_WINDOW, HALF_WINDOW), -HALF_WINDOW)
    per_group = []
    for gi, (_, dil) in enumerate(DIL_GROUPS):
        bt = rel_bias[_t5_bucket_np(off * dil)]
        per_group.append(bt[:, gi * DIL_HEADS_PER_GROUP:(gi + 1) * DIL_HEADS_PER_GROUP])
    return jnp.transpose(jnp.stack(per_group, axis=0), (2, 0, 1))[:, :, None, :].astype(F32)


def _gdn_params(a_log_f, a_log_b, dt_bias_f, dt_bias_b):
    pad = lambda f, bk: jnp.pad(jnp.concatenate([f, bk]), (0, LANES - 2 * GDN_HEADS))
    return jnp.stack([pad(a_log_f, a_log_b), pad(dt_bias_f, dt_bias_b)]
                     + [jnp.zeros((LANES,), F32)] * 6, axis=0).astype(F32)


def kernel(x, rel_bias, ln_mix_pre, w_in, conv_w, a_log_f, a_log_b, dt_bias_f, dt_bias_b, norm_a,
           w_branch_a, w_branch_b, w_out, ln_mix_post, ln_mlp_pre, w_ff1, w_ff2, ln_mlp_post):
    b, s, d = x.shape
    t = b * s
    c_small = 4 * GDN_WIDTH
    bias = _attention_bias(rel_bias)
    for l in range(ln_mix_pre.shape[0]):
        w = w_in[l].astype(BF16)
        c_qb = c_small + N_SMALL
        c_ga = c_qb + 3 * DIL_WIDTH
        w_wide = jnp.concatenate([w[:, :c_small], w[:, c_ga:], w[:, c_qb:c_ga]], axis=1)
        w_small = jnp.pad(w[:, c_small:c_qb], ((0, 0), (0, LANES - N_SMALL)))
        proj, small = _in_proj(x, ln_mix_pre[l][None, :], w_wide, w_small, 1536)

        lanepar = _gdn_params(a_log_f[l], a_log_b[l], dt_bias_f[l], dt_bias_b[l])
        o_a = _gdn(proj, small, lanepar, conv_w[l].astype(F32), norm_a[l][None, :].astype(F32))
        o_b = _dilated(proj, bias)

        x1 = _mix_out(x, o_a, o_b, proj,
                      w_branch_a[l].astype(BF16), w_branch_b[l].astype(BF16), w_out[l].astype(BF16),
                      ln_mix_post[l][None, :], 512)
        x2d = _mlp(x1.reshape(t, d), ln_mlp_pre[l][None, :], w_ff1[l].astype(BF16), w_ff2[l].astype(BF16),
                   ln_mlp_post[l][None, :], 1024, 1024)
        x = x2d.reshape(b, s, d)
    return x
```

```python
import functools
import math

import numpy as np
import jax
import jax.numpy as jnp
from jax import lax
from jax.experimental import pallas as pl
from jax.experimental.pallas import tpu as pltpu

F32 = jnp.float32
BF16 = jnp.bfloat16
HIGHEST = lax.Precision.HIGHEST

D_MODEL = 1024
GDN_HEADS = 8
HEAD_DIM = 128
GDN_WIDTH = GDN_HEADS * HEAD_DIM
CHUNK = 64
TRI_BASE = 8
CHUNK_GROUP = 16
GDN_HEADS_PER_STEP = 2
ATT_GROUP = 8
CONV_K = 5
DIL_GROUPS = ((128, 1), (512, 4), (2048, 16))
DIL_HEADS_PER_GROUP = 4
DIL_HEADS = 12
DIL_WIDTH = DIL_HEADS * HEAD_DIM
DIL_OUT_WIDTH = DIL_HEADS_PER_GROUP * HEAD_DIM
HALF_WINDOW = 64
QBLK = 128
KWIN = QBLK + 2 * HALF_WINDOW
REL_BUCKETS = 32
REL_MAX_DIST = 1024
D_FF = 4 * D_MODEL
EPS = 1e-6
NEG = -1e30
N_SMALL = 4 * GDN_HEADS
LANES = 128
BIAS_LEN = 3 * LANES

CB_QA, CB_KA, CB_VA, CB_ZA = 0, 8, 16, 24
CB_GA, CB_GB = 32, 40
CB_QB, CB_KB, CB_VB = 48, 60, 72
N_WIDE = 84 * LANES

VMEM_LIMIT = 56 * 1024 * 1024


def _mm(a, b):
    return jnp.dot(a, b, preferred_element_type=F32)


def _mm_nt(a, b):
    return lax.dot_general(a, b, (((1,), (1,)), ((), ())), preferred_element_type=F32)


def _mm_hi(a, b):
    return jnp.dot(a, b, preferred_element_type=F32, precision=HIGHEST)


def _rms(x, gain):
    return x * lax.rsqrt(jnp.mean(x * x, axis=-1, keepdims=True) + EPS) * gain


def _sigmoid(x):
    return 1.0 / (1.0 + jnp.exp(-x))


def _silu(x):
    h = 0.5 * x
    return h + h * jnp.tanh(h)


def _softplus(x):
    return jnp.maximum(x, 0.0) + jnp.log(1.0 + jnp.exp(-jnp.abs(x)))


def _in_proj_body(x_ref, g_ref, w_ref, ws_ref, o_ref, small_ref, h_ref):
    @pl.when(pl.program_id(1) == 0)
    def _():
        h_ref[...] = _rms(x_ref[...], g_ref[...]).astype(BF16)
        small_ref[...] = _mm(h_ref[...], ws_ref[...])

    h = h_ref[...]
    for c in range(0, o_ref.shape[0], 2):
        res = _mm(h, w_ref[:, c * LANES:(c + 2) * LANES])
        o_ref[c] = res[:, :LANES].astype(o_ref.dtype)
        o_ref[c + 1] = res[:, LANES:].astype(o_ref.dtype)


def _in_proj(x3d, gain, w, w_small, tn):
    b, s, d = x3d.shape
    n = w.shape[1]
    return pl.pallas_call(
        _in_proj_body,
        grid=(b, n // tn),
        in_specs=[pl.BlockSpec((None, s, d), lambda i, j: (i, 0, 0)),
                  pl.BlockSpec((1, d), lambda i, j: (0, 0)),
                  pl.BlockSpec((d, tn), lambda i, j: (0, j)),
                  pl.BlockSpec((d, LANES), lambda i, j: (0, 0))],
        out_specs=[pl.BlockSpec((None, tn // LANES, s, LANES), lambda i, j: (i, j, 0, 0)),
                   pl.BlockSpec((None, s, LANES), lambda i, j: (i, 0, 0))],
        out_shape=[jax.ShapeDtypeStruct((b, n // LANES, s, LANES), BF16),
                   jax.ShapeDtypeStruct((b, s, LANES), F32)],
        scratch_shapes=[pltpu.VMEM((s, d), BF16)],
        compiler_params=pltpu.CompilerParams(dimension_semantics=("parallel", "arbitrary"),
                                             vmem_limit_bytes=VMEM_LIMIT),
        name="in_proj",
    )(x3d, gain, w, w_small)


def _split_bf16(x):
    hi = x.astype(BF16)
    return hi, (x - hi.astype(F32)).astype(BF16)


def _split3_lanes(x):
    hi = x.astype(BF16)
    r = x - hi.astype(F32)
    mid = r.astype(BF16)
    lo = (r - mid.astype(F32)).astype(BF16)
    return jnp.concatenate([hi, mid, lo], axis=1)


def _block_diag(x, groups):
    return jnp.concatenate([jnp.where(g, x, 0.0) for g in groups], axis=0)


def _mm_parts(lhs, rhs, masks):
    (lh, ll), (rh, rl) = lhs, rhs
    bd = lambda p: jnp.concatenate([p * m for m in masks], axis=0)
    rh_bd = bd(rh)
    return (_mm(jnp.concatenate([lh, ll], axis=1), jnp.concatenate([rh_bd, rh_bd], axis=0))
            + _mm(lh, bd(rl)))


def _unit_tri_inverse_pairs(a2s, eye, lo, same_blk):
    masks = [jnp.where(g, 1.0, 0.0).astype(BF16) for g in lo]
    stack = lambda a, b: tuple(jnp.concatenate([p, q], axis=0) for p, q in zip(a, b))
    ds = [jnp.where(same_blk[0], a2, 0.0) for a2 in a2s]
    dps = [_split_bf16(d) for d in ds]
    xs = [_mm_parts(dp, dp, masks) for dp in dps]
    ts = [eye - d for d in ds]
    levels = int(math.log2(TRI_BASE)) - 1
    for lvl in range(levels):
        tps = [_split_bf16(t) for t in ts]
        xps = [_split_bf16(x) for x in xs]
        if lvl < levels - 1:
            outs = [_mm_parts(stack(tp, xp), xp, masks) for tp, xp in zip(tps, xps)]
            ts = [t + out[:CHUNK] for t, out in zip(ts, outs)]
            xs = [out[CHUNK:] for out in outs]
        else:
            ts = [t + _mm_parts(tp, xp, masks) for t, tp, xp in zip(ts, tps, xps)]
    inner = same_blk[0]
    for outer in tuple(same_blk[1:]) + (None,):
        between = [jnp.where(inner, 0.0, a2) if outer is None else jnp.where(outer & ~inner, a2, 0.0)
                   for a2 in a2s]
        tps = [_split_bf16(t) for t in ts]
        tls = [_mm_parts(tp, _split_bf16(l), masks) for tp, l in zip(tps, between)]
        ts = [t - _mm_parts(_split_bf16(tl), tp, masks) for t, tl, tp in zip(ts, tls, tps)]
        inner = outer
    return ts


def _gdn_body(lanepar_ref, cwq_ref, cwk_ref, cwv_ref, q_ref, k_ref, v_ref, z_ref, small_ref,
              na_ref, o_ref,
              xp, qn, kn, vn, act, gcf, bcf, mq_s, c_s, dec_s, of_s, ob_s):
    nh, s, _ = q_ref.shape
    nchunk = s // CHUNK
    hp = pl.program_id(1)
    rc = 256

    @pl.when(hp == 0)
    def _():
        lane = lax.broadcasted_iota(jnp.int32, (rc, LANES), 1)
        bi = lax.broadcasted_iota(jnp.int32, (rc, rc), 0)
        bj = lax.broadcasted_iota(jnp.int32, (rc, rc), 1)
        same_chunk = jnp.bitwise_xor(bi, bj) < CHUNK
        cum_f = jnp.where(same_chunk & (bj <= bi), 1.0, 0.0).astype(BF16)
        cum_b = jnp.where(same_chunk & (bj >= bi), 1.0, 0.0).astype(BF16)
        neg_a = -jnp.exp(lanepar_ref[0:1, :])
        dtb = lanepar_ref[1:2, :]
        sum3 = lambda y: y[:, :LANES] + y[:, LANES:2 * LANES] + y[:, 2 * LANES:]
        for c in range(s // rc):
            rows = pl.ds(c * rc, rc)
            slab = small_ref[rows, :]
            g3 = _split3_lanes(neg_a * _softplus(slab + dtb))
            act[rows, :] = jnp.where(lane < GDN_HEADS, sum3(_mm(cum_f, g3)),
                                     jnp.where(lane < 2 * GDN_HEADS, sum3(_mm(cum_b, g3)), _sigmoid(slab)))

    for hh in range(nh):
        lanes = pl.ds(hh * HEAD_DIM, HEAD_DIM)
        _gdn_prepare(hp * nh + hh, cwq_ref.at[:, lanes], cwk_ref.at[:, lanes], cwv_ref.at[:, lanes],
                     q_ref.at[hh], k_ref.at[hh], v_ref.at[hh],
                     xp, qn, kn, vn, act, gcf, bcf,
                     mq_s.at[hh], c_s.at[hh], dec_s.at[hh], of_s.at[hh], ob_s.at[hh])

    def phase2(step, states):
        new = []
        for idx, st in enumerate(states):
            hh, d = divmod(idx, 2)
            n = step if d == 0 else nchunk - 1 - step
            r = _mm(mq_s[hh, d, n], st.astype(BF16))
            rows = pl.ds(pl.multiple_of(n * CHUNK, CHUNK), CHUNK)
            out_s = of_s if d == 0 else ob_s
            out_s[hh, rows, :] = out_s[hh, rows, :] + r[HEAD_DIM:]
            new.append(st * dec_s[hh, d, n] + r[:HEAD_DIM] + c_s[hh, d, n])
        return tuple(new)

    zero_state = jnp.zeros((HEAD_DIM, HEAD_DIM), F32)
    lax.fori_loop(0, nchunk, phase2, (zero_state,) * (2 * nh))

    for hh in range(nh):
        for c in range(s // rc):
            rows = pl.ds(c * rc, rc)
            o = of_s[hh, rows, :] + ob_s[hh, rows, :]
            z = z_ref[hh, rows, :].astype(F32)
            o_ref[hh, rows, :] = (_rms(o, na_ref[...]) * _silu(z)).astype(o_ref.dtype)


def _gdn_prepare(h, cwq_ref, cwk_ref, cwv_ref, q_ref, k_ref, v_ref,
                 xp, qn, kn, vn, act, gcf, bcf, mq_s, c_s, dec_s, of_s, ob_s):
    s = q_ref.shape[0]
    nchunk = s // CHUNK
    rc = 256

    zeros8 = jnp.zeros((8, HEAD_DIM), F32)
    xp[pl.ds(0, 8), :] = zeros8
    xp[pl.ds(8 + s, 8), :] = zeros8

    def conv_into(src_ref, cw_ref, dst_ref, l2, scale):
        xp[pl.ds(8, s), :] = src_ref[...].astype(F32)
        for c in range(s // rc):
            acc = None
            for j in range(CONV_K):
                t = xp[pl.ds(c * rc + 8 - CONV_K // 2 + j, rc), :] * cw_ref[j:j + 1, :]
                acc = t if acc is None else acc + t
            y = _silu(acc)
            if l2:
                y = y * lax.rsqrt(jnp.sum(y * y, axis=-1, keepdims=True) + EPS)
                if scale != 1.0:
                    y = y * scale
            dst_ref[pl.ds(c * rc, rc), :] = y

    conv_into(q_ref, cwq_ref, qn, True, HEAD_DIM ** -0.5)
    conv_into(k_ref, cwk_ref, kn, True, 1.0)
    conv_into(v_ref, cwv_ref, vn, False, 1.0)

    sel_k = lax.broadcasted_iota(jnp.int32, (LANES, 4 * LANES), 0)
    sel_c = lax.broadcasted_iota(jnp.int32, (LANES, 4 * LANES), 1)
    sel = jnp.where(sel_k == (sel_c >> 7) * GDN_HEADS + h, 1.0, 0.0).astype(BF16)
    sel3 = jnp.concatenate([sel, sel, sel], axis=0)
    for c in range(s // rc):
        rows = pl.ds(c * rc, rc)
        picked = _mm(_split3_lanes(act[rows, :]), sel3)
        for d in range(2):
            gcf[d, rows, :] = picked[:, d * LANES:(d + 1) * LANES]
            bcf[d, rows, :] = picked[:, (2 + d) * LANES:(3 + d) * LANES]
    ii = lax.broadcasted_iota(jnp.int32, (CHUNK, 2 * CHUNK), 0)
    lj = lax.broadcasted_iota(jnp.int32, (CHUNK, 2 * CHUNK), 1)
    lo = lj < CHUNK
    hi = lj >= CHUNK
    jj = jnp.where(lo, lj, lj - CHUNK)
    is_diag = ii == jj
    tri2 = (lo & (ii >= jj)) | (hi & (ii <= jj))
    strict2 = (lo & (ii > jj)) | (hi & (ii < jj))
    pair = (lo, hi)
    lo_t = lax.broadcasted_iota(jnp.int32, (2 * CHUNK, 2 * CHUNK), 1) < CHUNK
    pair_t = (lo_t, ~lo_t)
    eye2 = jnp.where(is_diag, 1.0, 0.0)
    blk_bits = range(int(math.log2(TRI_BASE)), int(math.log2(CHUNK)))
    same_blk = tuple((ii >> sh) == (jj >> sh) for sh in blk_bits)

    def phase1(grp, carry):
        ns = [grp * CHUNK_GROUP + i for i in range(CHUNK_GROUP)]
        rws = [pl.ds(pl.multiple_of(n * CHUNK, CHUNK), CHUNK) for n in ns]
        qv = [qn[r, :] for r in rws]
        kv = [kn[r, :] for r in rws]
        kb = [k.astype(BF16) for k in kv]
        grams = [_mm_nt(jnp.concatenate([kb_, q.astype(BF16)], axis=0), jnp.concatenate([kb_, kb_], axis=0))
                 for kb_, q in zip(kb, qv)]
        gcc = [(gcf[0, r, :], gcf[1, r, :]) for r in rws]
        bcc = [(bcf[0, r, :], bcf[1, r, :]) for r in rws]
        gcc2 = [jnp.where(lo, gf, gb) for gf, gb in gcc]
        gcr2 = [jnp.sum(jnp.where(is_diag, g2, 0.0), axis=0, keepdims=True) for g2 in gcc2]
        gams = [jnp.where(tri2, jnp.exp(jnp.where(tri2, g2 - gr, 0.0)), 0.0) for g2, gr in zip(gcc2, gcr2)]
        a2s = [jnp.where(strict2, gram[:CHUNK] * jnp.where(lo, bf, bb) * gam, 0.0)
               for gram, (bf, bb), gam in zip(grams, bcc, gams)]
        t2s = _unit_tri_inverse_pairs(a2s, eye2, pair, same_blk)
        uws = []
        for i, r in enumerate(rws):
            v = vn[r, :]
            (gf, gb), (bf, bb) = gcc[i], bcc[i]
            rhs = jnp.concatenate([jnp.concatenate([v * bf, kv[i] * (bf * jnp.exp(gf))], axis=1),
                                   jnp.concatenate([v * bb, kv[i] * (bb * jnp.exp(gb))], axis=1)], axis=0)
            uws.append(_mm(_block_diag(t2s[i], pair).astype(BF16), rhs.astype(BF16)))
        fins = []
        for i in range(CHUNK_GROUP):
            gf, gb = gcc[i]
            gtot_f = gcr2[i][:, CHUNK - 1:CHUNK]
            gtot_b = gcr2[i][:, CHUNK:CHUNK + 1]
            kg = jnp.concatenate([kv[i] * jnp.exp(gtot_f - gf), kv[i] * jnp.exp(gtot_b - gb)], axis=0)
            lhs = jnp.concatenate([_block_diag(kg.T, pair_t), _block_diag(grams[i][CHUNK:] * gams[i], pair)], axis=0)
            fins.append(_mm(lhs.astype(BF16), uws[i].astype(BF16)))
            dec_s[0, ns[i]] = jnp.broadcast_to(jnp.exp(gtot_f), (1, HEAD_DIM))
            dec_s[1, ns[i]] = jnp.broadcast_to(jnp.exp(gtot_b), (1, HEAD_DIM))
        for i, (n, r) in enumerate(zip(ns, rws)):
            fin = fins[i]
            for d in range(2):
                top = fin[d * HEAD_DIM:(d + 1) * HEAD_DIM]
                bot = fin[2 * HEAD_DIM + d * CHUNK:2 * HEAD_DIM + (d + 1) * CHUNK]
                c_s[d, n] = top[:, :HEAD_DIM]
                mq_s[d, n, pl.ds(0, HEAD_DIM), :] = (-top[:, HEAD_DIM:]).astype(BF16)
                mq_s[d, n, pl.ds(HEAD_DIM, CHUNK), :] = (qv[i] * jnp.exp(gcc[i][d]) - bot[:, HEAD_DIM:]).astype(BF16)
                (of_s if d == 0 else ob_s)[r, :] = bot[:, :HEAD_DIM]
        return carry

    lax.fori_loop(0, nchunk // CHUNK_GROUP, phase1, 0)


def _gdn(proj, small, lanepar, conv_w, norm_a):
    b, _, s, _ = proj.shape
    nchunk = s // CHUNK
    nh = GDN_HEADS_PER_STEP
    assert 2 * CHUNK == LANES == HEAD_DIM and nchunk % CHUNK_GROUP == 0 and GDN_HEADS % nh == 0
    assert all(cb % nh == 0 for cb in (CB_QA, CB_KA, CB_VA, CB_ZA))
    col = lambda cb: pl.BlockSpec((None, nh, s, HEAD_DIM), lambda i, h, cb=cb: (i, cb // nh + h, 0, 0))
    cw = lambda cb: pl.BlockSpec((CONV_K, nh * HEAD_DIM), lambda i, h, cb=cb: (0, cb // nh + h))
    return pl.pallas_call(
        _gdn_body,
        grid=(b, GDN_HEADS // nh),
        in_specs=[pl.BlockSpec((8, LANES), lambda i, h: (0, 0)),
                  cw(0), cw(GDN_HEADS), cw(2 * GDN_HEADS),
                  col(CB_QA), col(CB_KA), col(CB_VA), col(CB_ZA),
                  pl.BlockSpec((None, s, LANES), lambda i, h: (i, 0, 0)),
                  pl.BlockSpec((1, HEAD_DIM), lambda i, h: (0, 0))],
        out_specs=pl.BlockSpec((None, nh, s, HEAD_DIM), lambda i, h: (i, h, 0, 0)),
        out_shape=jax.ShapeDtypeStruct((b, GDN_HEADS, s, HEAD_DIM), BF16),
        scratch_shapes=[
            pltpu.VMEM((s + 16, HEAD_DIM), F32),
            pltpu.VMEM((s, HEAD_DIM), F32),
            pltpu.VMEM((s, HEAD_DIM), F32),
            pltpu.VMEM((s, HEAD_DIM), F32),
            pltpu.VMEM((s, LANES), F32),
            pltpu.VMEM((2, s, LANES), F32),
            pltpu.VMEM((2, s, LANES), F32),
            pltpu.VMEM((nh, 2, nchunk, HEAD_DIM + CHUNK, HEAD_DIM), BF16),
            pltpu.VMEM((nh, 2, nchunk, HEAD_DIM, HEAD_DIM), F32),
            pltpu.VMEM((nh, 2, nchunk, 1, HEAD_DIM), F32),
            pltpu.VMEM((nh, s, HEAD_DIM), F32),
            pltpu.VMEM((nh, s, HEAD_DIM), F32),
        ],
        compiler_params=pltpu.CompilerParams(dimension_semantics=("parallel", "arbitrary"),
                                             vmem_limit_bytes=VMEM_LIMIT),
        name="gdn",
    )(lanepar, conv_w, conv_w, conv_w, proj, proj, proj, proj, small, norm_a)


def _dil_body(q0, k0, v0, q1, k1, v1, q2, k2, v2, bias_ref, o_ref,
              qf, kf, vf, qs, kp, vp, kpb, vpb, acc_s, m_s, l_s):
    s = q0.shape[0]
    scale = HEAD_DIM ** -0.5
    qi = lax.broadcasted_iota(jnp.int32, (QBLK, KWIN), 0)
    kj = lax.broadcasted_iota(jnp.int32, (QBLK, KWIN), 1)
    off = kj - HALF_WINDOW - qi
    band = (off >= -HALF_WINDOW) & (off <= HALF_WINDOW)
    zpad = jnp.zeros((HALF_WINDOW, HEAD_DIM), F32)

    for g, (q_ref, k_ref, v_ref) in enumerate(((q0, k0, v0), (q1, k1, v1), (q2, k2, v2))):
        dil = DIL_GROUPS[g][1]
        length = s // dil
        nblk = length // QBLK
        seg_q = max(length, QBLK)
        seg_k = seg_q + 2 * HALF_WINDOW
        nseg = min(dil, max(1, ATT_GROUP // nblk))
        bpi = min(nblk, ATT_GROUP)
        if dil > 1:
            qf[...] = q_ref[...].astype(F32)
            kf[...] = k_ref[...].astype(F32)
            vf[...] = v_ref[...].astype(F32)
            q_src, k_src, v_src = qs, kp, vp
        else:
            q_src, k_src, v_src = q_ref, kpb, vpb
            kpb[pl.ds(HALF_WINDOW, length), :] = k_ref[...]
            vpb[pl.ds(HALF_WINDOW, length), :] = v_ref[...]
        for c in range(nseg):
            for buf in (k_src, v_src):
                buf[pl.ds(c * seg_k, HALF_WINDOW), :] = zpad.astype(buf.dtype)
                buf[pl.ds(c * seg_k + HALF_WINDOW + length, HALF_WINDOW), :] = zpad.astype(buf.dtype)
        bias = pltpu.roll(jnp.broadcast_to(bias_ref[g], (QBLK, BIAS_LEN)), 0, 1,
                          stride=1, stride_axis=0)[:, :KWIN]

        def stage(c, r, dil=dil, length=length, seg_q=seg_q, seg_k=seg_k):
            if dil == 1:
                return
            cls = pl.ds(r, length, stride=dil)
            qs[pl.ds(c * seg_q, length), :] = qf[cls, :]
            kp[pl.ds(c * seg_k + HALF_WINDOW, length), :] = kf[cls, :]
            vp[pl.ds(c * seg_k + HALF_WINDOW, length), :] = vf[cls, :]

        def attend(items, g=g, dil=dil, length=length, bias=bias, q_src=q_src, k_src=k_src, v_src=v_src):
            logits = []
            for q_off, k_off, j0, r in items:
                kpos = j0 - HALF_WINDOW + kj
                ok = band & (kpos >= 0) & (kpos < length)
                sc = _mm_nt(q_src[pl.ds(q_off, QBLK), :].astype(BF16), k_src[pl.ds(k_off, KWIN), :].astype(BF16))
                logits.append(jnp.where(ok, sc * scale + bias, NEG))
            ms = [jnp.max(lg, axis=-1, keepdims=True) for lg in logits]
            ps = [jnp.exp(lg - m) for lg, m in zip(logits, ms)]
            ls = [jnp.sum(p, axis=-1, keepdims=True) for p in ps]
            accs = [_mm(p.astype(BF16), v_src[pl.ds(it[1], KWIN), :].astype(BF16)) for p, it in zip(ps, items)]
            for (q_off, k_off, j0, r), m, lsum, acc in zip(items, ms, ls, accs):
                tok = pl.ds(r + dil * j0, QBLK, stride=dil) if dil > 1 else pl.ds(j0, QBLK)
                acc_s[g, tok, :] = acc
                m_s[g, tok, :] = m
                l_s[g, tok, :] = lsum

        def classes(it, carry, nblk=nblk, nseg=nseg, bpi=bpi, seg_q=seg_q, seg_k=seg_k,
                    stage=stage, attend=attend):
            rs = [it * nseg + c for c in range(nseg)]
            for c, r in enumerate(rs):
                stage(c, r)

            def blocks(jt, carry2):
                items = []
                for c, r in enumerate(rs):
                    for i in range(bpi):
                        j0 = (jt * bpi + i) * QBLK
                        if not isinstance(j0, int):
                            j0 = pl.multiple_of(j0, QBLK)
                        items.append((c * seg_q + j0, c * seg_k + j0, j0, r))
                attend(items)
                return carry2

            if nblk == bpi:
                blocks(0, 0)
            else:
                lax.fori_loop(0, nblk // bpi, blocks, 0)
            return carry

        lax.fori_loop(0, dil // nseg, classes, 0)

    rc = 256
    for c in range(s // rc):
        rows = pl.ds(c * rc, rc)
        ms = [m_s[g, rows, :] for g in range(3)]
        mx = jnp.maximum(jnp.maximum(ms[0], ms[1]), ms[2])
        ws = [jnp.exp(ms[g] - mx) for g in range(3)]
        den = ws[0] * l_s[0, rows, :] + ws[1] * l_s[1, rows, :] + ws[2] * l_s[2, rows, :]
        num = ws[0] * acc_s[0, rows, :] + ws[1] * acc_s[1, rows, :] + ws[2] * acc_s[2, rows, :]
        o_ref[rows, :] = (num / den).astype(o_ref.dtype)


def _dilated(proj, bias):
    b, _, s, _ = proj.shape
    specs = []
    for g in range(3):
        for cb in (CB_QB, CB_KB, CB_VB):
            specs.append(pl.BlockSpec((None, None, s, HEAD_DIM),
                                      lambda i, h, cb=cb, g=g: (i, cb + g * DIL_HEADS_PER_GROUP + h, 0, 0)))
    specs.append(pl.BlockSpec((None, 3, 1, BIAS_LEN), lambda i, h: (h, 0, 0, 0)))
    return pl.pallas_call(
        _dil_body,
        grid=(b, DIL_HEADS_PER_GROUP),
        in_specs=specs,
        out_specs=pl.BlockSpec((None, None, s, HEAD_DIM), lambda i, h: (i, h, 0, 0)),
        out_shape=jax.ShapeDtypeStruct((b, DIL_HEADS_PER_GROUP, s, HEAD_DIM), BF16),
        scratch_shapes=[
            pltpu.VMEM((s, HEAD_DIM), F32),
            pltpu.VMEM((s, HEAD_DIM), F32),
            pltpu.VMEM((s, HEAD_DIM), F32),
            pltpu.VMEM((s, HEAD_DIM), F32),
            pltpu.VMEM((s + 2 * HALF_WINDOW, HEAD_DIM), F32),
            pltpu.VMEM((s + 2 * HALF_WINDOW, HEAD_DIM), F32),
            pltpu.VMEM((s + 2 * HALF_WINDOW, HEAD_DIM), BF16),
            pltpu.VMEM((s + 2 * HALF_WINDOW, HEAD_DIM), BF16),
            pltpu.VMEM((3, s, HEAD_DIM), F32),
            pltpu.VMEM((3, s, 1), F32),
            pltpu.VMEM((3, s, 1), F32),
        ],
        compiler_params=pltpu.CompilerParams(dimension_semantics=("parallel", "arbitrary"),
                                             vmem_limit_bytes=VMEM_LIMIT),
        name="dilated_attn",
    )(*([proj] * 9), bias)


def _mix_out_body(x_ref, oa_ref, ob_ref, ga_ref, gb_ref, wa_ref, wb_ref, wo_ref, ln_ref, o_ref):
    wide = lambda ref: jnp.concatenate([ref[c] for c in range(ref.shape[0])], axis=1)
    ya = _mm(wide(oa_ref), wa_ref[...])
    yb = _mm(wide(ob_ref), wb_ref[...])
    merged = _sigmoid(wide(ga_ref).astype(F32)) * ya + _sigmoid(wide(gb_ref).astype(F32)) * yb
    y = _mm(merged.astype(BF16), wo_ref[...])
    o_ref[...] = x_ref[...] + _rms(y, ln_ref[...])


def _mix_out(x3d, o_a, o_b, proj, w_a, w_b, w_o, ln_post, tm):
    b, s, d = x3d.shape
    nslab = d // LANES
    assert CB_GA % nslab == 0 and CB_GB % nslab == 0 and s % tm == 0
    const = lambda shape: pl.BlockSpec(shape, lambda i, j: (0, 0))
    slabs = lambda n, blk: pl.BlockSpec((None, n, tm, LANES), lambda i, j, blk=blk: (i, blk, j, 0))
    return pl.pallas_call(
        _mix_out_body,
        grid=(b, s // tm),
        in_specs=[pl.BlockSpec((None, tm, d), lambda i, j: (i, j, 0)),
                  slabs(GDN_HEADS, 0), slabs(DIL_HEADS_PER_GROUP, 0),
                  slabs(nslab, CB_GA // nslab), slabs(nslab, CB_GB // nslab),
                  const((GDN_WIDTH, d)), const((DIL_OUT_WIDTH, d)), const((d, d)), const((1, d))],
        out_specs=pl.BlockSpec((None, tm, d), lambda i, j: (i, j, 0)),
        out_shape=jax.ShapeDtypeStruct((b, s, d), F32),
        compiler_params=pltpu.CompilerParams(dimension_semantics=("parallel", "parallel"),
                                             vmem_limit_bytes=VMEM_LIMIT),
        name="mix_out",
    )(x3d, o_a, o_b, proj, proj, w_a, w_b, w_o, ln_post)


def _mlp_body(tf, x_ref, g1_ref, w1_ref, w2_ref, g2_ref, o_ref):
    x = x_ref[...]
    h = _rms(x, g1_ref[...]).astype(BF16)
    acc = None
    for k in range(w1_ref.shape[1] // tf):
        cols = pl.ds(k * tf, tf)
        f = jnp.maximum(_mm(h, w1_ref[:, cols]), 0.0)
        part = _mm((f * f).astype(BF16), w2_ref[cols, :])
        acc = part if acc is None else acc + part
    o_ref[...] = x + _rms(acc, g2_ref[...])


def _mlp(x2d, ln_pre, w1, w2, ln_post, tm, tf):
    t, d = x2d.shape
    dff = w1.shape[1]
    resident = lambda shape: pl.BlockSpec(shape, lambda i: (0, 0), pipeline_mode=pl.Buffered(1))
    return pl.pallas_call(
        functools.partial(_mlp_body, tf),
        grid=(t // tm,),
        in_specs=[pl.BlockSpec((tm, d), lambda i: (i, 0)),
                  resident((1, d)), resident((d, dff)), resident((dff, d)), resident((1, d))],
        out_specs=pl.BlockSpec((tm, d), lambda i: (i, 0)),
        out_shape=jax.ShapeDtypeStruct((t, d), F32),
        compiler_params=pltpu.CompilerParams(dimension_semantics=("parallel",),
                                             vmem_limit_bytes=VMEM_LIMIT),
        name="mlp",
    )(x2d, ln_pre, w1, w2, ln_post)


def _t5_bucket_np(rel):
    nb = REL_BUCKETS // 2
    ret = (rel > 0).astype(np.int32) * nb
    n = np.abs(rel)
    max_exact = nb // 2
    large = max_exact + (np.log(np.maximum(n, 1) / max_exact) / math.log(REL_MAX_DIST / max_exact)
                         * (nb - max_exact)).astype(np.int32)
    large = np.minimum(large, nb - 1)
    return ret + np.where(n < max_exact, n, large).astype(np.int32)


def _attention_bias(rel_bias):
    c = np.arange(BIAS_LEN)
    off = np.where(c < KWIN, np.clip(c - HALF_WINDOW, -HALF_WINDOW, HALF_WINDOW), -HALF_WINDOW)
    per_group = []
    for gi, (_, dil) in enumerate(DIL_GROUPS):
        bt = rel_bias[_t5_bucket_np(off * dil)]
        per_group.append(bt[:, gi * DIL_HEADS_PER_GROUP:(gi + 1) * DIL_HEADS_PER_GROUP])
    return jnp.transpose(jnp.stack(per_group, axis=0), (2, 0, 1))[:, :, None, :].astype(F32)


def _gdn_params(a_log_f, a_log_b, dt_bias_f, dt_bias_b):
    pad = lambda f, bk: jnp.pad(jnp.concatenate([f, bk]), (0, LANES - 2 * GDN_HEADS))
    return jnp.stack([pad(a_log_f, a_log_b), pad(dt_bias_f, dt_bias_b)]
                     + [jnp.zeros((LANES,), F32)] * 6, axis=0).astype(F32)


def kernel(x, rel_bias, ln_mix_pre, w_in, conv_w, a_log_f, a_log_b, dt_bias_f, dt_bias_b, norm_a,
           w_branch_a, w_branch_b, w_out, ln_mix_post, ln_mlp_pre, w_ff1, w_ff2, ln_mlp_post):
    b, s, d = x.shape
    t = b * s
    c_small = 4 * GDN_WIDTH
    bias = _attention_bias(rel_bias)
    for l in range(ln_mix_pre.shape[0]):
        w = w_in[l].astype(BF16)
        c_qb = c_small + N_SMALL
        c_ga = c_qb + 3 * DIL_WIDTH
        w_wide = jnp.concatenate([w[:, :c_small], w[:, c_ga:], w[:, c_qb:c_ga]], axis=1)
        w_small = jnp.pad(w[:, c_small:c_qb], ((0, 0), (0, LANES - N_SMALL)))
        proj, small = _in_proj(x, ln_mix_pre[l][None, :], w_wide, w_small, 1536)

        lanepar = _gdn_params(a_log_f[l], a_log_b[l], dt_bias_f[l], dt_bias_b[l])
        o_a = _gdn(proj, small, lanepar, conv_w[l].astype(F32), norm_a[l][None, :].astype(F32))
        o_b = _dilated(proj, bias)

        x1 = _mix_out(x, o_a, o_b, proj,
                      w_branch_a[l].astype(BF16), w_branch_b[l].astype(BF16), w_out[l].astype(BF16),
                      ln_mix_post[l][None, :], 512)
        x2d = _mlp(x1.reshape(t, d), ln_mlp_pre[l][None, :], w_ff1[l].astype(BF16), w_ff2[l].astype(BF16),
                   ln_mlp_post[l][None, :], 1024, 1024)
        x = x2d.reshape(b, s, d)
    return x
```

```python
import functools
import math

import numpy as np
import jax
import jax.numpy as jnp
from jax import lax
from jax.experimental import pallas as pl
from jax.experimental.pallas import tpu as pltpu

F32 = jnp.float32
BF16 = jnp.bfloat16
HIGHEST = lax.Precision.HIGHEST

D_MODEL = 1024
GDN_HEADS = 8
HEAD_DIM = 128
GDN_WIDTH = GDN_HEADS * HEAD_DIM
CHUNK = 64
TRI_BASE = 8
CHUNK_GROUP = 16
GDN_HEADS_PER_STEP = 2
ATT_GROUP = 8
CONV_K = 5
DIL_GROUPS = ((128, 1), (512, 4), (2048, 16))
DIL_HEADS_PER_GROUP = 4
DIL_HEADS = 12
DIL_WIDTH = DIL_HEADS * HEAD_DIM
DIL_OUT_WIDTH = DIL_HEADS_PER_GROUP * HEAD_DIM
HALF_WINDOW = 64
QBLK = 128
KWIN = QBLK + 2 * HALF_WINDOW
REL_BUCKETS = 32
REL_MAX_DIST = 1024
D_FF = 4 * D_MODEL
EPS = 1e-6
NEG = -1e30
N_SMALL = 4 * GDN_HEADS
LANES = 128
BIAS_LEN = 3 * LANES

CB_QA, CB_KA, CB_VA, CB_ZA = 0, 8, 16, 24
CB_GA, CB_GB = 32, 40
CB_QB, CB_KB, CB_VB = 48, 60, 72
N_WIDE = 84 * LANES

VMEM_LIMIT = 56 * 1024 * 1024


def _mm(a, b):
    return jnp.dot(a, b, preferred_element_type=F32)


def _mm_nt(a, b):
    return lax.dot_general(a, b, (((1,), (1,)), ((), ())), preferred_element_type=F32)


def _mm_hi(a, b):
    return jnp.dot(a, b, preferred_element_type=F32, precision=HIGHEST)


def _rms(x, gain):
    return x * lax.rsqrt(jnp.mean(x * x, axis=-1, keepdims=True) + EPS) * gain


def _sigmoid(x):
    return 1.0 / (1.0 + jnp.exp(-x))


def _silu(x):
    h = 0.5 * x
    return h + h * jnp.tanh(h)


def _softplus(x):
    return jnp.maximum(x, 0.0) + jnp.log(1.0 + jnp.exp(-jnp.abs(x)))


def _in_proj_body(x_ref, g_ref, w_ref, ws_ref, o_ref, small_ref, h_ref):
    @pl.when(pl.program_id(1) == 0)
    def _():
        h_ref[...] = _rms(x_ref[...], g_ref[...]).astype(BF16)
        small_ref[...] = _mm(h_ref[...], ws_ref[...])

    h = h_ref[...]
    for c in range(0, o_ref.shape[0], 2):
        res = _mm(h, w_ref[:, c * LANES:(c + 2) * LANES])
        o_ref[c] = res[:, :LANES].astype(o_ref.dtype)
        o_ref[c + 1] = res[:, LANES:].astype(o_ref.dtype)


def _in_proj(x3d, gain, w, w_small, tn):
    b, s, d = x3d.shape
    n = w.shape[1]
    return pl.pallas_call(
        _in_proj_body,
        grid=(b, n // tn),
        in_specs=[pl.BlockSpec((None, s, d), lambda i, j: (i, 0, 0)),
                  pl.BlockSpec((1, d), lambda i, j: (0, 0)),
                  pl.BlockSpec((d, tn), lambda i, j: (0, j)),
                  pl.BlockSpec((d, LANES), lambda i, j: (0, 0))],
        out_specs=[pl.BlockSpec((None, tn // LANES, s, LANES), lambda i, j: (i, j, 0, 0)),
                   pl.BlockSpec((None, s, LANES), lambda i, j: (i, 0, 0))],
        out_shape=[jax.ShapeDtypeStruct((b, n // LANES, s, LANES), BF16),
                   jax.ShapeDtypeStruct((b, s, LANES), F32)],
        scratch_shapes=[pltpu.VMEM((s, d), BF16)],
        compiler_params=pltpu.CompilerParams(dimension_semantics=("parallel", "arbitrary"),
                                             vmem_limit_bytes=VMEM_LIMIT),
        name="in_proj",
    )(x3d, gain, w, w_small)


def _split_bf16(x):
    hi = x.astype(BF16)
    return hi, (x - hi.astype(F32)).astype(BF16)


def _split3_lanes(x):
    hi = x.astype(BF16)
    r = x - hi.astype(F32)
    mid = r.astype(BF16)
    lo = (r - mid.astype(F32)).astype(BF16)
    return jnp.concatenate([hi, mid, lo], axis=1)


def _block_diag(x, groups):
    return jnp.concatenate([jnp.where(g, x, 0.0) for g in groups], axis=0)


def _mm_parts(lhs, rhs, masks):
    (lh, ll), (rh, rl) = lhs, rhs
    bd = lambda p: jnp.concatenate([p * m for m in masks], axis=0)
    rh_bd = bd(rh)
    return (_mm(jnp.concatenate([lh, ll], axis=1), jnp.concatenate([rh_bd, rh_bd], axis=0))
            + _mm(lh, bd(rl)))


def _unit_tri_inverse_pairs(a2s, eye, lo, same_blk):
    masks = [jnp.where(g, 1.0, 0.0).astype(BF16) for g in lo]
    stack = lambda a, b: tuple(jnp.concatenate([p, q], axis=0) for p, q in zip(a, b))
    ds = [jnp.where(same_blk[0], a2, 0.0) for a2 in a2s]
    dps = [_split_bf16(d) for d in ds]
    xs = [_mm_parts(dp, dp, masks) for dp in dps]
    ts = [eye - d for d in ds]
    levels = int(math.log2(TRI_BASE)) - 1
    for lvl in range(levels):
        tps = [_split_bf16(t) for t in ts]
        xps = [_split_bf16(x) for x in xs]
        if lvl < levels - 1:
            outs = [_mm_parts(stack(tp, xp), xp, masks) for tp, xp in zip(tps, xps)]
            ts = [t + out[:CHUNK] for t, out in zip(ts, outs)]
            xs = [out[CHUNK:] for out in outs]
        else:
            ts = [t + _mm_parts(tp, xp, masks) for t, tp, xp in zip(ts, tps, xps)]
    inner = same_blk[0]
    for outer in tuple(same_blk[1:]) + (None,):
        between = [jnp.where(inner, 0.0, a2) if outer is None else jnp.where(outer & ~inner, a2, 0.0)
                   for a2 in a2s]
        tps = [_split_bf16(t) for t in ts]
        tls = [_mm_parts(tp, _split_bf16(l), masks) for tp, l in zip(tps, between)]
        ts = [t - _mm_parts(_split_bf16(tl), tp, masks) for t, tl, tp in zip(ts, tls, tps)]
        inner = outer
    return ts


def _gdn_body(lanepar_ref, cwq_ref, cwk_ref, cwv_ref, q_ref, k_ref, v_ref, z_ref, small_ref,
              na_ref, o_ref,
              xp, qn, kn, vn, act, gcf, bcf, mq_s, c_s, dec_s, of_s, ob_s):
    nh, s, _ = q_ref.shape
    nchunk = s // CHUNK
    hp = pl.program_id(1)
    rc = 256

    @pl.when(hp == 0)
    def _():
        lane = lax.broadcasted_iota(jnp.int32, (rc, LANES), 1)
        bi = lax.broadcasted_iota(jnp.int32, (rc, rc), 0)
        bj = lax.broadcasted_iota(jnp.int32, (rc, rc), 1)
        same_chunk = jnp.bitwise_xor(bi, bj) < CHUNK
        cum_f = jnp.where(same_chunk & (bj <= bi), 1.0, 0.0).astype(BF16)
        cum_b = jnp.where(same_chunk & (bj >= bi), 1.0, 0.0).astype(BF16)
        neg_a = -jnp.exp(lanepar_ref[0:1, :])
        dtb = lanepar_ref[1:2, :]
        sum3 = lambda y: y[:, :LANES] + y[:, LANES:2 * LANES] + y[:, 2 * LANES:]
        for c in range(s // rc):
            rows = pl.ds(c * rc, rc)
            slab = small_ref[rows, :]
            g3 = _split3_lanes(neg_a * _softplus(slab + dtb))
            act[rows, :] = jnp.where(lane < GDN_HEADS, sum3(_mm(cum_f, g3)),
                                     jnp.where(lane < 2 * GDN_HEADS, sum3(_mm(cum_b, g3)), _sigmoid(slab)))

    for hh in range(nh):
        lanes = pl.ds(hh * HEAD_DIM, HEAD_DIM)
        _gdn_prepare(hp * nh + hh, cwq_ref.at[:, lanes], cwk_ref.at[:, lanes], cwv_ref.at[:, lanes],
                     q_ref.at[hh], k_ref.at[hh], v_ref.at[hh],
                     xp, qn, kn, vn, act, gcf, bcf,
                     mq_s.at[hh], c_s.at[hh], dec_s.at[hh], of_s.at[hh], ob_s.at[hh])

    def phase2(step, states):
        new = []
        for idx, st in enumerate(states):
            hh, d = divmod(idx, 2)
            n = step if d == 0 else nchunk - 1 - step
            r = _mm(mq_s[hh, d, n], st.astype(BF16))
            rows = pl.ds(pl.multiple_of(n * CHUNK, CHUNK), CHUNK)
            out_s = of_s if d == 0 else ob_s
            out_s[hh, rows, :] = out_s[hh, rows, :] + r[HEAD_DIM:]
            new.append(st * dec_s[hh, d, n] + r[:HEAD_DIM] + c_s[hh, d, n])
        return tuple(new)

    zero_state = jnp.zeros((HEAD_DIM, HEAD_DIM), F32)
    lax.fori_loop(0, nchunk, phase2, (zero_state,) * (2 * nh))

    for hh in range(nh):
        for c in range(s // rc):
            rows = pl.ds(c * rc, rc)
            o = of_s[hh, rows, :] + ob_s[hh, rows, :]
            z = z_ref[hh, rows, :].astype(F32)
            o_ref[hh, rows, :] = (_rms(o, na_ref[...]) * _silu(z)).astype(o_ref.dtype)


def _gdn_prepare(h, cwq_ref, cwk_ref, cwv_ref, q_ref, k_ref, v_ref,
                 xp, qn, kn, vn, act, gcf, bcf, mq_s, c_s, dec_s, of_s, ob_s):
    s = q_ref.shape[0]
    nchunk = s // CHUNK
    rc = 256

    zeros8 = jnp.zeros((8, HEAD_DIM), F32)
    xp[pl.ds(0, 8), :] = zeros8
    xp[pl.ds(8 + s, 8), :] = zeros8

    def conv_into(src_ref, cw_ref, dst_ref, l2, scale):
        xp[pl.ds(8, s), :] = src_ref[...].astype(F32)
        for c in range(s // rc):
            acc = None
            for j in range(CONV_K):
                t = xp[pl.ds(c * rc + 8 - CONV_K // 2 + j, rc), :] * cw_ref[j:j + 1, :]
                acc = t if acc is None else acc + t
            y = _silu(acc)
            if l2:
                y = y * lax.rsqrt(jnp.sum(y * y, axis=-1, keepdims=True) + EPS)
                if scale != 1.0:
                    y = y * scale
            dst_ref[pl.ds(c * rc, rc), :] = y

    conv_into(q_ref, cwq_ref, qn, True, HEAD_DIM ** -0.5)
    conv_into(k_ref, cwk_ref, kn, True, 1.0)
    conv_into(v_ref, cwv_ref, vn, False, 1.0)

    sel_k = lax.broadcasted_iota(jnp.int32, (LANES, 4 * LANES), 0)
    sel_c = lax.broadcasted_iota(jnp.int32, (LANES, 4 * LANES), 1)
    sel = jnp.where(sel_k == (sel_c >> 7) * GDN_HEADS + h, 1.0, 0.0).astype(BF16)
    sel3 = jnp.concatenate([sel, sel, sel], axis=0)
    for c in range(s // rc):
        rows = pl.ds(c * rc, rc)
        picked = _mm(_split3_lanes(act[rows, :]), sel3)
        for d in range(2):
            gcf[d, rows, :] = picked[:, d * LANES:(d + 1) * LANES]
            bcf[d, rows, :] = picked[:, (2 + d) * LANES:(3 + d) * LANES]
    ii = lax.broadcasted_iota(jnp.int32, (CHUNK, 2 * CHUNK), 0)
    lj = lax.broadcasted_iota(jnp.int32, (CHUNK, 2 * CHUNK), 1)
    lo = lj < CHUNK
    hi = lj >= CHUNK
    jj = jnp.where(lo, lj, lj - CHUNK)
    is_diag = ii == jj
    tri2 = (lo & (ii >= jj)) | (hi & (ii <= jj))
    strict2 = (lo & (ii > jj)) | (hi & (ii < jj))
    pair = (lo, hi)
    lo_t = lax.broadcasted_iota(jnp.int32, (2 * CHUNK, 2 * CHUNK), 1) < CHUNK
    pair_t = (lo_t, ~lo_t)
    eye2 = jnp.where(is_diag, 1.0, 0.0)
    blk_bits = range(int(math.log2(TRI_BASE)), int(math.log2(CHUNK)))
    same_blk = tuple((ii >> sh) == (jj >> sh) for sh in blk_bits)

    def phase1(grp, carry):
        ns = [grp * CHUNK_GROUP + i for i in range(CHUNK_GROUP)]
        rws = [pl.ds(pl.multiple_of(n * CHUNK, CHUNK), CHUNK) for n in ns]
        qv = [qn[r, :] for r in rws]
        kv = [kn[r, :] for r in rws]
        kb = [k.astype(BF16) for k in kv]
        grams = [_mm_nt(jnp.concatenate([kb_, q.astype(BF16)], axis=0), jnp.concatenate([kb_, kb_], axis=0))
                 for kb_, q in zip(kb, qv)]
        gcc = [(gcf[0, r, :], gcf[1, r, :]) for r in rws]
        bcc = [(bcf[0, r, :], bcf[1, r, :]) for r in rws]
        gcc2 = [jnp.where(lo, gf, gb) for gf, gb in gcc]
        gcr2 = [jnp.sum(jnp.where(is_diag, g2, 0.0), axis=0, keepdims=True) for g2 in gcc2]
        gams = [jnp.where(tri2, jnp.exp(jnp.where(tri2, g2 - gr, 0.0)), 0.0) for g2, gr in zip(gcc2, gcr2)]
        a2s = [jnp.where(strict2, gram[:CHUNK] * jnp.where(lo, bf, bb) * gam, 0.0)
               for gram, (bf, bb), gam in zip(grams, bcc, gams)]
        t2s = _unit_tri_inverse_pairs(a2s, eye2, pair, same_blk)
        uws = []
        for i, r in enumerate(rws):
            v = vn[r, :]
            (gf, gb), (bf, bb) = gcc[i], bcc[i]
            rhs = jnp.concatenate([jnp.concatenate([v * bf, kv[i] * (bf * jnp.exp(gf))], axis=1),
                                   jnp.concatenate([v * bb, kv[i] * (bb * jnp.exp(gb))], axis=1)], axis=0)
            uws.append(_mm(_block_diag(t2s[i], pair).astype(BF16), rhs.astype(BF16)))
        fins = []
        for i in range(CHUNK_GROUP):
            gf, gb = gcc[i]
            gtot_f = gcr2[i][:, CHUNK - 1:CHUNK]
            gtot_b = gcr2[i][:, CHUNK:CHUNK + 1]
            kg = jnp.concatenate([kv[i] * jnp.exp(gtot_f - gf), kv[i] * jnp.exp(gtot_b - gb)], axis=0)
            lhs = jnp.concatenate([_block_diag(kg.T, pair_t), _block_diag(grams[i][CHUNK:] * gams[i], pair)], axis=0)
            fins.append(_mm(lhs.astype(BF16), uws[i].astype(BF16)))
            dec_s[0, ns[i]] = jnp.broadcast_to(jnp.exp(gtot_f), (1, HEAD_DIM))
            dec_s[1, ns[i]] = jnp.broadcast_to(jnp.exp(gtot_b), (1, HEAD_DIM))
        for i, (n, r) in enumerate(zip(ns, rws)):
            fin = fins[i]
            for d in range(2):
                top = fin[d * HEAD_DIM:(d + 1) * HEAD_DIM]
                bot = fin[2 * HEAD_DIM + d * CHUNK:2 * HEAD_DIM + (d + 1) * CHUNK]
                c_s[d, n] = top[:, :HEAD_DIM]
                mq_s[d, n, pl.ds(0, HEAD_DIM), :] = (-top[:, HEAD_DIM:]).astype(BF16)
                mq_s[d, n, pl.ds(HEAD_DIM, CHUNK), :] = (qv[i] * jnp.exp(gcc[i][d]) - bot[:, HEAD_DIM:]).astype(BF16)
                (of_s if d == 0 else ob_s)[r, :] = bot[:, :HEAD_DIM]
        return carry

    lax.fori_loop(0, nchunk // CHUNK_GROUP, phase1, 0)


def _gdn(proj, small, lanepar, conv_w, norm_a):
    b, _, s, _ = proj.shape
    nchunk = s // CHUNK
    nh = GDN_HEADS_PER_STEP
    assert 2 * CHUNK == LANES == HEAD_DIM and nchunk % CHUNK_GROUP == 0 and GDN_HEADS % nh == 0
    assert all(cb % nh == 0 for cb in (CB_QA, CB_KA, CB_VA, CB_ZA))
    col = lambda cb: pl.BlockSpec((None, nh, s, HEAD_DIM), lambda i, h, cb=cb: (i, cb // nh + h, 0, 0))
    cw = lambda cb: pl.BlockSpec((CONV_K, nh * HEAD_DIM), lambda i, h, cb=cb: (0, cb // nh + h))
    return pl.pallas_call(
        _gdn_body,
        grid=(b, GDN_HEADS // nh),
        in_specs=[pl.BlockSpec((8, LANES), lambda i, h: (0, 0)),
                  cw(0), cw(GDN_HEADS), cw(2 * GDN_HEADS),
                  col(CB_QA), col(CB_KA), col(CB_VA), col(CB_ZA),
                  pl.BlockSpec((None, s, LANES), lambda i, h: (i, 0, 0)),
                  pl.BlockSpec((1, HEAD_DIM), lambda i, h: (0, 0))],
        out_specs=pl.BlockSpec((None, nh, s, HEAD_DIM), lambda i, h: (i, h, 0, 0)),
        out_shape=jax.ShapeDtypeStruct((b, GDN_HEADS, s, HEAD_DIM), BF16),
        scratch_shapes=[
            pltpu.VMEM((s + 16, HEAD_DIM), F32),
            pltpu.VMEM((s, HEAD_DIM), F32),
            pltpu.VMEM((s, HEAD_DIM), F32),
            pltpu.VMEM((s, HEAD_DIM), F32),
            pltpu.VMEM((s, LANES), F32),
            pltpu.VMEM((2, s, LANES), F32),
            pltpu.VMEM((2, s, LANES), F32),
            pltpu.VMEM((nh, 2, nchunk, HEAD_DIM + CHUNK, HEAD_DIM), BF16),
            pltpu.VMEM((nh, 2, nchunk, HEAD_DIM, HEAD_DIM), F32),
            pltpu.VMEM((nh, 2, nchunk, 1, HEAD_DIM), F32),
            pltpu.VMEM((nh, s, HEAD_DIM), F32),
            pltpu.VMEM((nh, s, HEAD_DIM), F32),
        ],
        compiler_params=pltpu.CompilerParams(dimension_semantics=("parallel", "arbitrary"),
                                             vmem_limit_bytes=VMEM_LIMIT),
        name="gdn",
    )(lanepar, conv_w, conv_w, conv_w, proj, proj, proj, proj, small, norm_a)


def _dil_body(q0, k0, v0, q1, k1, v1, q2, k2, v2, bias_ref, o_ref,
              qf, kf, vf, qs, kp, vp, kpb, vpb, acc_s, m_s, l_s):
    s = q0.shape[0]
    scale = HEAD_DIM ** -0.5
    qi = lax.broadcasted_iota(jnp.int32, (QBLK, KWIN), 0)
    kj = lax.broadcasted_iota(jnp.int32, (QBLK, KWIN), 1)
    off = kj - HALF_WINDOW - qi
    band = (off >= -HALF_WINDOW) & (off <= HALF_WINDOW)
    zpad = jnp.zeros((HALF_WINDOW, HEAD_DIM), F32)

    for g, (q_ref, k_ref, v_ref) in enumerate(((q0, k0, v0), (q1, k1, v1), (q2, k2, v2))):
        dil = DIL_GROUPS[g][1]
        length = s // dil
        nblk = length // QBLK
        seg_q = max(length, QBLK)
        seg_k = seg_q + 2 * HALF_WINDOW
        nseg = min(dil, max(1, ATT_GROUP // nblk))
        bpi = min(nblk, ATT_GROUP)
        if dil > 1:
            qf[...] = q_ref[...].astype(F32)
            kf[...] = k_ref[...].astype(F32)
            vf[...] = v_ref[...].astype(F32)
            q_src, k_src, v_src = qs, kp, vp
        else:
            q_src, k_src, v_src = q_ref, kpb, vpb
            kpb[pl.ds(HALF_WINDOW, length), :] = k_ref[...]
            vpb[pl.ds(HALF_WINDOW, length), :] = v_ref[...]
        for c in range(nseg):
            for buf in (k_src, v_src):
                buf[pl.ds(c * seg_k, HALF_WINDOW), :] = zpad.astype(buf.dtype)
                buf[pl.ds(c * seg_k + HALF_WINDOW + length, HALF_WINDOW), :] = zpad.astype(buf.dtype)
        bias = pltpu.roll(jnp.broadcast_to(bias_ref[g], (QBLK, BIAS_LEN)), 0, 1,
                          stride=1, stride_axis=0)[:, :KWIN]

        def stage(c, r, dil=dil, length=length, seg_q=seg_q, seg_k=seg_k):
            if dil == 1:
                return
            cls = pl.ds(r, length, stride=dil)
            qs[pl.ds(c * seg_q, length), :] = qf[cls, :]
            kp[pl.ds(c * seg_k + HALF_WINDOW, length), :] = kf[cls, :]
            vp[pl.ds(c * seg_k + HALF_WINDOW, length), :] = vf[cls, :]

        def attend(items, g=g, dil=dil, length=length, bias=bias, q_src=q_src, k_src=k_src, v_src=v_src):
            logits = []
            for q_off, k_off, j0, r in items:
                kpos = j0 - HALF_WINDOW + kj
                ok = band & (kpos >= 0) & (kpos < length)
                sc = _mm_nt(q_src[pl.ds(q_off, QBLK), :].astype(BF16), k_src[pl.ds(k_off, KWIN), :].astype(BF16))
                logits.append(jnp.where(ok, sc * scale + bias, NEG))
            ms = [jnp.max(lg, axis=-1, keepdims=True) for lg in logits]
            ps = [jnp.exp(lg - m) for lg, m in zip(logits, ms)]
            ls = [jnp.sum(p, axis=-1, keepdims=True) for p in ps]
            accs = [_mm(p.astype(BF16), v_src[pl.ds(it[1], KWIN), :].astype(BF16)) for p, it in zip(ps, items)]
            for (q_off, k_off, j0, r), m, lsum, acc in zip(items, ms, ls, accs):
                tok = pl.ds(r + dil * j0, QBLK, stride=dil) if dil > 1 else pl.ds(j0, QBLK)
                acc_s[g, tok, :] = acc
                m_s[g, tok, :] = m
                l_s[g, tok, :] = lsum

        def classes(it, carry, nblk=nblk, nseg=nseg, bpi=bpi, seg_q=seg_q, seg_k=seg_k,
                    stage=stage, attend=attend):
            rs = [it * nseg + c for c in range(nseg)]
            for c, r in enumerate(rs):
                stage(c, r)

            def blocks(jt, carry2):
                items = []
                for c, r in enumerate(rs):
                    for i in range(bpi):
                        j0 = (jt * bpi + i) * QBLK
                        if not isinstance(j0, int):
                            j0 = pl.multiple_of(j0, QBLK)
                        items.append((c * seg_q + j0, c * seg_k + j0, j0, r))
                attend(items)
                return carry2

            if nblk == bpi:
                blocks(0, 0)
            else:
                lax.fori_loop(0, nblk // bpi, blocks, 0)
            return carry

        lax.fori_loop(0, dil // nseg, classes, 0)

    rc = 256
    for c in range(s // rc):
        rows = pl.ds(c * rc, rc)
        ms = [m_s[g, rows, :] for g in range(3)]
        mx = jnp.maximum(jnp.maximum(ms[0], ms[1]), ms[2])
        ws = [jnp.exp(ms[g] - mx) for g in range(3)]
        den = ws[0] * l_s[0, rows, :] + ws[1] * l_s[1, rows, :] + ws[2] * l_s[2, rows, :]
        num = ws[0] * acc_s[0, rows, :] + ws[1] * acc_s[1, rows, :] + ws[2] * acc_s[2, rows, :]
        o_ref[rows, :] = (num / den).astype(o_ref.dtype)


def _dilated(proj, bias):
    b, _, s, _ = proj.shape
    specs = []
    for g in range(3):
        for cb in (CB_QB, CB_KB, CB_VB):
            specs.append(pl.BlockSpec((None, None, s, HEAD_DIM),
                                      lambda i, h, cb=cb, g=g: (i, cb + g * DIL_HEADS_PER_GROUP + h, 0, 0)))
    specs.append(pl.BlockSpec((None, 3, 1, BIAS_LEN), lambda i, h: (h, 0, 0, 0)))
    return pl.pallas_call(
        _dil_body,
        grid=(b, DIL_HEADS_PER_GROUP),
        in_specs=specs,
        out_specs=pl.BlockSpec((None, None, s, HEAD_DIM), lambda i, h: (i, h, 0, 0)),
        out_shape=jax.ShapeDtypeStruct((b, DIL_HEADS_PER_GROUP, s, HEAD_DIM), BF16),
        scratch_shapes=[
            pltpu.VMEM((s, HEAD_DIM), F32),
            pltpu.VMEM((s, HEAD_DIM), F32),
            pltpu.VMEM((s, HEAD_DIM), F32),
            pltpu.VMEM((s, HEAD_DIM), F32),
            pltpu.VMEM((s + 2 * HALF_WINDOW, HEAD_DIM), F32),
            pltpu.VMEM((s + 2 * HALF_WINDOW, HEAD_DIM), F32),
            pltpu.VMEM((s + 2 * HALF_WINDOW, HEAD_DIM), BF16),
            pltpu.VMEM((s + 2 * HALF_WINDOW, HEAD_DIM), BF16),
            pltpu.VMEM((3, s, HEAD_DIM), F32),
            pltpu.VMEM((3, s, 1), F32),
            pltpu.VMEM((3, s, 1), F32),
        ],
        compiler_params=pltpu.CompilerParams(dimension_semantics=("parallel", "arbitrary"),
                                             vmem_limit_bytes=VMEM_LIMIT),
        name="dilated_attn",
    )(*([proj] * 9), bias)


def _mix_out_body(x_ref, oa_ref, ob_ref, ga_ref, gb_ref, wa_ref, wb_ref, wo_ref, ln_ref, o_ref):
    wide = lambda ref: jnp.concatenate([ref[c] for c in range(ref.shape[0])], axis=1)
    ya = _mm(wide(oa_ref), wa_ref[...])
    yb = _mm(wide(ob_ref), wb_ref[...])
    merged = _sigmoid(wide(ga_ref).astype(F32)) * ya + _sigmoid(wide(gb_ref).astype(F32)) * yb
    y = _mm(merged.astype(BF16), wo_ref[...])
    o_ref[...] = x_ref[...] + _rms(y, ln_ref[...])


def _mix_out(x3d, o_a, o_b, proj, w_a, w_b, w_o, ln_post, tm):
    b, s, d = x3d.shape
    nslab = d // LANES
    assert CB_GA % nslab == 0 and CB_GB % nslab == 0 and s % tm == 0
    const = lambda shape: pl.BlockSpec(shape, lambda i, j: (0, 0), pipeline_mode=pl.Buffered(1))
    slabs =lambda n, blk: pl.BlockSpec((None, n, tm, LANES), lambda i, j, blk=blk: (i, blk, j, 0))
    return pl.pallas_call(
        _mix_out_body,
        grid=(b, s // tm),
        in_specs=[pl.BlockSpec((None, tm, d), lambda i, j: (i, j, 0)),
                  slabs(GDN_HEADS, 0), slabs(DIL_HEADS_PER_GROUP, 0),
                  slabs(nslab, CB_GA // nslab), slabs(nslab, CB_GB // nslab),
                  const((GDN_WIDTH, d)), const((DIL_OUT_WIDTH, d)), const((d, d)), const((1, d))],
        out_specs=pl.BlockSpec((None, tm, d), lambda i, j: (i, j, 0)),
        out_shape=jax.ShapeDtypeStruct((b, s, d), F32),
        compiler_params=pltpu.CompilerParams(dimension_semantics=("parallel", "parallel"),
                                             vmem_limit_bytes=VMEM_LIMIT),
        name="mix_out",
    )(x3d, o_a, o_b, proj, proj, w_a, w_b, w_o, ln_post)


def _mlp_body(tf, x_ref, g1_ref, w1_ref, w2_ref, g2_ref, o_ref):
    x = x_ref[...]
    h = _rms(x, g1_ref[...]).astype(BF16)
    acc = None
    for k in range(w1_ref.shape[1] // tf):
        cols = pl.ds(k * tf, tf)
        f = jnp.maximum(_mm(h, w1_ref[:, cols]), 0.0)
        part = _mm((f * f).astype(BF16), w2_ref[cols, :])
        acc = part if acc is None else acc + part
    o_ref[...] = x + _rms(acc, g2_ref[...])


def _mlp(x2d, ln_pre, w1, w2, ln_post, tm, tf):
    t, d = x2d.shape
    dff = w1.shape[1]
    resident = lambda shape: pl.BlockSpec(shape, lambda i: (0, 0), pipeline_mode=pl.Buffered(1))
    return pl.pallas_call(
        functools.partial(_mlp_body, tf),
        grid=(t // tm,),
        in_specs=[pl.BlockSpec((tm, d), lambda i: (i, 0)),
                  resident((1, d)), resident((d, dff)), resident((dff, d)), resident((1, d))],
        out_specs=pl.BlockSpec((tm, d), lambda i: (i, 0)),
        out_shape=jax.ShapeDtypeStruct((t, d), F32),
        compiler_params=pltpu.CompilerParams(dimension_semantics=("parallel",),
                                             vmem_limit_bytes=VMEM_LIMIT),
        name="mlp",
    )(x2d, ln_pre, w1, w2, ln_post)


def _t5_bucket_np(rel):
    nb = REL_BUCKETS // 2
    ret = (rel > 0).astype(np.int32) * nb
    n = np.abs(rel)
    max_exact = nb // 2
    large = max_exact + (np.log(np.maximum(n, 1) / max_exact) / math.log(REL_MAX_DIST / max_exact)
                         * (nb - max_exact)).astype(np.int32)
    large = np.minimum(large, nb - 1)
    return ret + np.where(n < max_exact, n, large).astype(np.int32)


def _attention_bias(rel_bias):
    c = np.arange(BIAS_LEN)
    off = np.where(c < KWIN, np.clip(c - HALF_WINDOW, -HALF_WINDOW, HALF_WINDOW), -HALF_WINDOW)
    per_group = []
    for gi, (_, dil) in enumerate(DIL_GROUPS):
        bt = rel_bias[_t5_bucket_np(off * dil)]
        per_group.append(bt[:, gi * DIL_HEADS_PER_GROUP:(gi + 1) * DIL_HEADS_PER_GROUP])
    return jnp.transpose(jnp.stack(per_group, axis=0), (2, 0, 1))[:, :, None, :].astype(F32)


def _gdn_params(a_log_f, a_log_b, dt_bias_f, dt_bias_b):
    pad = lambda f, bk: jnp.pad(jnp.concatenate([f, bk]), (0, LANES - 2 * GDN_HEADS))
    return jnp.stack([pad(a_log_f, a_log_b), pad(dt_bias_f, dt_bias_b)]
                     + [jnp.zeros((LANES,), F32)] * 6, axis=0).astype(F32)


def kernel(x, rel_bias, ln_mix_pre, w_in, conv_w, a_log_f, a_log_b, dt_bias_f, dt_bias_b, norm_a,
           w_branch_a, w_branch_b, w_out, ln_mix_post, ln_mlp_pre, w_ff1, w_ff2, ln_mlp_post):
    b, s, d = x.shape
    t = b * s
    c_small = 4 * GDN_WIDTH
    bias = _attention_bias(rel_bias)
    for l in range(ln_mix_pre.shape[0]):
        w = lax.optimization_barrier(w_in[l].astype(BF16))
        c_qb = c_small + N_SMALL
        c_ga = c_qb + 3 * DIL_WIDTH
        w_wide = jnp.concatenate([w[:, :c_small], w[:, c_ga:], w[:, c_qb:c_ga]], axis=1)
        w_small = jnp.pad(w[:, c_small:c_qb], ((0, 0), (0, LANES - N_SMALL)))
        proj, small = _in_proj(x, ln_mix_pre[l][None, :], w_wide, w_small, 1536)

        lanepar = _gdn_params(a_log_f[l], a_log_b[l], dt_bias_f[l], dt_bias_b[l])
        o_a = _gdn(proj, small, lanepar, conv_w[l].astype(F32), norm_a[l][None, :].astype(F32))
        o_b = _dilated(proj, bias)

        x1 = _mix_out(x, o_a, o_b, proj,
                      w_branch_a[l].astype(BF16), w_branch_b[l].astype(BF16), w_out[l].astype(BF16),
                      ln_mix_post[l][None, :], 1024)
        x2d = _mlp(x1.reshape(t, d), ln_mlp_pre[l][None, :], w_ff1[l].astype(BF16), w_ff2[l].astype(BF16),
                   ln_mlp_post[l][None, :], 1024, 1024)
        x = x2d.reshape(b, s, d)
    return x
```

```python
import functools
import math

import numpy as np
import jax
import jax.numpy as jnp
from jax import lax
from jax.experimental import pallas as pl
from jax.experimental.pallas import tpu as pltpu

F32 = jnp.float32
BF16 = jnp.bfloat16

LANES = 128
VMEM_LIMIT = 56 * 1024 * 1024

GDN_HEADS = 8
HEAD_DIM = 128
GDN_WIDTH = GDN_HEADS * HEAD_DIM
CHUNK = 64
TRI_BASE = 8
CHUNK_GROUP = 16
GDN_HEADS_PER_STEP = 2
CONV_K = 5
DIL_GROUPS = ((128, 1), (512, 4), (2048, 16))
DIL_HEADS_PER_GROUP = 4
DIL_WIDTH = len(DIL_GROUPS) * DIL_HEADS_PER_GROUP * HEAD_DIM
DIL_OUT_WIDTH = DIL_HEADS_PER_GROUP * HEAD_DIM
HALF_WINDOW = 64
QBLK = 128
KWIN = QBLK + 2 * HALF_WINDOW
ATT_GROUP = 8
BIAS_LEN = 3 * LANES
REL_BUCKETS = 32
REL_MAX_DIST = 1024
EPS = 1e-6
NEG = -1e30
N_SMALL = 4 * GDN_HEADS

CB_QA, CB_KA, CB_VA, CB_ZA = 0, 8, 16, 24
CB_GA, CB_GB = 32, 40
CB_QB, CB_KB, CB_VB = 48, 60, 72


def _mm(a, b):
    return jnp.dot(a, b, preferred_element_type=F32)


def _mm_nt(a, b):
    return lax.dot_general(a, b, (((1,), (1,)), ((), ())), preferred_element_type=F32)


def _rms(x, gain):
    return x * lax.rsqrt(jnp.mean(x * x, axis=-1, keepdims=True) + EPS) * gain


def _sigmoid(x):
    return 0.5 * jnp.tanh(0.5 * x) + 0.5


def _silu(x):
    h = 0.5 * x
    return h + h * jnp.tanh(h)


def _softplus(x):
    return jnp.maximum(x, 0.0) + jnp.log(1.0 + jnp.exp(-jnp.abs(x)))


def _in_proj_body(x_ref, g_ref, w_ref, ws_ref, o_ref, small_ref, h_ref):
    @pl.when(pl.program_id(1) == 0)
    def _():
        h_ref[...] = _rms(x_ref[...], g_ref[...]).astype(BF16)
        small_ref[...] = _mm(h_ref[...], ws_ref[...])

    h = h_ref[...]
    for c in range(0, o_ref.shape[0], 2):
        res = _mm(h, w_ref[:, c * LANES:(c + 2) * LANES])
        o_ref[c] = res[:, :LANES].astype(o_ref.dtype)
        o_ref[c + 1] = res[:, LANES:].astype(o_ref.dtype)


def _in_proj(x3d, gain, w, w_small, tn):
    b, s, d = x3d.shape
    n = w.shape[1]
    return pl.pallas_call(
        _in_proj_body,
        grid=(b, n // tn),
        in_specs=[pl.BlockSpec((None, s, d), lambda i, j: (i, 0, 0)),
                  pl.BlockSpec((1, d), lambda i, j: (0, 0)),
                  pl.BlockSpec((d, tn), lambda i, j: (0, j)),
                  pl.BlockSpec((d, LANES), lambda i, j: (0, 0))],
        out_specs=[pl.BlockSpec((None, tn // LANES, s, LANES), lambda i, j: (i, j, 0, 0)),
                   pl.BlockSpec((None, s, LANES), lambda i, j: (i, 0, 0))],
        out_shape=[jax.ShapeDtypeStruct((b, n // LANES, s, LANES), BF16),
                   jax.ShapeDtypeStruct((b, s, LANES), F32)],
        scratch_shapes=[pltpu.VMEM((s, d), BF16)],
        compiler_params=pltpu.CompilerParams(dimension_semantics=("parallel", "arbitrary"),
                                             vmem_limit_bytes=VMEM_LIMIT),
        name="in_proj",
    )(x3d, gain, w, w_small)


def _split_bf16(x):
    hi = x.astype(BF16)
    return hi, (x - hi.astype(F32)).astype(BF16)


def _split3_lanes(x):
    hi = x.astype(BF16)
    r = x - hi.astype(F32)
    mid = r.astype(BF16)
    lo = (r - mid.astype(F32)).astype(BF16)
    return jnp.concatenate([hi, mid, lo], axis=1)


def _block_diag(x, groups):
    return jnp.concatenate([jnp.where(g, x, 0.0) for g in groups], axis=0)


def _mm_parts(lhs, rhs, masks):
    (lh, ll), (rh, rl) = lhs, rhs
    bd = lambda p: jnp.concatenate([p * m for m in masks], axis=0)
    rh_bd = bd(rh)
    return (_mm(jnp.concatenate([lh, ll], axis=1), jnp.concatenate([rh_bd, rh_bd], axis=0))
            + _mm(lh, bd(rl)))


def _unit_tri_inverse_pairs(a2s, eye, lo, same_blk):
    masks = [jnp.where(g, 1.0, 0.0).astype(BF16) for g in lo]
    stack = lambda a, b: tuple(jnp.concatenate([p, q], axis=0) for p, q in zip(a, b))
    ds = [jnp.where(same_blk[0], a2, 0.0) for a2 in a2s]
    dps = [_split_bf16(d) for d in ds]
    xs = [_mm_parts(dp, dp, masks) for dp in dps]
    ts = [eye - d for d in ds]
    levels = int(math.log2(TRI_BASE)) - 1
    for lvl in range(levels):
        tps = [_split_bf16(t) for t in ts]
        xps = [_split_bf16(x) for x in xs]
        if lvl < levels - 1:
            outs = [_mm_parts(stack(tp, xp), xp, masks) for tp, xp in zip(tps, xps)]
            ts = [t + out[:CHUNK] for t, out in zip(ts, outs)]
            xs = [out[CHUNK:] for out in outs]
        else:
            ts = [t + _mm_parts(tp, xp, masks) for t, tp, xp in zip(ts, tps, xps)]
    inner = same_blk[0]
    for outer in tuple(same_blk[1:]) + (None,):
        between = [jnp.where(inner, 0.0, a2) if outer is None else jnp.where(outer & ~inner, a2, 0.0)
                   for a2 in a2s]
        tps = [_split_bf16(t) for t in ts]
        tls = [_mm_parts(tp, _split_bf16(l), masks) for tp, l in zip(tps, between)]
        ts = [t - _mm_parts(_split_bf16(tl), tp, masks) for t, tl, tp in zip(ts, tls, tps)]
        inner = outer
    return ts


def _gdn_body(lanepar_ref, cwq_ref, cwk_ref, cwv_ref, q_ref, k_ref, v_ref, z_ref, small_ref,
              na_ref, o_ref,
              xp, qn, kn, vn, act, gcf, bcf, mq_s, c_s, dec_s, of_s, ob_s):
    nh, s, _ = q_ref.shape
    nchunk = s // CHUNK
    hp = pl.program_id(1)
    rc = 256

    @pl.when(hp == 0)
    def _():
        lane = lax.broadcasted_iota(jnp.int32, (rc, LANES), 1)
        bi = lax.broadcasted_iota(jnp.int32, (rc, rc), 0)
        bj = lax.broadcasted_iota(jnp.int32, (rc, rc), 1)
        same_chunk = jnp.bitwise_xor(bi, bj) < CHUNK
        cum_f = jnp.where(same_chunk & (bj <= bi), 1.0, 0.0).astype(BF16)
        cum_b = jnp.where(same_chunk & (bj >= bi), 1.0, 0.0).astype(BF16)
        neg_a = -jnp.exp(lanepar_ref[0:1, :])
        dtb = lanepar_ref[1:2, :]
        sum3 = lambda y: y[:, :LANES] + y[:, LANES:2 * LANES] + y[:, 2 * LANES:]
        for c in range(s // rc):
            rows = pl.ds(c * rc, rc)
            slab = small_ref[rows, :]
            g3 = _split3_lanes(neg_a * _softplus(slab + dtb))
            act[rows, :] = jnp.where(lane < GDN_HEADS, sum3(_mm(cum_f, g3)),
                                     jnp.where(lane < 2 * GDN_HEADS, sum3(_mm(cum_b, g3)), _sigmoid(slab)))

    for hh in range(nh):
        lanes = pl.ds(hh * HEAD_DIM, HEAD_DIM)
        _gdn_prepare(hp * nh + hh, cwq_ref.at[:, lanes], cwk_ref.at[:, lanes], cwv_ref.at[:, lanes],
                     q_ref.at[hh], k_ref.at[hh], v_ref.at[hh],
                     xp, qn, kn, vn, act, gcf, bcf,
                     mq_s.at[hh], c_s.at[hh], dec_s.at[hh], of_s.at[hh], ob_s.at[hh])

    def phase2(step, states):
        new = []
        for idx, st in enumerate(states):
            hh, d = divmod(idx, 2)
            n = step if d == 0 else nchunk - 1 - step
            r = _mm(mq_s[hh, d, n], st.astype(BF16))
            rows = pl.ds(pl.multiple_of(n * CHUNK, CHUNK), CHUNK)
            out_s = of_s if d == 0 else ob_s
            out_s[hh, rows, :] = out_s[hh, rows, :] + r[HEAD_DIM:]
            new.append(st * dec_s[hh, d, n] + r[:HEAD_DIM] + c_s[hh, d, n])
        return tuple(new)

    zero_state = jnp.zeros((HEAD_DIM, HEAD_DIM), F32)
    lax.fori_loop(0, nchunk, phase2, (zero_state,) * (2 * nh))

    for hh in range(nh):
        for c in range(s // rc):
            rows = pl.ds(c * rc, rc)
            o = of_s[hh, rows, :] + ob_s[hh, rows, :]
            z = z_ref[hh, rows, :].astype(F32)
            o_ref[hh, rows, :] = (_rms(o, na_ref[...]) * _silu(z)).astype(o_ref.dtype)


def _gdn_prepare(h, cwq_ref, cwk_ref, cwv_ref, q_ref, k_ref, v_ref,
                 xp, qn, kn, vn, act, gcf, bcf, mq_s, c_s, dec_s, of_s, ob_s):
    s = q_ref.shape[0]
    nchunk = s // CHUNK
    rc = 256

    zeros8 = jnp.zeros((8, HEAD_DIM), F32)
    xp[pl.ds(0, 8), :] = zeros8
    xp[pl.ds(8 + s, 8), :] = zeros8

    def conv_into(src_ref, cw_ref, dst_ref, l2, scale):
        xp[pl.ds(8, s), :] = src_ref[...].astype(F32)
        for c in range(s // rc):
            acc = None
            for j in range(CONV_K):
                t = xp[pl.ds(c * rc + 8 - CONV_K // 2 + j, rc), :] * cw_ref[j:j + 1, :]
                acc = t if acc is None else acc + t
            y = _silu(acc)
            if l2:
                y = y * lax.rsqrt(jnp.sum(y * y, axis=-1, keepdims=True) + EPS)
                if scale != 1.0:
                    y = y * scale
            dst_ref[pl.ds(c * rc, rc), :] = y

    conv_into(q_ref, cwq_ref, qn, True, HEAD_DIM ** -0.5)
    conv_into(k_ref, cwk_ref, kn, True, 1.0)
    conv_into(v_ref, cwv_ref, vn, False, 1.0)

    sel_k = lax.broadcasted_iota(jnp.int32, (LANES, 4 * LANES), 0)
    sel_c = lax.broadcasted_iota(jnp.int32, (LANES, 4 * LANES), 1)
    sel = jnp.where(sel_k == (sel_c >> 7) * GDN_HEADS + h, 1.0, 0.0).astype(BF16)
    sel3 = jnp.concatenate([sel, sel, sel], axis=0)
    for c in range(s // rc):
        rows = pl.ds(c * rc, rc)
        picked = _mm(_split3_lanes(act[rows, :]), sel3)
        for d in range(2):
            gcf[d, rows, :] = picked[:, d * LANES:(d + 1) * LANES]
            bcf[d, rows, :] = picked[:, (2 + d) * LANES:(3 + d) * LANES]
    ii = lax.broadcasted_iota(jnp.int32, (CHUNK, 2 * CHUNK), 0)
    lj = lax.broadcasted_iota(jnp.int32, (CHUNK, 2 * CHUNK), 1)
    lo = lj < CHUNK
    hi = lj >= CHUNK
    jj = jnp.where(lo, lj, lj - CHUNK)
    is_diag = ii == jj
    tri2 = (lo & (ii >= jj)) | (hi & (ii <= jj))
    strict2 = (lo & (ii > jj)) | (hi & (ii < jj))
    pair = (lo, hi)
    lo_t = lax.broadcasted_iota(jnp.int32, (2 * CHUNK, 2 * CHUNK), 1) < CHUNK
    pair_t = (lo_t, ~lo_t)
    eye2 = jnp.where(is_diag, 1.0, 0.0)
    blk_bits = range(int(math.log2(TRI_BASE)), int(math.log2(CHUNK)))
    same_blk = tuple((ii >> sh) == (jj >> sh) for sh in blk_bits)

    def phase1(grp, carry):
        ns = [grp * CHUNK_GROUP + i for i in range(CHUNK_GROUP)]
        rws = [pl.ds(pl.multiple_of(n * CHUNK, CHUNK), CHUNK) for n in ns]
        qv = [qn[r, :] for r in rws]
        kv = [kn[r, :] for r in rws]
        kb = [k.astype(BF16) for k in kv]
        grams = [_mm_nt(jnp.concatenate([kb_, q.astype(BF16)], axis=0), jnp.concatenate([kb_, kb_], axis=0))
                 for kb_, q in zip(kb, qv)]
        gcc = [(gcf[0, r, :], gcf[1, r, :]) for r in rws]
        bcc = [(bcf[0, r, :], bcf[1, r, :]) for r in rws]
        gcc2 = [jnp.where(lo, gf, gb) for gf, gb in gcc]
        gcr2 = [jnp.sum(jnp.where(is_diag, g2, 0.0), axis=0, keepdims=True) for g2 in gcc2]
        gams = [jnp.where(tri2, jnp.exp(jnp.where(tri2, g2 - gr, 0.0)), 0.0) for g2, gr in zip(gcc2, gcr2)]
        a2s = [jnp.where(strict2, gram[:CHUNK] * jnp.where(lo, bf, bb) * gam, 0.0)
               for gram, (bf, bb), gam in zip(grams, bcc, gams)]
        t2s = _unit_tri_inverse_pairs(a2s, eye2, pair, same_blk)
        uws = []
        for i, r in enumerate(rws):
            v = vn[r, :]
            (gf, gb), (bf, bb) = gcc[i], bcc[i]
            rhs = jnp.concatenate([jnp.concatenate([v * bf, kv[i] * (bf * jnp.exp(gf))], axis=1),
                                   jnp.concatenate([v * bb, kv[i] * (bb * jnp.exp(gb))], axis=1)], axis=0)
            uws.append(_mm(_block_diag(t2s[i], pair).astype(BF16), rhs.astype(BF16)))
        fins = []
        for i in range(CHUNK_GROUP):
            gf, gb = gcc[i]
            gtot_f = gcr2[i][:, CHUNK - 1:CHUNK]
            gtot_b = gcr2[i][:, CHUNK:CHUNK + 1]
            kg = jnp.concatenate([kv[i] * jnp.exp(gtot_f - gf), kv[i] * jnp.exp(gtot_b - gb)], axis=0)
            lhs = jnp.concatenate([_block_diag(kg.T, pair_t), _block_diag(grams[i][CHUNK:] * gams[i], pair)], axis=0)
            fins.append(_mm(lhs.astype(BF16), uws[i].astype(BF16)))
            dec_s[0, ns[i]] = jnp.broadcast_to(jnp.exp(gtot_f), (1, HEAD_DIM))
            dec_s[1, ns[i]] = jnp.broadcast_to(jnp.exp(gtot_b), (1, HEAD_DIM))
        for i, (n, r) in enumerate(zip(ns, rws)):
            fin = fins[i]
            for d in range(2):
                top = fin[d * HEAD_DIM:(d + 1) * HEAD_DIM]
                bot = fin[2 * HEAD_DIM + d * CHUNK:2 * HEAD_DIM + (d + 1) * CHUNK]
                c_s[d, n] = top[:, :HEAD_DIM]
                mq_s[d, n, pl.ds(0, HEAD_DIM), :] = (-top[:, HEAD_DIM:]).astype(BF16)
                mq_s[d, n, pl.ds(HEAD_DIM, CHUNK), :] = (qv[i] * jnp.exp(gcc[i][d]) - bot[:, HEAD_DIM:]).astype(BF16)
                (of_s if d == 0 else ob_s)[r, :] = bot[:, :HEAD_DIM]
        return carry

    lax.fori_loop(0, nchunk // CHUNK_GROUP, phase1, 0)


def _gdn(proj, small, lanepar, conv_w, norm_a):
    b, _, s, _ = proj.shape
    nchunk = s // CHUNK
    nh = GDN_HEADS_PER_STEP
    assert 2 * CHUNK == LANES == HEAD_DIM and nchunk % CHUNK_GROUP == 0 and GDN_HEADS % nh == 0
    assert all(cb % nh == 0 for cb in (CB_QA, CB_KA, CB_VA, CB_ZA))
    col = lambda cb: pl.BlockSpec((None, nh, s, HEAD_DIM), lambda i, h, cb=cb: (i, cb // nh + h, 0, 0))
    cw = lambda cb: pl.BlockSpec((CONV_K, nh * HEAD_DIM), lambda i, h, cb=cb: (0, cb // nh + h))
    return pl.pallas_call(
        _gdn_body,
        grid=(b, GDN_HEADS // nh),
        in_specs=[pl.BlockSpec((8, LANES), lambda i, h: (0, 0)),
                  cw(0), cw(GDN_HEADS), cw(2 * GDN_HEADS),
                  col(CB_QA), col(CB_KA), col(CB_VA), col(CB_ZA),
                  pl.BlockSpec((None, s, LANES), lambda i, h: (i, 0, 0)),
                  pl.BlockSpec((1, HEAD_DIM), lambda i, h: (0, 0))],
        out_specs=pl.BlockSpec((None, nh, s, HEAD_DIM), lambda i, h: (i, h, 0, 0)),
        out_shape=jax.ShapeDtypeStruct((b, GDN_HEADS, s, HEAD_DIM), BF16),
        scratch_shapes=[
            pltpu.VMEM((s + 16, HEAD_DIM), F32),
            pltpu.VMEM((s, HEAD_DIM), F32),
            pltpu.VMEM((s, HEAD_DIM), F32),
            pltpu.VMEM((s, HEAD_DIM), F32),
            pltpu.VMEM((s, LANES), F32),
            pltpu.VMEM((2, s, LANES), F32),
            pltpu.VMEM((2, s, LANES), F32),
            pltpu.VMEM((nh, 2, nchunk, HEAD_DIM + CHUNK, HEAD_DIM), BF16),
            pltpu.VMEM((nh, 2, nchunk, HEAD_DIM, HEAD_DIM), F32),
            pltpu.VMEM((nh, 2, nchunk, 1, HEAD_DIM), F32),
            pltpu.VMEM((nh, s, HEAD_DIM), F32),
            pltpu.VMEM((nh, s, HEAD_DIM), F32),
        ],
        compiler_params=pltpu.CompilerParams(dimension_semantics=("parallel", "arbitrary"),
                                             vmem_limit_bytes=VMEM_LIMIT),
        name="gdn",
    )(lanepar, conv_w, conv_w, conv_w, proj, proj, proj, proj, small, norm_a)


def _dil_body(q0, k0, v0, q1, k1, v1, q2, k2, v2, bias_ref, o_ref,
              qf, kf, vf, qs, kp, vp, kpb, vpb, acc_s, m_s, l_s):
    s = q0.shape[0]
    scale = HEAD_DIM ** -0.5
    qi = lax.broadcasted_iota(jnp.int32, (QBLK, KWIN), 0)
    kj = lax.broadcasted_iota(jnp.int32, (QBLK, KWIN), 1)
    off = kj - HALF_WINDOW - qi
    band = (off >= -HALF_WINDOW) & (off <= HALF_WINDOW)
    zpad = jnp.zeros((HALF_WINDOW, HEAD_DIM), F32)

    for g, (q_ref, k_ref, v_ref) in enumerate(((q0, k0, v0), (q1, k1, v1), (q2, k2, v2))):
        dil = DIL_GROUPS[g][1]
        length = s // dil
        nblk = length // QBLK
        seg_q = max(length, QBLK)
        seg_k = seg_q + 2 * HALF_WINDOW
        nseg = min(dil, max(1, ATT_GROUP // nblk))
        bpi = min(nblk, ATT_GROUP)
        if dil > 1:
            qf[...] = q_ref[...].astype(F32)
            kf[...] = k_ref[...].astype(F32)
            vf[...] = v_ref[...].astype(F32)
            q_src, k_src, v_src = qs, kp, vp
        else:
            q_src, k_src, v_src = q_ref, kpb, vpb
            kpb[pl.ds(HALF_WINDOW, length), :] = k_ref[...]
            vpb[pl.ds(HALF_WINDOW, length), :] = v_ref[...]
        for c in range(nseg):
            for buf in (k_src, v_src):
                buf[pl.ds(c * seg_k, HALF_WINDOW), :] = zpad.astype(buf.dtype)
                buf[pl.ds(c * seg_k + HALF_WINDOW + length, HALF_WINDOW), :] = zpad.astype(buf.dtype)
        bias = pltpu.roll(jnp.broadcast_to(bias_ref[g], (QBLK, BIAS_LEN)), 0, 1,
                          stride=1, stride_axis=0)[:, :KWIN]

        def stage(c, r, dil=dil, length=length, seg_q=seg_q, seg_k=seg_k):
            if dil == 1:
                return
            cls = pl.ds(r, length, stride=dil)
            qs[pl.ds(c * seg_q, length), :] = qf[cls, :]
            kp[pl.ds(c * seg_k + HALF_WINDOW, length), :] = kf[cls, :]
            vp[pl.ds(c * seg_k + HALF_WINDOW, length), :] = vf[cls, :]

        def attend(items, g=g, dil=dil, length=length, bias=bias, q_src=q_src, k_src=k_src, v_src=v_src):
            logits = []
            for q_off, k_off, j0, r in items:
                kpos = j0 - HALF_WINDOW + kj
                ok = band & (kpos >= 0) & (kpos < length)
                sc = _mm_nt(q_src[pl.ds(q_off, QBLK), :].astype(BF16), k_src[pl.ds(k_off, KWIN), :].astype(BF16))
                logits.append(jnp.where(ok, sc * scale + bias, NEG))
            ms = [jnp.max(lg, axis=-1, keepdims=True) for lg in logits]
            ps = [jnp.exp(lg - m) for lg, m in zip(logits, ms)]
            ls = [jnp.sum(p, axis=-1, keepdims=True) for p in ps]
            accs = [_mm(p.astype(BF16), v_src[pl.ds(it[1], KWIN), :].astype(BF16)) for p, it in zip(ps, items)]
            for (q_off, k_off, j0, r), m, lsum, acc in zip(items, ms, ls, accs):
                tok = pl.ds(r + dil * j0, QBLK, stride=dil) if dil > 1 else pl.ds(j0, QBLK)
                acc_s[g, tok, :] = acc
                m_s[g, tok, :] = m
                l_s[g, tok, :] = lsum

        def classes(it, carry, nblk=nblk, nseg=nseg, bpi=bpi, seg_q=seg_q, seg_k=seg_k,
                    stage=stage, attend=attend):
            rs = [it * nseg + c for c in range(nseg)]
            for c, r in enumerate(rs):
                stage(c, r)

            def blocks(jt, carry2):
                items = []
                for c, r in enumerate(rs):
                    for i in range(bpi):
                        j0 = (jt * bpi + i) * QBLK
                        if not isinstance(j0, int):
                            j0 = pl.multiple_of(j0, QBLK)
                        items.append((c * seg_q + j0, c * seg_k + j0, j0, r))
                attend(items)
                return carry2

            if nblk == bpi:
                blocks(0, 0)
            else:
                lax.fori_loop(0, nblk // bpi, blocks, 0)
            return carry

        lax.fori_loop(0, dil // nseg, classes, 0)

    rc = 256
    for c in range(s // rc):
        rows = pl.ds(c * rc, rc)
        ms = [m_s[g, rows, :] for g in range(3)]
        mx = jnp.maximum(jnp.maximum(ms[0], ms[1]), ms[2])
        ws = [jnp.exp(ms[g] - mx) for g in range(3)]
        den = ws[0] * l_s[0, rows, :] + ws[1] * l_s[1, rows, :] + ws[2] * l_s[2, rows, :]
        num = ws[0] * acc_s[0, rows, :] + ws[1] * acc_s[1, rows, :] + ws[2] * acc_s[2, rows, :]
        o_ref[rows, :] = (num * (1.0 / den)).astype(o_ref.dtype)


def _dilated(proj, bias):
    b, _, s, _ = proj.shape
    specs = []
    for g in range(3):
        for cb in (CB_QB, CB_KB, CB_VB):
            specs.append(pl.BlockSpec((None, None, s, HEAD_DIM),
                                      lambda i, h, cb=cb, g=g: (i, cb + g * DIL_HEADS_PER_GROUP + h, 0, 0)))
    specs.append(pl.BlockSpec((None, 3, 1, BIAS_LEN), lambda i, h: (h, 0, 0, 0)))
    return pl.pallas_call(
        _dil_body,
        grid=(b, DIL_HEADS_PER_GROUP),
        in_specs=specs,
        out_specs=pl.BlockSpec((None, None, s, HEAD_DIM), lambda i, h: (i, h, 0, 0)),
        out_shape=jax.ShapeDtypeStruct((b, DIL_HEADS_PER_GROUP, s, HEAD_DIM), BF16),
        scratch_shapes=[
            pltpu.VMEM((s, HEAD_DIM), F32),
            pltpu.VMEM((s, HEAD_DIM), F32),
            pltpu.VMEM((s, HEAD_DIM), F32),
            pltpu.VMEM((s, HEAD_DIM), F32),
            pltpu.VMEM((s + 2 * HALF_WINDOW, HEAD_DIM), F32),
            pltpu.VMEM((s + 2 * HALF_WINDOW, HEAD_DIM), F32),
            pltpu.VMEM((s + 2 * HALF_WINDOW, HEAD_DIM), BF16),
            pltpu.VMEM((s + 2 * HALF_WINDOW, HEAD_DIM), BF16),
            pltpu.VMEM((3, s, HEAD_DIM), F32),
            pltpu.VMEM((3, s, 1), F32),
            pltpu.VMEM((3, s, 1), F32),
        ],
        compiler_params=pltpu.CompilerParams(dimension_semantics=("parallel", "arbitrary"),
                                             vmem_limit_bytes=VMEM_LIMIT),
        name="dilated_attn",
    )(*([proj] * 9), bias)


def _mix_out_body(x_ref, oa_ref, ob_ref, ga_ref, gb_ref, wa_ref, wb_ref, wo_ref, ln_ref, o_ref):
    wide = lambda ref: jnp.concatenate([ref[c] for c in range(ref.shape[0])], axis=1)
    ya = _mm(wide(oa_ref), wa_ref[...])
    yb = _mm(wide(ob_ref), wb_ref[...])
    merged = _sigmoid(wide(ga_ref).astype(F32)) * ya + _sigmoid(wide(gb_ref).astype(F32)) * yb
    y = _mm(merged.astype(BF16), wo_ref[...])
    o_ref[...] = x_ref[...] + _rms(y, ln_ref[...])


def _mix_out(x3d, o_a, o_b, proj, w_a, w_b, w_o, ln_post, tm):
    b, s, d = x3d.shape
    nslab = d // LANES
    assert CB_GA % nslab == 0 and CB_GB % nslab == 0 and s % tm == 0
    const = lambda shape: pl.BlockSpec(shape, lambda i, j: (0, 0), pipeline_mode=pl.Buffered(1))
    slabs =lambda n, blk: pl.BlockSpec((None, n, tm, LANES), lambda i, j, blk=blk: (i, blk, j, 0))
    return pl.pallas_call(
        _mix_out_body,
        grid=(b, s // tm),
        in_specs=[pl.BlockSpec((None, tm, d), lambda i, j: (i, j, 0)),
                  slabs(GDN_HEADS, 0), slabs(DIL_HEADS_PER_GROUP, 0),
                  slabs(nslab, CB_GA // nslab), slabs(nslab, CB_GB // nslab),
                  const((GDN_WIDTH, d)), const((DIL_OUT_WIDTH, d)), const((d, d)), const((1, d))],
        out_specs=pl.BlockSpec((None, tm, d), lambda i, j: (i, j, 0)),
        out_shape=jax.ShapeDtypeStruct((b, s, d), F32),
        compiler_params=pltpu.CompilerParams(dimension_semantics=("parallel", "parallel"),
                                             vmem_limit_bytes=VMEM_LIMIT),
        name="mix_out",
    )(x3d, o_a, o_b, proj, proj, w_a, w_b, w_o, ln_post)


def _mlp_body(tf, x_ref, g1_ref, w1_ref, w2_ref, g2_ref, o_ref):
    x = x_ref[...]
    h = _rms(x, g1_ref[...]).astype(BF16)
    acc = None
    for k in range(w1_ref.shape[1] // tf):
        cols = pl.ds(k * tf, tf)
        f = jnp.maximum(_mm(h, w1_ref[:, cols]), 0.0)
        part = _mm((f * f).astype(BF16), w2_ref[cols, :])
        acc = part if acc is None else acc + part
    o_ref[...] = x + _rms(acc, g2_ref[...])


def _mlp(x2d, ln_pre, w1, w2, ln_post, tm, tf):
    t, d = x2d.shape
    dff = w1.shape[1]
    resident = lambda shape: pl.BlockSpec(shape, lambda i: (0, 0), pipeline_mode=pl.Buffered(1))
    return pl.pallas_call(
        functools.partial(_mlp_body, tf),
        grid=(t // tm,),
        in_specs=[pl.BlockSpec((tm, d), lambda i: (i, 0)),
                  resident((1, d)), resident((d, dff)), resident((dff, d)), resident((1, d))],
        out_specs=pl.BlockSpec((tm, d), lambda i: (i, 0)),
        out_shape=jax.ShapeDtypeStruct((t, d), F32),
        compiler_params=pltpu.CompilerParams(dimension_semantics=("parallel",),
                                             vmem_limit_bytes=VMEM_LIMIT),
        name="mlp",
    )(x2d, ln_pre, w1, w2, ln_post)


def _t5_bucket_np(rel):
    nb = REL_BUCKETS // 2
    ret = (rel > 0).astype(np.int32) * nb
    n = np.abs(rel)
    max_exact = nb // 2
    large = max_exact + (np.log(np.maximum(n, 1) / max_exact) / math.log(REL_MAX_DIST / max_exact)
                         * (nb - max_exact)).astype(np.int32)
    large = np.minimum(large, nb - 1)
    return ret + np.where(n < max_exact, n, large).astype(np.int32)


def _attention_bias(rel_bias):
    c = np.arange(BIAS_LEN)
    off = np.where(c < KWIN, np.clip(c - HALF_WINDOW, -HALF_WINDOW, HALF_WINDOW), -HALF_WINDOW)
    per_group = []
    for gi, (_, dil) in enumerate(DIL_GROUPS):
        bt = rel_bias[_t5_bucket_np(off * dil)]
        per_group.append(bt[:, gi * DIL_HEADS_PER_GROUP:(gi + 1) * DIL_HEADS_PER_GROUP])
    return jnp.transpose(jnp.stack(per_group, axis=0), (2, 0, 1))[:, :, None, :].astype(F32)


def _gdn_params(a_log_f, a_log_b, dt_bias_f, dt_bias_b):
    pad = lambda f, bk: jnp.pad(jnp.concatenate([f, bk]), (0, LANES - 2 * GDN_HEADS))
    return jnp.stack([pad(a_log_f, a_log_b), pad(dt_bias_f, dt_bias_b)]
                     + [jnp.zeros((LANES,), F32)] * 6, axis=0).astype(F32)


def kernel(x, rel_bias, ln_mix_pre, w_in, conv_w, a_log_f, a_log_b, dt_bias_f, dt_bias_b, norm_a,
           w_branch_a, w_branch_b, w_out, ln_mix_post, ln_mlp_pre, w_ff1, w_ff2, ln_mlp_post):
    b, s, d = x.shape
    t = b * s
    c_small = 4 * GDN_WIDTH
    bias = _attention_bias(rel_bias)
    for l in range(ln_mix_pre.shape[0]):
        w = w_in[l].astype(BF16)
        c_qb = c_small + N_SMALL
        c_ga = c_qb + 3 * DIL_WIDTH
        w_wide = jnp.concatenate([w[:, :c_small], w[:, c_ga:], w[:, c_qb:c_ga]], axis=1)
        w_small = jnp.pad(w[:, c_small:c_qb], ((0, 0), (0, LANES - N_SMALL)))
        proj, small = _in_proj(x, ln_mix_pre[l][None, :], w_wide, w_small, 1536)

        lanepar = _gdn_params(a_log_f[l], a_log_b[l], dt_bias_f[l], dt_bias_b[l])
        o_a = _gdn(proj, small, lanepar, conv_w[l].astype(F32), norm_a[l][None, :].astype(F32))
        o_b = _dilated(proj, bias)

        x1 = _mix_out(x, o_a, o_b, proj,
                      w_branch_a[l].astype(BF16), w_branch_b[l].astype(BF16), w_out[l].astype(BF16),
                      ln_mix_post[l][None, :], 1024)
        x2d = _mlp(x1.reshape(t, d), ln_mlp_pre[l][None, :], w_ff1[l].astype(BF16), w_ff2[l].astype(BF16),
                   ln_mlp_post[l][None, :], 1024, 1024)
        x = x2d.reshape(b, s, d)
    return x
```

```python
import functools
import math

import numpy as np
import jax
import jax.numpy as jnp
from jax import lax
from jax.experimental import pallas as pl
from jax.experimental.pallas import tpu as pltpu

F32 = jnp.float32
BF16 = jnp.bfloat16

LANES = 128
VMEM_LIMIT = 56 * 1024 * 1024

GDN_HEADS = 8
HEAD_DIM = 128
GDN_WIDTH = GDN_HEADS * HEAD_DIM
CHUNK = 64
TRI_BASE = 8
CHUNK_GROUP = 16
GDN_HEADS_PER_STEP = 2
CONV_K = 5
DIL_GROUPS = ((128, 1), (512, 4), (2048, 16))
DIL_HEADS_PER_GROUP = 4
DIL_WIDTH = len(DIL_GROUPS) * DIL_HEADS_PER_GROUP * HEAD_DIM
DIL_OUT_WIDTH = DIL_HEADS_PER_GROUP * HEAD_DIM
HALF_WINDOW = 64
QBLK = 128
KWIN = QBLK + 2 * HALF_WINDOW
ATT_GROUP = 8
BIAS_LEN = 3 * LANES
REL_BUCKETS = 32
REL_MAX_DIST = 1024
EPS = 1e-6
NEG = -1e30
N_SMALL = 4 * GDN_HEADS

CB_QA, CB_KA, CB_VA, CB_ZA = 0, 8, 16, 24
CB_GA, CB_GB = 32, 40
CB_QB, CB_KB, CB_VB = 48, 60, 72


def _mm(a, b):
    return jnp.dot(a, b, preferred_element_type=F32)


def _mm_nt(a, b):
    return lax.dot_general(a, b, (((1,), (1,)), ((), ())), preferred_element_type=F32)


def _rms(x, gain):
    return x * lax.rsqrt(jnp.mean(x * x, axis=-1, keepdims=True) + EPS) * gain


def _sigmoid(x):
    return 0.5 * jnp.tanh(0.5 * x) + 0.5


def _silu(x):
    h = 0.5 * x
    return h + h * jnp.tanh(h)


def _softplus(x):
    return jnp.maximum(x, 0.0) + jnp.log(1.0 + jnp.exp(-jnp.abs(x)))


def _in_proj_body(x_ref, g_ref, w_ref, ws_ref, o_ref, small_ref, h_ref):
    @pl.when(pl.program_id(1) == 0)
    def _():
        h_ref[...] = _rms(x_ref[...], g_ref[...]).astype(BF16)
        small_ref[...] = _mm(h_ref[...], ws_ref[...])

    h = h_ref[...]
    for c in range(0, o_ref.shape[0], 2):
        res = _mm(h, w_ref[:, c * LANES:(c + 2) * LANES])
        o_ref[c] = res[:, :LANES].astype(o_ref.dtype)
        o_ref[c + 1] = res[:, LANES:].astype(o_ref.dtype)


def _in_proj(x3d, gain, w, w_small, tn):
    b, s, d = x3d.shape
    n = w.shape[1]
    return pl.pallas_call(
        _in_proj_body,
        grid=(b, n // tn),
        in_specs=[pl.BlockSpec((None, s, d), lambda i, j: (i, 0, 0)),
                  pl.BlockSpec((1, d), lambda i, j: (0, 0)),
                  pl.BlockSpec((d, tn), lambda i, j: (0, j)),
                  pl.BlockSpec((d, LANES), lambda i, j: (0, 0))],
        out_specs=[pl.BlockSpec((None, tn // LANES, s, LANES), lambda i, j: (i, j, 0, 0)),
                   pl.BlockSpec((None, s, LANES), lambda i, j: (i, 0, 0))],
        out_shape=[jax.ShapeDtypeStruct((b, n // LANES, s, LANES), BF16),
                   jax.ShapeDtypeStruct((b, s, LANES), F32)],
        scratch_shapes=[pltpu.VMEM((s, d), BF16)],
        compiler_params=pltpu.CompilerParams(dimension_semantics=("parallel", "arbitrary"),
                                             vmem_limit_bytes=VMEM_LIMIT),
        name="in_proj",
    )(x3d, gain, w, w_small)


def _split_bf16(x):
    hi = x.astype(BF16)
    return hi, (x - hi.astype(F32)).astype(BF16)


def _split3_lanes(x):
    hi = x.astype(BF16)
    r = x - hi.astype(F32)
    mid = r.astype(BF16)
    lo = (r - mid.astype(F32)).astype(BF16)
    return jnp.concatenate([hi, mid, lo], axis=1)


def _block_diag(x, groups):
    return jnp.concatenate([jnp.where(g, x, 0.0) for g in groups], axis=0)


def _mm_parts(lhs, rhs, masks):
    (lh, ll), (rh, rl) = lhs, rhs
    bd = lambda p: jnp.concatenate([p * m for m in masks], axis=0)
    rh_bd = bd(rh)
    return (_mm(jnp.concatenate([lh, ll], axis=1), jnp.concatenate([rh_bd, rh_bd], axis=0))
            + _mm(lh, bd(rl)))


def _unit_tri_inverse_pairs(a2s, eye, lo, same_blk):
    masks = [jnp.where(g, 1.0, 0.0).astype(BF16) for g in lo]
    stack = lambda a, b: tuple(jnp.concatenate([p, q], axis=0) for p, q in zip(a, b))
    ds = [jnp.where(same_blk[0], a2, 0.0) for a2 in a2s]
    dps = [_split_bf16(d) for d in ds]
    xs = [_mm_parts(dp, dp, masks) for dp in dps]
    ts = [eye - d for d in ds]
    levels = int(math.log2(TRI_BASE)) - 1
    for lvl in range(levels):
        tps = [_split_bf16(t) for t in ts]
        xps = [_split_bf16(x) for x in xs]
        if lvl < levels - 1:
            outs = [_mm_parts(stack(tp, xp), xp, masks) for tp, xp in zip(tps, xps)]
            ts = [t + out[:CHUNK] for t, out in zip(ts, outs)]
            xs = [out[CHUNK:] for out in outs]
        else:
            ts = [t + _mm_parts(tp, xp, masks) for t, tp, xp in zip(ts, tps, xps)]
    inner = same_blk[0]
    for outer in tuple(same_blk[1:]) + (None,):
        between = [jnp.where(inner, 0.0, a2) if outer is None else jnp.where(outer & ~inner, a2, 0.0)
                   for a2 in a2s]
        tps = [_split_bf16(t) for t in ts]
        tls = [_mm_parts(tp, _split_bf16(l), masks) for tp, l in zip(tps, between)]
        ts = [t - _mm_parts(_split_bf16(tl), tp, masks) for t, tl, tp in zip(ts, tls, tps)]
        inner = outer
    return ts


def _gdn_body(lanepar_ref, cwq_ref, cwk_ref, cwv_ref, q_ref, k_ref, v_ref, z_ref, small_ref,
              na_ref, o_ref,
              xp, qn, kn, vn, act, gcf, bcf, mq_s, c_s, dec_s, of_s, ob_s):
    nh, s, _ = q_ref.shape
    nchunk = s // CHUNK
    hp = pl.program_id(1)
    rc = 256

    @pl.when(hp == 0)
    def _():
        lane = lax.broadcasted_iota(jnp.int32, (rc, LANES), 1)
        bi = lax.broadcasted_iota(jnp.int32, (rc, rc), 0)
        bj = lax.broadcasted_iota(jnp.int32, (rc, rc), 1)
        same_chunk = jnp.bitwise_xor(bi, bj) < CHUNK
        cum_f = jnp.where(same_chunk & (bj <= bi), 1.0, 0.0).astype(BF16)
        cum_b = jnp.where(same_chunk & (bj >= bi), 1.0, 0.0).astype(BF16)
        neg_a = -jnp.exp(lanepar_ref[0:1, :])
        dtb = lanepar_ref[1:2, :]
        sum3 = lambda y: y[:, :LANES] + y[:, LANES:2 * LANES] + y[:, 2 * LANES:]
        for c in range(s // rc):
            rows = pl.ds(c * rc, rc)
            slab = small_ref[rows, :]
            g3 = _split3_lanes(neg_a * _softplus(slab + dtb))
            act[rows, :] = jnp.where(lane < GDN_HEADS, sum3(_mm(cum_f, g3)),
                                     jnp.where(lane < 2 * GDN_HEADS, sum3(_mm(cum_b, g3)), _sigmoid(slab)))

    for hh in range(nh):
        lanes = pl.ds(hh * HEAD_DIM, HEAD_DIM)
        _gdn_prepare(hp * nh + hh, cwq_ref.at[:, lanes], cwk_ref.at[:, lanes], cwv_ref.at[:, lanes],
                     q_ref.at[hh], k_ref.at[hh], v_ref.at[hh],
                     xp, qn, kn, vn, act, gcf, bcf,
                     mq_s.at[hh], c_s.at[hh], dec_s.at[hh], of_s.at[hh], ob_s.at[hh])

    def phase2(step, states):
        new = []
        for idx, st in enumerate(states):
            hh, d = divmod(idx, 2)
            n = step if d == 0 else nchunk - 1 - step
            r = _mm(mq_s[hh, d, n], st.astype(BF16))
            rows = pl.ds(pl.multiple_of(n * CHUNK, CHUNK), CHUNK)
            out_s = of_s if d == 0 else ob_s
            out_s[hh, rows, :] = out_s[hh, rows, :] + r[HEAD_DIM:]
            new.append(st * dec_s[hh, d, n] + r[:HEAD_DIM] + c_s[hh, d, n])
        return tuple(new)

    zero_state = jnp.zeros((HEAD_DIM, HEAD_DIM), F32)
    lax.fori_loop(0, nchunk, phase2, (zero_state,) * (2 * nh))

    for hh in range(nh):
        for c in range(s // rc):
            rows = pl.ds(c * rc, rc)
            o = of_s[hh, rows, :] + ob_s[hh, rows, :]
            z = z_ref[hh, rows, :].astype(F32)
            o_ref[hh, rows, :] = (_rms(o, na_ref[...]) * _silu(z)).astype(o_ref.dtype)


def _gdn_prepare(h, cwq_ref, cwk_ref, cwv_ref, q_ref, k_ref, v_ref,
                 xp, qn, kn, vn, act, gcf, bcf, mq_s, c_s, dec_s, of_s, ob_s):
    s = q_ref.shape[0]
    nchunk = s // CHUNK
    rc = 256

    zeros8 = jnp.zeros((8, HEAD_DIM), F32)
    xp[pl.ds(0, 8), :] = zeros8
    xp[pl.ds(8 + s, 8), :] = zeros8

    def conv_into(src_ref, cw_ref, dst_ref, l2, scale):
        xp[pl.ds(8, s), :] = src_ref[...].astype(F32)
        for c in range(s // rc):
            acc = None
            for j in range(CONV_K):
                t = xp[pl.ds(c * rc + 8 - CONV_K // 2 + j, rc), :] * cw_ref[j:j + 1, :]
                acc = t if acc is None else acc + t
            y = _silu(acc)
            if l2:
                y = y * lax.rsqrt(jnp.sum(y * y, axis=-1, keepdims=True) + EPS)
                if scale != 1.0:
                    y = y * scale
            dst_ref[pl.ds(c * rc, rc), :] = y

    conv_into(q_ref, cwq_ref, qn, True, HEAD_DIM ** -0.5)
    conv_into(k_ref, cwk_ref, kn, True, 1.0)
    conv_into(v_ref, cwv_ref, vn, False, 1.0)

    sel_k = lax.broadcasted_iota(jnp.int32, (LANES, 4 * LANES), 0)
    sel_c = lax.broadcasted_iota(jnp.int32, (LANES, 4 * LANES), 1)
    sel = jnp.where(sel_k == (sel_c >> 7) * GDN_HEADS + h, 1.0, 0.0).astype(BF16)
    sel3 = jnp.concatenate([sel, sel, sel], axis=0)
    for c in range(s // rc):
        rows = pl.ds(c * rc, rc)
        picked = _mm(_split3_lanes(act[rows, :]), sel3)
        for d in range(2):
            gcf[d, rows, :] = picked[:, d * LANES:(d + 1) * LANES]
            bcf[d, rows, :] = picked[:, (2 + d) * LANES:(3 + d) * LANES]
    ii = lax.broadcasted_iota(jnp.int32, (CHUNK, 2 * CHUNK), 0)
    lj = lax.broadcasted_iota(jnp.int32, (CHUNK, 2 * CHUNK), 1)
    lo = lj < CHUNK
    hi = lj >= CHUNK
    jj = jnp.where(lo, lj, lj - CHUNK)
    is_diag = ii == jj
    tri2 = (lo & (ii >= jj)) | (hi & (ii <= jj))
    strict2 = (lo & (ii > jj)) | (hi & (ii < jj))
    pair = (lo, hi)
    lo_t = lax.broadcasted_iota(jnp.int32, (2 * CHUNK, 2 * CHUNK), 1) < CHUNK
    pair_t = (lo_t, ~lo_t)
    eye2 = jnp.where(is_diag, 1.0, 0.0)
    blk_bits = range(int(math.log2(TRI_BASE)), int(math.log2(CHUNK)))
    same_blk = tuple((ii >> sh) == (jj >> sh) for sh in blk_bits)

    def phase1(grp, carry):
        ns = [grp * CHUNK_GROUP + i for i in range(CHUNK_GROUP)]
        rws = [pl.ds(pl.multiple_of(n * CHUNK, CHUNK), CHUNK) for n in ns]
        qv = [qn[r, :] for r in rws]
        kv = [kn[r, :] for r in rws]
        kb = [k.astype(BF16) for k in kv]
        grams = [_mm_nt(jnp.concatenate([kb_, q.astype(BF16)], axis=0), jnp.concatenate([kb_, kb_], axis=0))
                 for kb_, q in zip(kb, qv)]
        gcc = [(gcf[0, r, :], gcf[1, r, :]) for r in rws]
        bcc = [(bcf[0, r, :], bcf[1, r, :]) for r in rws]
        gcc2 = [jnp.where(lo, gf, gb) for gf, gb in gcc]
        gcr2 = [jnp.sum(jnp.where(is_diag, g2, 0.0), axis=0, keepdims=True) for g2 in gcc2]
        gams = [jnp.where(tri2, jnp.exp(jnp.where(tri2, g2 - gr, 0.0)), 0.0) for g2, gr in zip(gcc2, gcr2)]
        a2s = [jnp.where(strict2, gram[:CHUNK] * jnp.where(lo, bf, bb) * gam, 0.0)
               for gram, (bf, bb), gam in zip(grams, bcc, gams)]
        t2s = _unit_tri_inverse_pairs(a2s, eye2, pair, same_blk)
        uws = []
        for i, r in enumerate(rws):
            v = vn[r, :]
            (gf, gb), (bf, bb) = gcc[i], bcc[i]
            rhs = jnp.concatenate([jnp.concatenate([v * bf, kv[i] * (bf * jnp.exp(gf))], axis=1),
                                   jnp.concatenate([v * bb, kv[i] * (bb * jnp.exp(gb))], axis=1)], axis=0)
            uws.append(_mm(_block_diag(t2s[i], pair).astype(BF16), rhs.astype(BF16)))
        fins = []
        for i in range(CHUNK_GROUP):
            gf, gb = gcc[i]
            gtot_f = gcr2[i][:, CHUNK - 1:CHUNK]
            gtot_b = gcr2[i][:, CHUNK:CHUNK + 1]
            kg = jnp.concatenate([kv[i] * jnp.exp(gtot_f - gf), kv[i] * jnp.exp(gtot_b - gb)], axis=0)
            lhs = jnp.concatenate([_block_diag(kg.T, pair_t), _block_diag(grams[i][CHUNK:] * gams[i], pair)], axis=0)
            fins.append(_mm(lhs.astype(BF16), uws[i].astype(BF16)))
            dec_s[0, ns[i]] = jnp.broadcast_to(jnp.exp(gtot_f), (1, HEAD_DIM))
            dec_s[1, ns[i]] = jnp.broadcast_to(jnp.exp(gtot_b), (1, HEAD_DIM))
        for i, (n, r) in enumerate(zip(ns, rws)):
            fin = fins[i]
            for d in range(2):
                top = fin[d * HEAD_DIM:(d + 1) * HEAD_DIM]
                bot = fin[2 * HEAD_DIM + d * CHUNK:2 * HEAD_DIM + (d + 1) * CHUNK]
                c_s[d, n] = top[:, :HEAD_DIM]
                mq_s[d, n, pl.ds(0, HEAD_DIM), :] = (-top[:, HEAD_DIM:]).astype(BF16)
                mq_s[d, n, pl.ds(HEAD_DIM, CHUNK), :] = (qv[i] * jnp.exp(gcc[i][d]) - bot[:, HEAD_DIM:]).astype(BF16)
                (of_s if d == 0 else ob_s)[r, :] = bot[:, :HEAD_DIM]
        return carry

    lax.fori_loop(0, nchunk // CHUNK_GROUP, phase1, 0)


def _gdn(proj, small, lanepar, conv_w, norm_a):
    b, _, s, _ = proj.shape
    nchunk = s // CHUNK
    nh = GDN_HEADS_PER_STEP
    assert 2 * CHUNK == LANES == HEAD_DIM and nchunk % CHUNK_GROUP == 0 and GDN_HEADS % nh == 0
    assert all(cb % nh == 0 for cb in (CB_QA, CB_KA, CB_VA, CB_ZA))
    col = lambda cb: pl.BlockSpec((None, nh, s, HEAD_DIM), lambda i, h, cb=cb: (i, cb // nh + h, 0, 0))
    cw = lambda cb: pl.BlockSpec((CONV_K, nh * HEAD_DIM), lambda i, h, cb=cb: (0, cb // nh + h))
    return pl.pallas_call(
        _gdn_body,
        grid=(b, GDN_HEADS // nh),
        in_specs=[pl.BlockSpec((8, LANES), lambda i, h: (0, 0)),
                  cw(0), cw(GDN_HEADS), cw(2 * GDN_HEADS),
                  col(CB_QA), col(CB_KA), col(CB_VA), col(CB_ZA),
                  pl.BlockSpec((None, s, LANES), lambda i, h: (i, 0, 0)),
                  pl.BlockSpec((1, HEAD_DIM), lambda i, h: (0, 0))],
        out_specs=pl.BlockSpec((None, nh, s, HEAD_DIM), lambda i, h: (i, h, 0, 0)),
        out_shape=jax.ShapeDtypeStruct((b, GDN_HEADS, s, HEAD_DIM), BF16),
        scratch_shapes=[
            pltpu.VMEM((s + 16, HEAD_DIM), F32),
            pltpu.VMEM((s, HEAD_DIM), F32),
            pltpu.VMEM((s, HEAD_DIM), F32),
            pltpu.VMEM((s, HEAD_DIM), F32),
            pltpu.VMEM((s, LANES), F32),
            pltpu.VMEM((2, s, LANES), F32),
            pltpu.VMEM((2, s, LANES), F32),
            pltpu.VMEM((nh, 2, nchunk, HEAD_DIM + CHUNK, HEAD_DIM), BF16),
            pltpu.VMEM((nh, 2, nchunk, HEAD_DIM, HEAD_DIM), F32),
            pltpu.VMEM((nh, 2, nchunk, 1, HEAD_DIM), F32),
            pltpu.VMEM((nh, s, HEAD_DIM), F32),
            pltpu.VMEM((nh, s, HEAD_DIM), F32),
        ],
        compiler_params=pltpu.CompilerParams(dimension_semantics=("parallel", "arbitrary"),
                                             vmem_limit_bytes=VMEM_LIMIT),
        name="gdn",
    )(lanepar, conv_w, conv_w, conv_w, proj, proj, proj, proj, small, norm_a)


def _dil_body(q0, k0, v0, q1, k1, v1, q2, k2, v2, bias_ref, o_ref,
              qf, kf, vf, qs, kp, vp, kpb, vpb, acc_s, m_s, l_s):
    s = q0.shape[0]
    scale = HEAD_DIM ** -0.5
    qi = lax.broadcasted_iota(jnp.int32, (QBLK, KWIN), 0)
    kj = lax.broadcasted_iota(jnp.int32, (QBLK, KWIN), 1)
    off = kj - HALF_WINDOW - qi
    band = (off >= -HALF_WINDOW) & (off <= HALF_WINDOW)
    zpad = jnp.zeros((HALF_WINDOW, HEAD_DIM), F32)

    for g, (q_ref, k_ref, v_ref) in enumerate(((q0, k0, v0), (q1, k1, v1), (q2, k2, v2))):
        dil = DIL_GROUPS[g][1]
        length = s // dil
        nblk = length // QBLK
        seg_q = max(length, QBLK)
        seg_k = seg_q + 2 * HALF_WINDOW
        nseg = min(dil, max(1, ATT_GROUP // nblk))
        bpi = min(nblk, ATT_GROUP)
        if dil > 1:
            qf[...] = q_ref[...].astype(F32)
            kf[...] = k_ref[...].astype(F32)
            vf[...] = v_ref[...].astype(F32)
            q_src, k_src, v_src = qs, kp, vp
        else:
            q_src, k_src, v_src = q_ref, kpb, vpb
            kpb[pl.ds(HALF_WINDOW, length), :] = k_ref[...]
            vpb[pl.ds(HALF_WINDOW, length), :] = v_ref[...]
        for c in range(nseg):
            for buf in (k_src, v_src):
                buf[pl.ds(c * seg_k, HALF_WINDOW), :] = zpad.astype(buf.dtype)
                buf[pl.ds(c * seg_k + HALF_WINDOW + length, HALF_WINDOW), :] = zpad.astype(buf.dtype)
        bias = pltpu.roll(jnp.broadcast_to(bias_ref[g], (QBLK, BIAS_LEN)), 0, 1,
                          stride=1, stride_axis=0)[:, :KWIN]

        def stage(c, r, dil=dil, length=length, seg_q=seg_q, seg_k=seg_k):
            if dil == 1:
                return
            cls = pl.ds(r, length, stride=dil)
            qs[pl.ds(c * seg_q, length), :] = qf[cls, :]
            kp[pl.ds(c * seg_k + HALF_WINDOW, length), :] = kf[cls, :]
            vp[pl.ds(c * seg_k + HALF_WINDOW, length), :] = vf[cls, :]

        def attend(items, g=g, dil=dil, length=length, bias=bias, q_src=q_src, k_src=k_src, v_src=v_src):
            logits = []
            for q_off, k_off, j0, r in items:
                kpos = j0 - HALF_WINDOW + kj
                ok = band & (kpos >= 0) & (kpos < length)
                sc = _mm_nt(q_src[pl.ds(q_off, QBLK), :].astype(BF16), k_src[pl.ds(k_off, KWIN), :].astype(BF16))
                logits.append(jnp.where(ok, sc * scale + bias, NEG))
            ms = [jnp.max(lg, axis=-1, keepdims=True) for lg in logits]
            ps = [jnp.exp(lg - m) for lg, m in zip(logits, ms)]
            ls = [jnp.sum(p, axis=-1, keepdims=True) for p in ps]
            accs = [_mm(p.astype(BF16), v_src[pl.ds(it[1], KWIN), :].astype(BF16)) for p, it in zip(ps, items)]
            for (q_off, k_off, j0, r), m, lsum, acc in zip(items, ms, ls, accs):
                tok = pl.ds(r + dil * j0, QBLK, stride=dil) if dil > 1 else pl.ds(j0, QBLK)
                acc_s[g, tok, :] = acc
                m_s[g, tok, :] = m
                l_s[g, tok, :] = lsum

        def classes(it, carry, nblk=nblk, nseg=nseg, bpi=bpi, seg_q=seg_q, seg_k=seg_k,
                    stage=stage, attend=attend):
            rs = [it * nseg + c for c in range(nseg)]
            for c, r in enumerate(rs):
                stage(c, r)

            def blocks(jt, carry2):
                items = []
                for c, r in enumerate(rs):
                    for i in range(bpi):
                        j0 = (jt * bpi + i) * QBLK
                        if not isinstance(j0, int):
                            j0 = pl.multiple_of(j0, QBLK)
                        items.append((c * seg_q + j0, c * seg_k + j0, j0, r))
                attend(items)
                return carry2

            if nblk == bpi:
                blocks(0, 0)
            else:
                lax.fori_loop(0, nblk // bpi, blocks, 0)
            return carry

        lax.fori_loop(0, dil // nseg, classes, 0)

    rc = 256
    for c in range(s // rc):
        rows = pl.ds(c * rc, rc)
        ms = [m_s[g, rows, :] for g in range(3)]
        mx = jnp.maximum(jnp.maximum(ms[0], ms[1]), ms[2])
        ws = [jnp.exp(ms[g] - mx) for g in range(3)]
        den = ws[0] * l_s[0, rows, :] + ws[1] * l_s[1, rows, :] + ws[2] * l_s[2, rows, :]
        num = ws[0] * acc_s[0, rows, :] + ws[1] * acc_s[1, rows, :] + ws[2] * acc_s[2, rows, :]
        o_ref[rows, :] = (num * (1.0 / den)).astype(o_ref.dtype)


def _dilated(proj, bias):
    b, _, s, _ = proj.shape
    specs = []
    for g in range(3):
        for cb in (CB_QB, CB_KB, CB_VB):
            specs.append(pl.BlockSpec((None, None, s, HEAD_DIM),
                                      lambda i, h, cb=cb, g=g: (i, cb + g * DIL_HEADS_PER_GROUP + h, 0, 0)))
    specs.append(pl.BlockSpec((None, 3, 1, BIAS_LEN), lambda i, h: (h, 0, 0, 0)))
    return pl.pallas_call(
        _dil_body,
        grid=(b, DIL_HEADS_PER_GROUP),
        in_specs=specs,
        out_specs=pl.BlockSpec((None, None, s, HEAD_DIM), lambda i, h: (i, h, 0, 0)),
        out_shape=jax.ShapeDtypeStruct((b, DIL_HEADS_PER_GROUP, s, HEAD_DIM), BF16),
        scratch_shapes=[
            pltpu.VMEM((s, HEAD_DIM), F32),
            pltpu.VMEM((s, HEAD_DIM), F32),
            pltpu.VMEM((s, HEAD_DIM), F32),
            pltpu.VMEM((s, HEAD_DIM), F32),
            pltpu.VMEM((s + 2 * HALF_WINDOW, HEAD_DIM), F32),
            pltpu.VMEM((s + 2 * HALF_WINDOW, HEAD_DIM), F32),
            pltpu.VMEM((s + 2 * HALF_WINDOW, HEAD_DIM), BF16),
            pltpu.VMEM((s + 2 * HALF_WINDOW, HEAD_DIM), BF16),
            pltpu.VMEM((3, s, HEAD_DIM), F32),
            pltpu.VMEM((3, s, 1), F32),
            pltpu.VMEM((3, s, 1), F32),
        ],
        compiler_params=pltpu.CompilerParams(dimension_semantics=("parallel", "arbitrary"),
                                             vmem_limit_bytes=VMEM_LIMIT),
        name="dilated_attn",
    )(*([proj] * 9), bias)


def _mix_out_body(x_ref, oa_ref, ob_ref, ga_ref, gb_ref, wa_ref, wb_ref, wo_ref, ln_ref, o_ref):
    wide = lambda ref: jnp.concatenate([ref[c] for c in range(ref.shape[0])], axis=1)
    ya = _mm(wide(oa_ref), wa_ref[...])
    yb = _mm(wide(ob_ref), wb_ref[...])
    merged = _sigmoid(wide(ga_ref).astype(F32)) * ya + _sigmoid(wide(gb_ref).astype(F32)) * yb
    y = _mm(merged.astype(BF16), wo_ref[...])
    o_ref[...] = x_ref[...] + _rms(y, ln_ref[...])


def _mix_out(x3d, o_a, o_b, proj, w_a, w_b, w_o, ln_post, tm):
    b, s, d = x3d.shape
    nslab = d // LANES
    assert CB_GA % nslab == 0 and CB_GB % nslab == 0 and s % tm == 0
    const = lambda shape: pl.BlockSpec(shape, lambda i, j: (0, 0), pipeline_mode=pl.Buffered(1))
    slabs =lambda n, blk: pl.BlockSpec((None, n, tm, LANES), lambda i, j, blk=blk: (i, blk, j, 0))
    return pl.pallas_call(
        _mix_out_body,
        grid=(b, s // tm),
        in_specs=[pl.BlockSpec((None, tm, d), lambda i, j: (i, j, 0)),
                  slabs(GDN_HEADS, 0), slabs(DIL_HEADS_PER_GROUP, 0),
                  slabs(nslab, CB_GA // nslab), slabs(nslab, CB_GB // nslab),
                  const((GDN_WIDTH, d)), const((DIL_OUT_WIDTH, d)), const((d, d)), const((1, d))],
        out_specs=pl.BlockSpec((None, tm, d), lambda i, j: (i, j, 0)),
        out_shape=jax.ShapeDtypeStruct((b, s, d), F32),
        compiler_params=pltpu.CompilerParams(dimension_semantics=("parallel", "parallel"),
                                             vmem_limit_bytes=VMEM_LIMIT),
        name="mix_out",
    )(x3d, o_a, o_b, proj, proj, w_a, w_b, w_o, ln_post)


def _mlp_body(tf, x_ref, g1_ref, w1_ref, w2_ref, g2_ref, o_ref):
    x = x_ref[...]
    h = _rms(x, g1_ref[...]).astype(BF16)
    acc = None
    for k in range(w1_ref.shape[1] // tf):
        cols = pl.ds(k * tf, tf)
        f = jnp.maximum(_mm(h, w1_ref[:, cols]), 0.0)
        part = _mm((f * f).astype(BF16), w2_ref[cols, :])
        acc = part if acc is None else acc + part
    o_ref[...] = x + _rms(acc, g2_ref[...])


def _mlp(x2d, ln_pre, w1, w2, ln_post, tm, tf):
    t, d = x2d.shape
    dff = w1.shape[1]
    resident = lambda shape: pl.BlockSpec(shape, lambda i: (0, 0), pipeline_mode=pl.Buffered(1))
    return pl.pallas_call(
        functools.partial(_mlp_body, tf),
        grid=(t // tm,),
        in_specs=[pl.BlockSpec((tm, d), lambda i: (i, 0)),
                  resident((1, d)), resident((d, dff)), resident((dff, d)), resident((1, d))],
        out_specs=pl.BlockSpec((tm, d), lambda i: (i, 0)),
        out_shape=jax.ShapeDtypeStruct((t, d), F32),
        compiler_params=pltpu.CompilerParams(dimension_semantics=("parallel",),
                                             vmem_limit_bytes=VMEM_LIMIT),
        name="mlp",
    )(x2d, ln_pre, w1, w2, ln_post)


def _t5_bucket_np(rel):
    nb = REL_BUCKETS // 2
    ret = (rel > 0).astype(np.int32) * nb
    n = np.abs(rel)
    max_exact = nb // 2
    large = max_exact + (np.log(np.maximum(n, 1) / max_exact) / math.log(REL_MAX_DIST / max_exact)
                         * (nb - max_exact)).astype(np.int32)
    large = np.minimum(large, nb - 1)
    return ret + np.where(n < max_exact, n, large).astype(np.int32)


def _attention_bias(rel_bias):
    c = np.arange(BIAS_LEN)
    off = np.where(c < KWIN, np.clip(c - HALF_WINDOW, -HALF_WINDOW, HALF_WINDOW), -HALF_WINDOW)
    per_group = []
    for gi, (_, dil) in enumerate(DIL_GROUPS):
        bt = rel_bias[_t5_bucket_np(off * dil)]
        per_group.append(bt[:, gi * DIL_HEADS_PER_GROUP:(gi + 1) * DIL_HEADS_PER_GROUP])
    return jnp.transpose(jnp.stack(per_group, axis=0), (2, 0, 1))[:, :, None, :].astype(F32)


def _gdn_params(a_log_f, a_log_b, dt_bias_f, dt_bias_b):
    pad = lambda f, bk: jnp.pad(jnp.concatenate([f, bk]), (0, LANES - 2 * GDN_HEADS))
    return jnp.stack([pad(a_log_f, a_log_b), pad(dt_bias_f, dt_bias_b)]
                     + [jnp.zeros((LANES,), F32)] * 6, axis=0).astype(F32)


def kernel(x, rel_bias, ln_mix_pre, w_in, conv_w, a_log_f, a_log_b, dt_bias_f, dt_bias_b, norm_a,
           w_branch_a, w_branch_b, w_out, ln_mix_post, ln_mlp_pre, w_ff1, w_ff2, ln_mlp_post):
    b, s, d = x.shape
    t = b * s
    c_small = 4 * GDN_WIDTH
    bias = _attention_bias(rel_bias)
    for l in range(ln_mix_pre.shape[0]):
        w = w_in[l].astype(BF16)
        c_qb = c_small + N_SMALL
        c_ga = c_qb + 3 * DIL_WIDTH
        w_wide = jnp.concatenate([w[:, :c_small], w[:, c_ga:], w[:, c_qb:c_ga]], axis=1)
        w_small = jnp.pad(w[:, c_small:c_qb], ((0, 0), (0, LANES - N_SMALL)))
        proj, small = _in_proj(x, ln_mix_pre[l][None, :], w_wide, w_small, 1792)

        lanepar = _gdn_params(a_log_f[l], a_log_b[l], dt_bias_f[l], dt_bias_b[l])
        o_a = _gdn(proj, small, lanepar, conv_w[l].astype(F32), norm_a[l][None, :].astype(F32))
        o_b = _dilated(proj, bias)

        x1 = _mix_out(x, o_a, o_b, proj,
                      w_branch_a[l].astype(BF16), w_branch_b[l].astype(BF16), w_out[l].astype(BF16),
                      ln_mix_post[l][None, :], 1024)
        x2d = _mlp(x1.reshape(t, d), ln_mlp_pre[l][None, :], w_ff1[l].astype(BF16), w_ff2[l].astype(BF16),
                   ln_mlp_post[l][None, :], 1024, 1024)
        x = x2d.reshape(b, s, d)
    return x
```

```python
import functools
import math

import numpy as np
import jax
import jax.numpy as jnp
from jax import lax
from jax.experimental import pallas as pl
from jax.experimental.pallas import tpu as pltpu

F32 = jnp.float32
BF16 = jnp.bfloat16

LANES = 128
VMEM_LIMIT = 56 * 1024 * 1024

GDN_HEADS = 8
HEAD_DIM = 128
GDN_WIDTH = GDN_HEADS * HEAD_DIM
CHUNK = 64
TRI_BASE = 8
CHUNK_GROUP = 16
GDN_HEADS_PER_STEP = 2
CONV_K = 5
DIL_GROUPS = ((128, 1), (512, 4), (2048, 16))
DIL_HEADS_PER_GROUP = 4
DIL_WIDTH = len(DIL_GROUPS) * DIL_HEADS_PER_GROUP * HEAD_DIM
DIL_OUT_WIDTH = DIL_HEADS_PER_GROUP * HEAD_DIM
HALF_WINDOW = 64
QBLK = 128
KWIN = QBLK + 2 * HALF_WINDOW
ATT_GROUP = 8
BIAS_LEN = 3 * LANES
REL_BUCKETS = 32
REL_MAX_DIST = 1024
EPS = 1e-6
NEG = -1e30
N_SMALL = 4 * GDN_HEADS

CB_QA, CB_KA, CB_VA, CB_ZA = 0, 8, 16, 24
CB_GA, CB_GB = 32, 40
CB_QB, CB_KB, CB_VB = 48, 60, 72


def _mm(a, b):
    return jnp.dot(a, b, preferred_element_type=F32)


def _mm_nt(a, b):
    return lax.dot_general(a, b, (((1,), (1,)), ((), ())), preferred_element_type=F32)


def _rms(x, gain):
    return x * lax.rsqrt(jnp.mean(x * x, axis=-1, keepdims=True) + EPS) * gain


def _sigmoid(x):
    return 0.5 * jnp.tanh(0.5 * x) + 0.5


def _silu(x):
    h = 0.5 * x
    return h + h * jnp.tanh(h)


def _softplus(x):
    return jnp.maximum(x, 0.0) + jnp.log(1.0 + jnp.exp(-jnp.abs(x)))


def _in_proj_body(x_ref, g_ref, w_ref, ws_ref, o_ref, small_ref, h_ref):
    @pl.when(pl.program_id(1) == 0)
    def _():
        h_ref[...] = _rms(x_ref[...], g_ref[...]).astype(BF16)
        small_ref[...] = _mm(h_ref[...], ws_ref[...])

    h = h_ref[...]
    for c in range(0, o_ref.shape[0], 2):
        res = _mm(h, w_ref[:, c * LANES:(c + 2) * LANES])
        o_ref[c] = res[:, :LANES].astype(o_ref.dtype)
        o_ref[c + 1] = res[:, LANES:].astype(o_ref.dtype)


def _in_proj(x3d, gain, w, w_small, tn):
    b, s, d = x3d.shape
    n = w.shape[1]
    return pl.pallas_call(
        _in_proj_body,
        grid=(b, n // tn),
        in_specs=[pl.BlockSpec((None, s, d), lambda i, j: (i, 0, 0)),
                  pl.BlockSpec((1, d), lambda i, j: (0, 0)),
                  pl.BlockSpec((d, tn), lambda i, j: (0, j)),
                  pl.BlockSpec((d, LANES), lambda i, j: (0, 0))],
        out_specs=[pl.BlockSpec((None, tn // LANES, s, LANES), lambda i, j: (i, j, 0, 0)),
                   pl.BlockSpec((None, s, LANES), lambda i, j: (i, 0, 0))],
        out_shape=[jax.ShapeDtypeStruct((b, n // LANES, s, LANES), BF16),
                   jax.ShapeDtypeStruct((b, s, LANES), F32)],
        scratch_shapes=[pltpu.VMEM((s, d), BF16)],
        compiler_params=pltpu.CompilerParams(dimension_semantics=("parallel", "arbitrary"),
                                             vmem_limit_bytes=VMEM_LIMIT),
        name="in_proj",
    )(x3d, gain, w, w_small)


def _split_bf16(x):
    hi = x.astype(BF16)
    return hi, (x - hi.astype(F32)).astype(BF16)


def _split3_lanes(x):
    hi = x.astype(BF16)
    r = x - hi.astype(F32)
    mid = r.astype(BF16)
    lo = (r - mid.astype(F32)).astype(BF16)
    return jnp.concatenate([hi, mid, lo], axis=1)


def _block_diag(x, groups):
    return jnp.concatenate([jnp.where(g, x, 0.0) for g in groups], axis=0)


def _mm_parts(lhs, rhs, masks):
    (lh, ll), (rh, rl) = lhs, rhs
    bd = lambda p: jnp.concatenate([p * m for m in masks], axis=0)
    rh_bd = bd(rh)
    return (_mm(jnp.concatenate([lh, ll], axis=1), jnp.concatenate([rh_bd, rh_bd], axis=0))
            + _mm(lh, bd(rl)))


def _unit_tri_inverse_pairs(a2s, eye, lo, same_blk):
    masks = [jnp.where(g, 1.0, 0.0).astype(BF16) for g in lo]
    stack = lambda a, b: tuple(jnp.concatenate([p, q], axis=0) for p, q in zip(a, b))
    ds = [jnp.where(same_blk[0], a2, 0.0) for a2 in a2s]
    dps = [_split_bf16(d) for d in ds]
    xs = [_mm_parts(dp, dp, masks) for dp in dps]
    ts = [eye - d for d in ds]
    levels = int(math.log2(TRI_BASE)) - 1
    for lvl in range(levels):
        tps = [_split_bf16(t) for t in ts]
        xps = [_split_bf16(x) for x in xs]
        if lvl < levels - 1:
            outs = [_mm_parts(stack(tp, xp), xp, masks) for tp, xp in zip(tps, xps)]
            ts = [t + out[:CHUNK] for t, out in zip(ts, outs)]
            xs = [out[CHUNK:] for out in outs]
        else:
            ts = [t + _mm_parts(tp, xp, masks) for t, tp, xp in zip(ts, tps, xps)]
    inner = same_blk[0]
    for outer in tuple(same_blk[1:]) + (None,):
        between = [jnp.where(inner, 0.0, a2) if outer is None else jnp.where(outer & ~inner, a2, 0.0)
                   for a2 in a2s]
        tps = [_split_bf16(t) for t in ts]
        tls = [_mm_parts(tp, _split_bf16(l), masks) for tp, l in zip(tps, between)]
        ts = [t - _mm_parts(_split_bf16(tl), tp, masks) for t, tl, tp in zip(ts, tls, tps)]
        inner = outer
    return ts


def _gdn_body(lanepar_ref, cwq_ref, cwk_ref, cwv_ref, q_ref, k_ref, v_ref, z_ref, small_ref,
              na_ref, o_ref,
              xp, qn, kn, vn, act, gcf, bcf, mq_s, c_s, dec_s, of_s, ob_s):
    nh, s, _ = q_ref.shape
    nchunk = s // CHUNK
    hp = pl.program_id(1)
    rc = 256

    @pl.when(hp == 0)
    def _():
        lane = lax.broadcasted_iota(jnp.int32, (rc, LANES), 1)
        bi = lax.broadcasted_iota(jnp.int32, (rc, rc), 0)
        bj = lax.broadcasted_iota(jnp.int32, (rc, rc), 1)
        same_chunk = jnp.bitwise_xor(bi, bj) < CHUNK
        cum_f = jnp.where(same_chunk & (bj <= bi), 1.0, 0.0).astype(BF16)
        cum_b = jnp.where(same_chunk & (bj >= bi), 1.0, 0.0).astype(BF16)
        neg_a = -jnp.exp(lanepar_ref[0:1, :])
        dtb = lanepar_ref[1:2, :]
        sum3 = lambda y: y[:, :LANES] + y[:, LANES:2 * LANES] + y[:, 2 * LANES:]
        for c in range(s // rc):
            rows = pl.ds(c * rc, rc)
            slab = small_ref[rows, :]
            g3 = _split3_lanes(neg_a * _softplus(slab + dtb))
            act[rows, :] = jnp.where(lane < GDN_HEADS, sum3(_mm(cum_f, g3)),
                                     jnp.where(lane < 2 * GDN_HEADS, sum3(_mm(cum_b, g3)), _sigmoid(slab)))

    for hh in range(nh):
        lanes = pl.ds(hh * HEAD_DIM, HEAD_DIM)
        _gdn_prepare(hp * nh + hh, cwq_ref.at[:, lanes], cwk_ref.at[:, lanes], cwv_ref.at[:, lanes],
                     q_ref.at[hh], k_ref.at[hh], v_ref.at[hh],
                     xp, qn, kn, vn, act, gcf, bcf,
                     mq_s.at[hh], c_s.at[hh], dec_s.at[hh], of_s.at[hh], ob_s.at[hh])

    def phase2(step, states):
        new = []
        for idx, st in enumerate(states):
            hh, d = divmod(idx, 2)
            n = step if d == 0 else nchunk - 1 - step
            r = _mm(mq_s[hh, d, n], st.astype(BF16))
            rows = pl.ds(pl.multiple_of(n * CHUNK, CHUNK), CHUNK)
            out_s = of_s if d == 0 else ob_s
            out_s[hh, rows, :] = out_s[hh, rows, :] + r[HEAD_DIM:]
            new.append(st * dec_s[hh, d, n] + r[:HEAD_DIM] + c_s[hh, d, n])
        return tuple(new)

    zero_state = jnp.zeros((HEAD_DIM, HEAD_DIM), F32)
    lax.fori_loop(0, nchunk, phase2, (zero_state,) * (2 * nh))

    for hh in range(nh):
        for c in range(s // rc):
            rows = pl.ds(c * rc, rc)
            o = of_s[hh, rows, :] + ob_s[hh, rows, :]
            z = z_ref[hh, rows, :].astype(F32)
            o_ref[hh, rows, :] = (_rms(o, na_ref[...]) * _silu(z)).astype(o_ref.dtype)


def _gdn_prepare(h, cwq_ref, cwk_ref, cwv_ref, q_ref, k_ref, v_ref,
                 xp, qn, kn, vn, act, gcf, bcf, mq_s, c_s, dec_s, of_s, ob_s):
    s = q_ref.shape[0]
    nchunk = s // CHUNK
    rc = 256

    zeros8 = jnp.zeros((8, HEAD_DIM), F32)
    xp[pl.ds(0, 8), :] = zeros8
    xp[pl.ds(8 + s, 8), :] = zeros8

    def conv_into(src_ref, cw_ref, dst_ref, l2, scale):
        xp[pl.ds(8, s), :] = src_ref[...].astype(F32)
        for c in range(s // rc):
            acc = None
            for j in range(CONV_K):
                t = xp[pl.ds(c * rc + 8 - CONV_K // 2 + j, rc), :] * cw_ref[j:j + 1, :]
                acc = t if acc is None else acc + t
            y = _silu(acc)
            if l2:
                y = y * lax.rsqrt(jnp.sum(y * y, axis=-1, keepdims=True) + EPS)
                if scale != 1.0:
                    y = y * scale
            dst_ref[pl.ds(c * rc, rc), :] = y

    conv_into(q_ref, cwq_ref, qn, True, HEAD_DIM ** -0.5)
    conv_into(k_ref, cwk_ref, kn, True, 1.0)
    conv_into(v_ref, cwv_ref, vn, False, 1.0)

    sel_k = lax.broadcasted_iota(jnp.int32, (LANES, 4 * LANES), 0)
    sel_c = lax.broadcasted_iota(jnp.int32, (LANES, 4 * LANES), 1)
    sel = jnp.where(sel_k == (sel_c >> 7) * GDN_HEADS + h, 1.0, 0.0).astype(BF16)
    sel3 = jnp.concatenate([sel, sel, sel], axis=0)
    for c in range(s // rc):
        rows = pl.ds(c * rc, rc)
        picked = _mm(_split3_lanes(act[rows, :]), sel3)
        for d in range(2):
            gcf[d, rows, :] = picked[:, d * LANES:(d + 1) * LANES]
            bcf[d, rows, :] = picked[:, (2 + d) * LANES:(3 + d) * LANES]
    ii = lax.broadcasted_iota(jnp.int32, (CHUNK, 2 * CHUNK), 0)
    lj = lax.broadcasted_iota(jnp.int32, (CHUNK, 2 * CHUNK), 1)
    lo = lj < CHUNK
    hi = lj >= CHUNK
    jj = jnp.where(lo, lj, lj - CHUNK)
    is_diag = ii == jj
    tri2 = (lo & (ii >= jj)) | (hi & (ii <= jj))
    strict2 = (lo & (ii > jj)) | (hi & (ii < jj))
    pair = (lo, hi)
    lo_t = lax.broadcasted_iota(jnp.int32, (2 * CHUNK, 2 * CHUNK), 1) < CHUNK
    pair_t = (lo_t, ~lo_t)
    eye2 = jnp.where(is_diag, 1.0, 0.0)
    blk_bits = range(int(math.log2(TRI_BASE)), int(math.log2(CHUNK)))
    same_blk = tuple((ii >> sh) == (jj >> sh) for sh in blk_bits)

    def phase1(grp, carry):
        ns = [grp * CHUNK_GROUP + i for i in range(CHUNK_GROUP)]
        rws = [pl.ds(pl.multiple_of(n * CHUNK, CHUNK), CHUNK) for n in ns]
        qv = [qn[r, :] for r in rws]
        kv = [kn[r, :] for r in rws]
        kb = [k.astype(BF16) for k in kv]
        grams = [_mm_nt(jnp.concatenate([kb_, q.astype(BF16)], axis=0), jnp.concatenate([kb_, kb_], axis=0))
                 for kb_, q in zip(kb, qv)]
        gcc = [(gcf[0, r, :], gcf[1, r, :]) for r in rws]
        bcc = [(bcf[0, r, :], bcf[1, r, :]) for r in rws]
        gcc2 = [jnp.where(lo, gf, gb) for gf, gb in gcc]
        gcr2 = [jnp.sum(jnp.where(is_diag, g2, 0.0), axis=0, keepdims=True) for g2 in gcc2]
        gams = [jnp.where(tri2, jnp.exp(jnp.where(tri2, g2 - gr, 0.0)), 0.0) for g2, gr in zip(gcc2, gcr2)]
        a2s = [jnp.where(strict2, gram[:CHUNK] * jnp.where(lo, bf, bb) * gam, 0.0)
               for gram, (bf, bb), gam in zip(grams, bcc, gams)]
        t2s = _unit_tri_inverse_pairs(a2s, eye2, pair, same_blk)
        uws = []
        for i, r in enumerate(rws):
            v = vn[r, :]
            (gf, gb), (bf, bb) = gcc[i], bcc[i]
            rhs = jnp.concatenate([jnp.concatenate([v * bf, kv[i] * (bf * jnp.exp(gf))], axis=1),
                                   jnp.concatenate([v * bb, kv[i] * (bb * jnp.exp(gb))], axis=1)], axis=0)
            uws.append(_mm(_block_diag(t2s[i], pair).astype(BF16), rhs.astype(BF16)))
        fins = []
        for i in range(CHUNK_GROUP):
            gf, gb = gcc[i]
            gtot_f = gcr2[i][:, CHUNK - 1:CHUNK]
            gtot_b = gcr2[i][:, CHUNK:CHUNK + 1]
            kg = jnp.concatenate([kv[i] * jnp.exp(gtot_f - gf), kv[i] * jnp.exp(gtot_b - gb)], axis=0)
            lhs = jnp.concatenate([_block_diag(kg.T, pair_t), _block_diag(grams[i][CHUNK:] * gams[i], pair)], axis=0)
            fins.append(_mm(lhs.astype(BF16), uws[i].astype(BF16)))
            dec_s[0, ns[i]] = jnp.broadcast_to(jnp.exp(gtot_f), (1, HEAD_DIM))
            dec_s[1, ns[i]] = jnp.broadcast_to(jnp.exp(gtot_b), (1, HEAD_DIM))
        for i, (n, r) in enumerate(zip(ns, rws)):
            fin = fins[i]
            for d in range(2):
                top = fin[d * HEAD_DIM:(d + 1) * HEAD_DIM]
                bot = fin[2 * HEAD_DIM + d * CHUNK:2 * HEAD_DIM + (d + 1) * CHUNK]
                c_s[d, n] = top[:, :HEAD_DIM]
                mq_s[d, n, pl.ds(0, HEAD_DIM), :] = (-top[:, HEAD_DIM:]).astype(BF16)
                mq_s[d, n, pl.ds(HEAD_DIM, CHUNK), :] = (qv[i] * jnp.exp(gcc[i][d]) - bot[:, HEAD_DIM:]).astype(BF16)
                (of_s if d == 0 else ob_s)[r, :] = bot[:, :HEAD_DIM]
        return carry

    lax.fori_loop(0, nchunk // CHUNK_GROUP, phase1, 0)


def _gdn(proj, small, lanepar, conv_w, norm_a):
    b, _, s, _ = proj.shape
    nchunk = s // CHUNK
    nh = GDN_HEADS_PER_STEP
    assert 2 * CHUNK == LANES == HEAD_DIM and nchunk % CHUNK_GROUP == 0 and GDN_HEADS % nh == 0
    assert all(cb % nh == 0 for cb in (CB_QA, CB_KA, CB_VA, CB_ZA))
    col = lambda cb: pl.BlockSpec((None, nh, s, HEAD_DIM), lambda i, h, cb=cb: (i, cb // nh + h, 0, 0))
    cw = lambda cb: pl.BlockSpec((CONV_K, nh * HEAD_DIM), lambda i, h, cb=cb: (0, cb // nh + h))
    return pl.pallas_call(
        _gdn_body,
        grid=(b, GDN_HEADS // nh),
        in_specs=[pl.BlockSpec((8, LANES), lambda i, h: (0, 0)),
                  cw(0), cw(GDN_HEADS), cw(2 * GDN_HEADS),
                  col(CB_QA), col(CB_KA), col(CB_VA), col(CB_ZA),
                  pl.BlockSpec((None, s, LANES), lambda i, h: (i, 0, 0)),
                  pl.BlockSpec((1, HEAD_DIM), lambda i, h: (0, 0))],
        out_specs=pl.BlockSpec((None, nh, s, HEAD_DIM), lambda i, h: (i, h, 0, 0)),
        out_shape=jax.ShapeDtypeStruct((b, GDN_HEADS, s, HEAD_DIM), BF16),
        scratch_shapes=[
            pltpu.VMEM((s + 16, HEAD_DIM), F32),
            pltpu.VMEM((s, HEAD_DIM), F32),
            pltpu.VMEM((s, HEAD_DIM), F32),
            pltpu.VMEM((s, HEAD_DIM), F32),
            pltpu.VMEM((s, LANES), F32),
            pltpu.VMEM((2, s, LANES), F32),
            pltpu.VMEM((2, s, LANES), F32),
            pltpu.VMEM((nh, 2, nchunk, HEAD_DIM + CHUNK, HEAD_DIM), BF16),
            pltpu.VMEM((nh, 2, nchunk, HEAD_DIM, HEAD_DIM), F32),
            pltpu.VMEM((nh, 2, nchunk, 1, HEAD_DIM), F32),
            pltpu.VMEM((nh, s, HEAD_DIM), F32),
            pltpu.VMEM((nh, s, HEAD_DIM), F32),
        ],
        compiler_params=pltpu.CompilerParams(dimension_semantics=("parallel", "arbitrary"),
                                             vmem_limit_bytes=VMEM_LIMIT),
        name="gdn",
    )(lanepar, conv_w, conv_w, conv_w, proj, proj, proj, proj, small, norm_a)


def _dil_body(q0, k0, v0, q1, k1, v1, q2, k2, v2, bias_ref, o_ref,
              qf, kf, vf, qs, kp, vp, kpb, vpb, acc_s, m_s, l_s):
    s = q0.shape[0]
    scale = HEAD_DIM ** -0.5
    qi = lax.broadcasted_iota(jnp.int32, (QBLK, KWIN), 0)
    kj = lax.broadcasted_iota(jnp.int32, (QBLK, KWIN), 1)
    off = kj - HALF_WINDOW - qi
    band = (off >= -HALF_WINDOW) & (off <= HALF_WINDOW)
    zpad = jnp.zeros((HALF_WINDOW, HEAD_DIM), F32)

    for g, (q_ref, k_ref, v_ref) in enumerate(((q0, k0, v0), (q1, k1, v1), (q2, k2, v2))):
        dil = DIL_GROUPS[g][1]
        length = s // dil
        nblk = length // QBLK
        seg_q = max(length, QBLK)
        seg_k = seg_q + 2 * HALF_WINDOW
        nseg = min(dil, max(1, ATT_GROUP // nblk))
        bpi = min(nblk, ATT_GROUP)
        if dil > 1:
            qf[...] = q_ref[...].astype(F32)
            kf[...] = k_ref[...].astype(F32)
            vf[...] = v_ref[...].astype(F32)
            q_src, k_src, v_src = qs, kp, vp
        else:
            q_src, k_src, v_src = q_ref, kpb, vpb
            kpb[pl.ds(HALF_WINDOW, length), :] = k_ref[...]
            vpb[pl.ds(HALF_WINDOW, length), :] = v_ref[...]
        for c in range(nseg):
            for buf in (k_src, v_src):
                buf[pl.ds(c * seg_k, HALF_WINDOW), :] = zpad.astype(buf.dtype)
                buf[pl.ds(c * seg_k + HALF_WINDOW + length, HALF_WINDOW), :] = zpad.astype(buf.dtype)
        bias = pltpu.roll(jnp.broadcast_to(bias_ref[g], (QBLK, BIAS_LEN)), 0, 1,
                          stride=1, stride_axis=0)[:, :KWIN]

        def stage(c, r, dil=dil, length=length, seg_q=seg_q, seg_k=seg_k):
            if dil == 1:
                return
            cls = pl.ds(r, length, stride=dil)
            qs[pl.ds(c * seg_q, length), :] = qf[cls, :]
            kp[pl.ds(c * seg_k + HALF_WINDOW, length), :] = kf[cls, :]
            vp[pl.ds(c * seg_k + HALF_WINDOW, length), :] = vf[cls, :]

        def attend(items, g=g, dil=dil, length=length, bias=bias, q_src=q_src, k_src=k_src, v_src=v_src):
            logits = []
            for q_off, k_off, j0, r in items:
                kpos = j0 - HALF_WINDOW + kj
                ok = band & (kpos >= 0) & (kpos < length)
                sc = _mm_nt(q_src[pl.ds(q_off, QBLK), :].astype(BF16), k_src[pl.ds(k_off, KWIN), :].astype(BF16))
                logits.append(jnp.where(ok, sc * scale + bias, NEG))
            ms = [jnp.max(lg, axis=-1, keepdims=True) for lg in logits]
            ps = [jnp.exp(lg - m) for lg, m in zip(logits, ms)]
            ls = [jnp.sum(p, axis=-1, keepdims=True) for p in ps]
            accs = [_mm(p.astype(BF16), v_src[pl.ds(it[1], KWIN), :].astype(BF16)) for p, it in zip(ps, items)]
            for (q_off, k_off, j0, r), m, lsum, acc in zip(items, ms, ls, accs):
                tok = pl.ds(r + dil * j0, QBLK, stride=dil) if dil > 1 else pl.ds(j0, QBLK)
                acc_s[g, tok, :] = acc
                m_s[g, tok, :] = m
                l_s[g, tok, :] = lsum

        def classes(it, carry, nblk=nblk, nseg=nseg, bpi=bpi, seg_q=seg_q, seg_k=seg_k,
                    stage=stage, attend=attend):
            rs = [it * nseg + c for c in range(nseg)]
            for c, r in enumerate(rs):
                stage(c, r)

            def blocks(jt, carry2):
                items = []
                for c, r in enumerate(rs):
                    for i in range(bpi):
                        j0 = (jt * bpi + i) * QBLK
                        if not isinstance(j0, int):
                            j0 = pl.multiple_of(j0, QBLK)
                        items.append((c * seg_q + j0, c * seg_k + j0, j0, r))
                attend(items)
                return carry2

            if nblk == bpi:
                blocks(0, 0)
            else:
                lax.fori_loop(0, nblk // bpi, blocks, 0)
            return carry

        lax.fori_loop(0, dil // nseg, classes, 0)

    rc = 256
    for c in range(s // rc):
        rows = pl.ds(c * rc, rc)
        ms = [m_s[g, rows, :] for g in range(3)]
        mx = jnp.maximum(jnp.maximum(ms[0], ms[1]), ms[2])
        ws = [jnp.exp(ms[g] - mx) for g in range(3)]
        den = ws[0] * l_s[0, rows, :] + ws[1] * l_s[1, rows, :] + ws[2] * l_s[2, rows, :]
        num = ws[0] * acc_s[0, rows, :] + ws[1] * acc_s[1, rows, :] + ws[2] * acc_s[2, rows, :]
        o_ref[rows, :] = (num * (1.0 / den)).astype(o_ref.dtype)


def _dilated(proj, bias):
    b, _, s, _ = proj.shape
    specs = []
    for g in range(3):
        for cb in (CB_QB, CB_KB, CB_VB):
            specs.append(pl.BlockSpec((None, None, s, HEAD_DIM),
                                      lambda i, h, cb=cb, g=g: (i, cb + g * DIL_HEADS_PER_GROUP + h, 0, 0)))
    specs.append(pl.BlockSpec((None, 3, 1, BIAS_LEN), lambda i, h: (h, 0, 0, 0)))
    return pl.pallas_call(
        _dil_body,
        grid=(b, DIL_HEADS_PER_GROUP),
        in_specs=specs,
        out_specs=pl.BlockSpec((None, None, s, HEAD_DIM), lambda i, h: (i, h, 0, 0)),
        out_shape=jax.ShapeDtypeStruct((b, DIL_HEADS_PER_GROUP, s, HEAD_DIM), BF16),
        scratch_shapes=[
            pltpu.VMEM((s, HEAD_DIM), F32),
            pltpu.VMEM((s, HEAD_DIM), F32),
            pltpu.VMEM((s, HEAD_DIM), F32),
            pltpu.VMEM((s, HEAD_DIM), F32),
            pltpu.VMEM((s + 2 * HALF_WINDOW, HEAD_DIM), F32),
            pltpu.VMEM((s + 2 * HALF_WINDOW, HEAD_DIM), F32),
            pltpu.VMEM((s + 2 * HALF_WINDOW, HEAD_DIM), BF16),
            pltpu.VMEM((s + 2 * HALF_WINDOW, HEAD_DIM), BF16),
            pltpu.VMEM((3, s, HEAD_DIM), F32),
            pltpu.VMEM((3, s, 1), F32),
            pltpu.VMEM((3, s, 1), F32),
        ],
        compiler_params=pltpu.CompilerParams(dimension_semantics=("parallel", "arbitrary"),
                                             vmem_limit_bytes=VMEM_LIMIT),
        name="dilated_attn",
    )(*([proj] * 9), bias)


def _mix_out_body(x_ref, oa_ref, ob_ref, ga_ref, gb_ref, wa_ref, wb_ref, wo_ref, ln_ref, o_ref):
    wide = lambda ref: jnp.concatenate([ref[c] for c in range(ref.shape[0])], axis=1)
    ya = _mm(wide(oa_ref), wa_ref[...])
    yb = _mm(wide(ob_ref), wb_ref[...])
    merged = _sigmoid(wide(ga_ref).astype(F32)) * ya + _sigmoid(wide(gb_ref).astype(F32)) * yb
    y = _mm(merged.astype(BF16), wo_ref[...])
    o_ref[...] = x_ref[...] + _rms(y, ln_ref[...])


def _mix_out(x3d, o_a, o_b, proj, w_a, w_b, w_o, ln_post, tm):
    b, s, d = x3d.shape
    nslab = d // LANES
    assert CB_GA % nslab == 0 and CB_GB % nslab == 0 and s % tm == 0
    const = lambda shape: pl.BlockSpec(shape, lambda i, j: (0, 0), pipeline_mode=pl.Buffered(1))
    slabs =lambda n, blk: pl.BlockSpec((None, n, tm, LANES), lambda i, j, blk=blk: (i, blk, j, 0))
    return pl.pallas_call(
        _mix_out_body,
        grid=(b, s // tm),
        in_specs=[pl.BlockSpec((None, tm, d), lambda i, j: (i, j, 0)),
                  slabs(GDN_HEADS, 0), slabs(DIL_HEADS_PER_GROUP, 0),
                  slabs(nslab, CB_GA // nslab), slabs(nslab, CB_GB // nslab),
                  const((GDN_WIDTH, d)), const((DIL_OUT_WIDTH, d)), const((d, d)), const((1, d))],
        out_specs=pl.BlockSpec((None, tm, d), lambda i, j: (i, j, 0)),
        out_shape=jax.ShapeDtypeStruct((b, s, d), F32),
        compiler_params=pltpu.CompilerParams(dimension_semantics=("parallel", "parallel"),
                                             vmem_limit_bytes=VMEM_LIMIT),
        name="mix_out",
    )(x3d, o_a, o_b, proj, proj, w_a, w_b, w_o, ln_post)


def _mlp_body(tf, x_ref, g1_ref, w1_ref, w2_ref, g2_ref, o_ref):
    x = x_ref[...]
    h = _rms(x, g1_ref[...]).astype(BF16)
    acc = None
    for k in range(w1_ref.shape[1] // tf):
        cols = pl.ds(k * tf, tf)
        f = jnp.maximum(_mm(h, w1_ref[:, cols]), 0.0)
        part = _mm((f * f).astype(BF16), w2_ref[cols, :])
        acc = part if acc is None else acc + part
    o_ref[...] = x + _rms(acc, g2_ref[...])


def _mlp(x2d, ln_pre, w1, w2, ln_post, tm, tf):
    t, d = x2d.shape
    dff = w1.shape[1]
    resident = lambda shape: pl.BlockSpec(shape, lambda i: (0, 0), pipeline_mode=pl.Buffered(1))
    return pl.pallas_call(
        functools.partial(_mlp_body, tf),
        grid=(t // tm,),
        in_specs=[pl.BlockSpec((tm, d), lambda i: (i, 0)),
                  resident((1, d)), resident((d, dff)), resident((dff, d)), resident((1, d))],
        out_specs=pl.BlockSpec((tm, d), lambda i: (i, 0)),
        out_shape=jax.ShapeDtypeStruct((t, d), F32),
        compiler_params=pltpu.CompilerParams(dimension_semantics=("parallel",),
                                             vmem_limit_bytes=VMEM_LIMIT),
        name="mlp",
    )(x2d, ln_pre, w1, w2, ln_post)


def _mix_mlp_body(tf, x_ref, oa_ref, ob_ref, ga_ref, gb_ref, wa_ref, wb_ref, wo_ref, ln_ref,
                  g1_ref, w1_ref, w2_ref, g2_ref, o_ref):
    wide = lambda ref: jnp.concatenate([ref[c] for c in range(ref.shape[0])], axis=1)
    ya = _mm(wide(oa_ref), wa_ref[...])
    yb = _mm(wide(ob_ref), wb_ref[...])
    merged = _sigmoid(wide(ga_ref).astype(F32)) * ya + _sigmoid(wide(gb_ref).astype(F32)) * yb
    x1 = x_ref[...] + _rms(_mm(merged.astype(BF16), wo_ref[...]), ln_ref[...])
    h = _rms(x1, g1_ref[...]).astype(BF16)
    acc = None
    for k in range(w1_ref.shape[1] // tf):
        cols = pl.ds(k * tf, tf)
        f = jnp.maximum(_mm(h, w1_ref[:, cols]), 0.0)
        part = _mm((f * f).astype(BF16), w2_ref[cols, :])
        acc = part if acc is None else acc + part
    o_ref[...] = x1 + _rms(acc, g2_ref[...])


def _mix_mlp(x3d, o_a, o_b, proj, w_a, w_b, w_o, ln_post, ln_pre, w1, w2, ln_post2, tm, tf):
    b, s, d = x3d.shape
    dff = w1.shape[1]
    nslab = d // LANES
    assert CB_GA % nslab == 0 and CB_GB % nslab == 0 and s % tm == 0
    const = lambda shape: pl.BlockSpec(shape, lambda i, j: (0, 0), pipeline_mode=pl.Buffered(1))
    slabs = lambda n, blk: pl.BlockSpec((None, n, tm, LANES), lambda i, j, blk=blk: (i, blk, j, 0))
    return pl.pallas_call(
        functools.partial(_mix_mlp_body, tf),
        grid=(b, s // tm),
        in_specs=[pl.BlockSpec((None, tm, d), lambda i, j: (i, j, 0)),
                  slabs(GDN_HEADS, 0), slabs(DIL_HEADS_PER_GROUP, 0),
                  slabs(nslab, CB_GA // nslab), slabs(nslab, CB_GB // nslab),
                  const((GDN_WIDTH, d)), const((DIL_OUT_WIDTH, d)), const((d, d)), const((1, d)),
                  const((1, d)), const((d, dff)), const((dff, d)), const((1, d))],
        out_specs=pl.BlockSpec((None, tm, d), lambda i, j: (i, j, 0)),
        out_shape=jax.ShapeDtypeStruct((b, s, d), F32),
        compiler_params=pltpu.CompilerParams(dimension_semantics=("parallel", "parallel"),
                                             vmem_limit_bytes=VMEM_LIMIT),
        name="mix_mlp",
    )(x3d, o_a, o_b, proj, proj, w_a, w_b, w_o, ln_post, ln_pre, w1, w2, ln_post2)


def _t5_bucket_np(rel):
    nb = REL_BUCKETS // 2
    ret = (rel > 0).astype(np.int32) * nb
    n = np.abs(rel)
    max_exact = nb // 2
    large = max_exact + (np.log(np.maximum(n, 1) / max_exact) / math.log(REL_MAX_DIST / max_exact)
                         * (nb - max_exact)).astype(np.int32)
    large = np.minimum(large, nb - 1)
    return ret + np.where(n < max_exact, n, large).astype(np.int32)


def _attention_bias(rel_bias):
    c = np.arange(BIAS_LEN)
    off = np.where(c < KWIN, np.clip(c - HALF_WINDOW, -HALF_WINDOW, HALF_WINDOW), -HALF_WINDOW)
    per_group = []
    for gi, (_, dil) in enumerate(DIL_GROUPS):
        bt = rel_bias[_t5_bucket_np(off * dil)]
        per_group.append(bt[:, gi * DIL_HEADS_PER_GROUP:(gi + 1) * DIL_HEADS_PER_GROUP])
    return jnp.transpose(jnp.stack(per_group, axis=0), (2, 0, 1))[:, :, None, :].astype(F32)


def _gdn_params(a_log_f, a_log_b, dt_bias_f, dt_bias_b):
    pad = lambda f, bk: jnp.pad(jnp.concatenate([f, bk]), (0, LANES - 2 * GDN_HEADS))
    return jnp.stack([pad(a_log_f, a_log_b), pad(dt_bias_f, dt_bias_b)]
                     + [jnp.zeros((LANES,), F32)] * 6, axis=0).astype(F32)


def kernel(x, rel_bias, ln_mix_pre, w_in, conv_w, a_log_f, a_log_b, dt_bias_f, dt_bias_b, norm_a,
           w_branch_a, w_branch_b, w_out, ln_mix_post, ln_mlp_pre, w_ff1, w_ff2, ln_mlp_post):
    b, s, d = x.shape
    t = b * s
    c_small = 4 * GDN_WIDTH
    bias = _attention_bias(rel_bias)
    for l in range(ln_mix_pre.shape[0]):
        w = w_in[l].astype(BF16)
        c_qb = c_small + N_SMALL
        c_ga = c_qb + 3 * DIL_WIDTH
        w_wide = jnp.concatenate([w[:, :c_small], w[:, c_ga:], w[:, c_qb:c_ga]], axis=1)
        w_small = jnp.pad(w[:, c_small:c_qb], ((0, 0), (0, LANES - N_SMALL)))
        proj, small = _in_proj(x, ln_mix_pre[l][None, :], w_wide, w_small, 1536)

        lanepar = _gdn_params(a_log_f[l], a_log_b[l], dt_bias_f[l], dt_bias_b[l])
        o_a = _gdn(proj, small, lanepar, conv_w[l].astype(F32), norm_a[l][None, :].astype(F32))
        o_b = _dilated(proj, bias)

        x = _mix_mlp(x, o_a, o_b, proj,
                     w_branch_a[l].astype(BF16), w_branch_b[l].astype(BF16), w_out[l].astype(BF16),
                     ln_mix_post[l][None, :], ln_mlp_pre[l][None, :], w_ff1[l].astype(BF16),
                     w_ff2[l].astype(BF16), ln_mlp_post[l][None, :], 512, 1024)
    return x
```
